```python
import jax, jax.numpy as jnp
from jax import lax
import numpy as np

D_MODEL = 2048
BATCH = 4
SEQ = 2048
DEPTH = 2

GRID_W = 64
WIN_H_MAX = 8
WIN_W = 16
A_HEAD_DIM = 128
D_A = D_MODEL // 2
A_HEADS = D_A // A_HEAD_DIM
CHUNK = 128
D_B = D_MODEL // 4
B_GROUPS = 4
B_GROUP_DIM = D_B // B_GROUPS
D_C = D_MODEL // 4
CONV_W = 31
D_MIX = D_A + D_B + D_C
SPLIT_SIZES = [D_A] * 4 + [D_B] * 3 + [D_C] * 3
D_IN = sum(SPLIT_SIZES)
EPS = 1e-6

kernel_name = 'hybrid_natten_gmlp_conformer_encoder'


def rms_norm(x, g):
    xf = x.astype(jnp.float32)
    y = xf * lax.rsqrt(jnp.mean(xf * xf, axis=-1, keepdims=True) + EPS)
    return (y * g.astype(jnp.float32)).astype(x.dtype)


def layer_norm(x, g, b):
    xf = x.astype(jnp.float32)
    mu = jnp.mean(xf, axis=-1, keepdims=True)
    xc = xf - mu
    var = jnp.mean(xc * xc, axis=-1, keepdims=True)
    y = xc * lax.rsqrt(var + EPS) * g.astype(jnp.float32) + b.astype(jnp.float32)
    return y.astype(x.dtype)


def neighbourhood_attention(q, k, v, rpb):
    bsz, t, _ = q.shape
    rows = t // GRID_W
    kh = min(WIN_H_MAX, rows)

    def to_grid(z):
        return z.reshape(bsz, rows, GRID_W, A_HEADS, A_HEAD_DIM).transpose(0, 3, 1, 2, 4)

    qg = to_grid(q) * (A_HEAD_DIM ** -0.5)
    kg = to_grid(k)
    vg = to_grid(v)
    cols = np.arange(GRID_W)
    col_start = np.clip(cols - WIN_W // 2, 0, GRID_W - WIN_W)
    col_idx = col_start[:, None] + np.arange(WIN_W)[None, :]
    dc = col_idx - cols[:, None] + (WIN_W - 1)
    rpb_f = rpb.astype(jnp.float32)

    def one_row(r):
        rs = jnp.clip(r - kh // 2, 0, rows - kh)
        k_rows = lax.dynamic_slice_in_dim(kg, rs, kh, axis=2)
        v_rows = lax.dynamic_slice_in_dim(vg, rs, kh, axis=2)
        k_win = k_rows[:, :, :, col_idx, :]
        v_win = v_rows[:, :, :, col_idx, :]
        q_row = lax.dynamic_index_in_dim(qg, r, axis=2, keepdims=False)
        s = jnp.einsum('bhwd,bhiwjd->bhwij', q_row, k_win).astype(jnp.float32)
        dr = rs + jnp.arange(kh) - r + (WIN_H_MAX - 1)
        bias = rpb_f[:, dr][:, :, dc].transpose(0, 2, 1, 3)
        s = s + bias[None]
        p = jax.nn.softmax(s.reshape(bsz, A_HEADS, GRID_W, kh * WIN_W), axis=-1)
        p = p.reshape(bsz, A_HEADS, GRID_W, kh, WIN_W).astype(v.dtype)
        return jnp.einsum('bhwij,bhiwjd->bhwd', p, v_win)

    out = lax.map(one_row, jnp.arange(rows))
    return out.transpose(1, 0, 3, 2, 4).reshape(bsz, t, D_A)


def spatial_gating(u, v, ln_g, ln_b, w_s, b_s):
    bsz, t, _ = u.shape
    u = jax.nn.gelu(u, approximate=False)
    v = layer_norm(jax.nn.gelu(v, approximate=False), ln_g, ln_b)
    vc = v.reshape(bsz, t // CHUNK, CHUNK, B_GROUPS, B_GROUP_DIM)
    s = jnp.einsum('gts,bnsgc->bntgc', w_s, vc) + b_s.T[None, None, :, :, None]
    return u * s.reshape(bsz, t, D_B)


def conformer_conv(a, b, conv_w, conv_b, ln_g, ln_b, pw_w, pw_b):
    h = a * jax.nn.sigmoid(b)
    h = lax.conv_general_dilated(
        h, conv_w, window_strides=(1,), padding=[(CONV_W // 2, CONV_W // 2)],
        dimension_numbers=('NWC', 'WIO', 'NWC'), feature_group_count=D_C) + conv_b
    h = jax.nn.silu(layer_norm(h, ln_g, ln_b))
    return h @ pw_w + pw_b


def setup_inputs(seed: int = 0) -> dict:
    key = jax.random.key(seed)
    ks = jax.random.split(key, 16)
    f32 = jnp.float32
    nrm = lambda k, shape, s: jax.random.normal(k, shape, f32) * s
    return {
        'x': nrm(ks[0], (BATCH, SEQ, D_MODEL), 1.0),
        'pre_norm_g': 1.0 + nrm(ks[1], (DEPTH, D_MODEL), 0.02),
        'w_in': nrm(ks[2], (DEPTH, D_MODEL, D_IN), D_MODEL ** -0.5),
        'attn_rpb': nrm(ks[3], (DEPTH, A_HEADS, 2 * WIN_H_MAX - 1, 2 * WIN_W - 1), 0.1),
        'sgu_ln_g': 1.0 + nrm(ks[4], (DEPTH, D_B), 0.02),
        'sgu_ln_b': nrm(ks[5], (DEPTH, D_B), 0.01),
        'sgu_w': nrm(ks[6], (DEPTH, B_GROUPS, CHUNK, CHUNK), CHUNK ** -0.5),
        'sgu_b': 1.0 + nrm(ks[7], (DEPTH, B_GROUPS, CHUNK), 0.01),
        'conv_w': nrm(ks[8], (DEPTH, CONV_W, 1, D_C), CONV_W ** -0.5),
        'conv_b': nrm(ks[9], (DEPTH, D_C), 0.01),
        'conv_ln_g': 1.0 + nrm(ks[10], (DEPTH, D_C), 0.02),
        'conv_ln_b': nrm(ks[11], (DEPTH, D_C), 0.01),
        'conv_pw_w': nrm(ks[12], (DEPTH, D_C, D_C), D_C ** -0.5),
        'conv_pw_b': nrm(ks[13], (DEPTH, D_C), 0.01),
        'w_out': nrm(ks[14], (DEPTH, D_MIX, D_MODEL), D_MIX ** -0.5),
        'post_norm_g': 1.0 + nrm(ks[15], (DEPTH, D_MODEL), 0.02),
    }


def reference(x, pre_norm_g, w_in, attn_rpb, sgu_ln_g, sgu_ln_b, sgu_w, sgu_b,
              conv_w, conv_b, conv_ln_g, conv_ln_b, conv_pw_w, conv_pw_b,
              w_out, post_norm_g):
    offsets = np.cumsum(SPLIT_SIZES)[:-1].tolist()
    for l in range(DEPTH):
        h = rms_norm(x, pre_norm_g[l])
        z = h @ w_in[l]
        q, k, v, g_a, u_b, v_b, g_b, a_c, b_c, g_c = jnp.split(z, offsets, axis=-1)
        y_a = neighbourhood_attention(q, k, v, attn_rpb[l]) * jax.nn.silu(g_a)
        y_b = spatial_gating(u_b, v_b, sgu_ln_g[l], sgu_ln_b[l], sgu_w[l], sgu_b[l]) * jax.nn.silu(g_b)
        y_c = conformer_conv(a_c, b_c, conv_w[l], conv_b[l], conv_ln_g[l], conv_ln_b[l],
                             conv_pw_w[l], conv_pw_b[l]) * jax.nn.silu(g_c)
        y = jnp.concatenate([y_a, y_b, y_c], axis=-1) @ w_out[l]
        x = x + rms_norm(y, post_norm_g[l])
    return x
```

```python
import functools
import math

import numpy as np
import jax
import jax.numpy as jnp
from jax import lax
from jax.experimental import pallas as pl
from jax.experimental.pallas import tpu as pltpu

D_MODEL = 2048
GRID_W = 64
WIN_H = 8
WIN_W = 16
HEAD_DIM = 128
D_A = 1024
N_HEADS = 8
CHUNK = 128
D_B = 512
B_GROUPS = 4
D_C = 512
CONV_W = 31
D_MIX = D_A + D_B + D_C
D_IN = 4 * D_A + 3 * D_B + 3 * D_C
EPS = 1e-6
MASKED = -1e30

CONV_PAD = 16
VMEM_LIMIT = 56 * 1024 * 1024

f32 = jnp.float32
bf16 = jnp.bfloat16


def _sigmoid(x):
    return 1.0 / (1.0 + jnp.exp(-x))


def _silu(x):
    return x * _sigmoid(x)


def _gelu(x):
    return 0.5 * x * (1.0 + lax.erf(x * math.sqrt(0.5)))


def _layer_norm(x, g, b):
    mu = jnp.mean(x, axis=-1, keepdims=True)
    xc = x - mu
    var = jnp.mean(xc * xc, axis=-1, keepdims=True)
    return xc * lax.rsqrt(var + EPS) * g + b


def _in_proj_kernel(x_ref, g_ref, w_ref, z_ref, h_scr):
    @pl.when(pl.program_id(1) == 0)
    def _():
        x = x_ref[...]
        ms = jnp.mean(x * x, axis=-1, keepdims=True)
        h_scr[...] = (x * lax.rsqrt(ms + EPS) * g_ref[...]).astype(bf16)

    z_ref[...] = jnp.dot(h_scr[...], w_ref[...], preferred_element_type=f32).astype(z_ref.dtype)


def _in_proj(x2, g, w, tm=1024, tn=1024):
    n_tok = x2.shape[0]
    return pl.pallas_call(
        _in_proj_kernel,
        grid=(n_tok // tm, D_IN // tn),
        in_specs=[
            pl.BlockSpec((tm, D_MODEL), lambda i, j: (i, 0)),
            pl.BlockSpec((1, D_MODEL), lambda i, j: (0, 0)),
            pl.BlockSpec((D_MODEL, tn), lambda i, j: (0, j)),
        ],
        out_specs=pl.BlockSpec((tm, tn), lambda i, j: (i, j)),
        out_shape=jax.ShapeDtypeStruct((n_tok, D_IN), bf16),
        scratch_shapes=[pltpu.VMEM((tm, D_MODEL), bf16)],
        compiler_params=pltpu.CompilerParams(
            dimension_semantics=("arbitrary", "arbitrary"), vmem_limit_bytes=VMEM_LIMIT),
        name="in_proj",
    )(x2, g, w)


def _attn_kernel(q_ref, k_ref, v_ref, g_ref, bias_ref, o_ref, *, rows):
    scale = HEAD_DIM ** -0.5
    n_win = WIN_H * GRID_W

    def body(r, carry):
        rs = jnp.clip(r - WIN_H // 2, 0, rows - WIN_H)
        d0 = rs - r + (WIN_H - 1)
        q0 = pl.multiple_of(r * GRID_W, GRID_W)
        k0 = pl.multiple_of(rs * GRID_W, GRID_W)
        q = q_ref[0, pl.ds(q0, GRID_W), :]
        kw = k_ref[0, pl.ds(k0, n_win), :]
        vw = v_ref[0, pl.ds(k0, n_win), :]
        s = lax.dot_general(q, kw, (((1,), (1,)), ((), ())), preferred_element_type=f32)
        s = s * scale + bias_ref[0, d0]
        m = jnp.max(s, axis=-1, keepdims=True)
        p = jnp.exp(s - m)
        l = jnp.sum(p, axis=-1, keepdims=True)
        o = jnp.dot(p.astype(bf16), vw, preferred_element_type=f32) * (1.0 / l)
        g = g_ref[0, pl.ds(q0, GRID_W), :].astype(f32)
        o_ref[0, pl.ds(q0, GRID_W), :] = (o * _silu(g)).astype(o_ref.dtype)
        return carry

    lax.fori_loop(0, rows, body, 0)


def _attention(z3, bias):
    bsz, t, _ = z3.shape
    rows = t // GRID_W
    blk = (1, t, HEAD_DIM)
    return pl.pallas_call(
        functools.partial(_attn_kernel, rows=rows),
        grid=(bsz, N_HEADS),
        in_specs=[
            pl.BlockSpec(blk, lambda b, h: (b, 0, h)),
            pl.BlockSpec(blk, lambda b, h: (b, 0, N_HEADS + h)),
            pl.BlockSpec(blk, lambda b, h: (b, 0, 2 * N_HEADS + h)),
            pl.BlockSpec(blk, lambda b, h: (b, 0, 3 * N_HEADS + h)),
            pl.BlockSpec((1, WIN_H, GRID_W, WIN_H * GRID_W), lambda b, h: (h, 0, 0, 0)),
        ],
        out_specs=pl.BlockSpec(blk, lambda b, h: (b, 0, h)),
        out_shape=jax.ShapeDtypeStruct((bsz, t, D_A), bf16),
        compiler_params=pltpu.CompilerParams(
            dimension_semantics=("arbitrary", "arbitrary"), vmem_limit_bytes=VMEM_LIMIT),
        name="nbr_attention",
    )(z3, z3, z3, z3, bias)


def _attention_bias(rpb):
    cols = np.arange(GRID_W)
    col_start = np.clip(cols - WIN_W // 2, 0, GRID_W - WIN_W)
    kc = cols[None, :]
    in_win = (kc >= col_start[:, None]) & (kc < col_start[:, None] + WIN_W)
    dc = np.clip(kc - cols[:, None] + (WIN_W - 1), 0, 2 * WIN_W - 2)
    t = jnp.where(in_win[None, None], rpb.astype(f32)[:, :, dc], MASKED)
    return jnp.stack(
        [jnp.concatenate([t[:, d0 + i] for i in range(WIN_H)], axis=-1) for d0 in range(WIN_H)], axis=1)


def _sgu_kernel(u_ref, v_ref, g_ref, lng_ref, lnb_ref, ws_ref, bs_ref, o_ref, *, tm):
    gd = D_B // B_GROUPS
    v = _layer_norm(_gelu(v_ref[...].astype(f32)), lng_ref[...], lnb_ref[...]).astype(bf16)
    for n in range(tm // CHUNK):
        rows = slice(n * CHUNK, (n + 1) * CHUNK)
        for grp in range(B_GROUPS):
            cols = slice(grp * gd, (grp + 1) * gd)
            s = jnp.dot(ws_ref[grp], v[rows, cols], preferred_element_type=f32) + bs_ref[:, cols]
            u = _gelu(u_ref[rows, cols].astype(f32))
            g = g_ref[rows, cols].astype(f32)
            o_ref[rows, cols] = (u * s * _silu(g)).astype(o_ref.dtype)


def _sgu(z2, ln_g, ln_b, w_s, b_map, tm=512):
    n_tok = z2.shape[0]
    c0 = 4 * D_A // D_B
    blk = (tm, D_B)
    const2 = lambda i: (0, 0)
    return pl.pallas_call(
        functools.partial(_sgu_kernel, tm=tm),
        grid=(n_tok // tm,),
        in_specs=[
            pl.BlockSpec(blk, lambda i: (i, c0)),
            pl.BlockSpec(blk, lambda i: (i, c0 + 1)),
            pl.BlockSpec(blk, lambda i: (i, c0 + 2)),
            pl.BlockSpec((1, D_B), const2),
            pl.BlockSpec((1, D_B), const2),
            pl.BlockSpec((B_GROUPS, CHUNK, CHUNK), lambda i: (0, 0, 0)),
            pl.BlockSpec((CHUNK, D_B), const2),
        ],
        out_specs=pl.BlockSpec(blk, lambda i: (i, 0)),
        out_shape=jax.ShapeDtypeStruct((n_tok, D_B), bf16),
        compiler_params=pltpu.CompilerParams(
            dimension_semantics=("arbitrary",), vmem_limit_bytes=VMEM_LIMIT),
        name="sgu",
    )(z2, z2, z2, ln_g, ln_b, w_s, b_map)


def _conv_kernel(a_ref, b_ref, g_ref, cw_ref, cb_ref, lng_ref, lnb_ref, pw_ref, pb_ref, o_ref,
                 h_scr, c_scr, *, t, tt):
    zeros = jnp.zeros((CONV_PAD, D_C), f32)
    h_scr[pl.ds(0, CONV_PAD), :] = zeros
    h_scr[pl.ds(CONV_PAD + t, CONV_PAD), :] = zeros
    h_scr[pl.ds(CONV_PAD, t), :] = a_ref[0].astype(f32) * _sigmoid(b_ref[0].astype(f32))

    shift = CONV_PAD - CONV_W // 2
    for ti in range(t // tt):
        for cb in range(D_C // 128):
            cols = slice(cb * 128, (cb + 1) * 128)
            acc = jnp.zeros((tt, 128), f32)
            for k in range(CONV_W):
                acc = acc + cw_ref[k:k + 1, cols] * h_scr[pl.ds(ti * tt + k + shift, tt), cols]
            c_scr[pl.ds(ti * tt, tt), cols] = acc

    c = c_scr[...] + cb_ref[...]
    c = _silu(_layer_norm(c, lng_ref[...], lnb_ref[...])).astype(bf16)
    y = jnp.dot(c, pw_ref[...], preferred_element_type=f32) + pb_ref[...]
    o_ref[0] = (y * _silu(g_ref[0].astype(f32))).astype(o_ref.dtype)


def _conformer_conv(z3, conv_w, conv_b, ln_g, ln_b, pw_w, pw_b, tt=64):
    bsz, t, _ = z3.shape
    c0 = (4 * D_A + 3 * D_B) // D_C
    blk = (1, t, D_C)
    const2 = lambda b: (0, 0)
    return pl.pallas_call(
        functools.partial(_conv_kernel, t=t, tt=tt),
        grid=(bsz,),
        in_specs=[
            pl.BlockSpec(blk, lambda b: (b, 0, c0)),
            pl.BlockSpec(blk, lambda b: (b, 0, c0 + 1)),
            pl.BlockSpec(blk, lambda b: (b, 0, c0 + 2)),
            pl.BlockSpec((CONV_W, D_C), const2),
            pl.BlockSpec((1, D_C), const2),
            pl.BlockSpec((1, D_C), const2),
            pl.BlockSpec((1, D_C), const2),
            pl.BlockSpec((D_C, D_C), const2),
            pl.BlockSpec((1, D_C), const2),
        ],
        out_specs=pl.BlockSpec(blk, lambda b: (b, 0, 0)),
        out_shape=jax.ShapeDtypeStruct((bsz, t, D_C), bf16),
        scratch_shapes=[pltpu.VMEM((t + 2 * CONV_PAD, D_C), f32), pltpu.VMEM((t, D_C), f32)],
        compiler_params=pltpu.CompilerParams(
            dimension_semantics=("arbitrary",), vmem_limit_bytes=VMEM_LIMIT),
        name="conformer_conv",
    )(z3, z3, z3, conv_w, conv_b, ln_g, ln_b, pw_w, pw_b)


def _out_proj_kernel(ya_ref, yb_ref, yc_ref, w_ref, g_ref, x_ref, o_ref, cat_scr):
    cat_scr[:, 0:D_A] = ya_ref[...]
    cat_scr[:, D_A:D_A + D_B] = yb_ref[...]
    cat_scr[:, D_A + D_B:D_MIX] = yc_ref[...]
    y = jnp.dot(cat_scr[...], w_ref[...], preferred_element_type=f32)
    ms = jnp.mean(y * y, axis=-1, keepdims=True)
    o_ref[...] = x_ref[...] + y * lax.rsqrt(ms + EPS) * g_ref[...]


def _out_proj(ya, yb, yc, w, g, x2, tm=512):
    n_tok = x2.shape[0]
    const2 = lambda i: (0, 0)
    return pl.pallas_call(
        _out_proj_kernel,
        grid=(n_tok // tm,),
        in_specs=[
            pl.BlockSpec((tm, D_A), lambda i: (i, 0)),
            pl.BlockSpec((tm, D_B), lambda i: (i, 0)),
            pl.BlockSpec((tm, D_C), lambda i: (i, 0)),
            pl.BlockSpec((D_MIX, D_MODEL), const2),
            pl.BlockSpec((1, D_MODEL), const2),
            pl.BlockSpec((tm, D_MODEL), lambda i: (i, 0)),
        ],
        out_specs=pl.BlockSpec((tm, D_MODEL), lambda i: (i, 0)),
        out_shape=jax.ShapeDtypeStruct((n_tok, D_MODEL), f32),
        scratch_shapes=[pltpu.VMEM((tm, D_MIX), bf16)],
        compiler_params=pltpu.CompilerParams(
            dimension_semantics=("arbitrary",), vmem_limit_bytes=VMEM_LIMIT),
        name="out_proj",
    )(ya, yb, yc, w, g, x2)


def kernel(x, pre_norm_g, w_in, attn_rpb, sgu_ln_g, sgu_ln_b, sgu_w, sgu_b, conv_w, conv_b,
           conv_ln_g, conv_ln_b, conv_pw_w, conv_pw_b, w_out, post_norm_g):
    bsz, t, d = x.shape
    depth = w_in.shape[0]
    assert d == D_MODEL and t % GRID_W == 0 and t % CHUNK == 0
    n_tok = bsz * t
    x2 = x.reshape(n_tok, d)
    row = lambda a: a.reshape(1, -1).astype(f32)
    for l in range(depth):
        z2 = _in_proj(x2, row(pre_norm_g[l]), w_in[l].astype(bf16))
        z3 = z2.reshape(bsz, t, D_IN)
        y_a = _attention(z3, _attention_bias(attn_rpb[l]))
        b_map = jnp.repeat(sgu_b[l].astype(f32).T, D_B // B_GROUPS, axis=1)
        y_b = _sgu(z2, row(sgu_ln_g[l]), row(sgu_ln_b[l]), sgu_w[l].astype(bf16), b_map)
        y_c = _conformer_conv(z3, conv_w[l].reshape(CONV_W, D_C).astype(f32), row(conv_b[l]),
                              row(conv_ln_g[l]), row(conv_ln_b[l]), conv_pw_w[l].astype(bf16),
                              row(conv_pw_b[l]))
        x2 = _out_proj(y_a.reshape(n_tok, D_A), y_b, y_c.reshape(n_tok, D_C), w_out[l].astype(bf16),
                       row(post_norm_g[l]), x2)
    return x2.reshape(bsz, t, d)
```

```python
import functools
import math

import numpy as np
import jax
import jax.numpy as jnp
from jax import lax
from jax.experimental import pallas as pl
from jax.experimental.pallas import tpu as pltpu

D_MODEL = 2048
GRID_W = 64
WIN_H = 8
WIN_W = 16
HEAD_DIM = 128
D_A = 1024
N_HEADS = 8
CHUNK = 128
D_B = 512
B_GROUPS = 4
D_C = 512
CONV_W = 31
D_MIX = D_A + D_B + D_C
D_IN = 4 * D_A + 3 * D_B + 3 * D_C
EPS = 1e-6
MASKED = -1e30

LANES = 128
SUBLANES = 8
VMEM_LIMIT = 56 * 1024 * 1024

Q_ROWS = 4
K_ROWS = Q_ROWS + WIN_H
PAIR = 2 * GRID_W
assert PAIR == LANES

CONV_PAD = 16

f32 = jnp.float32
bf16 = jnp.bfloat16


def _sigmoid(x):
    return 1.0 / (1.0 + jnp.exp(-x))


def _silu(x):
    return x * _sigmoid(x)


def _gelu(x):
    return 0.5 * x * (1.0 + lax.erf(x * math.sqrt(0.5)))


def _layer_norm(x, g, b):
    mu = jnp.mean(x, axis=-1, keepdims=True)
    xc = x - mu
    var = jnp.mean(xc * xc, axis=-1, keepdims=True)
    return xc * lax.rsqrt(var + EPS) * g + b


def _rms_norm(x, g):
    ms = jnp.mean(x * x, axis=-1, keepdims=True)
    return x * lax.rsqrt(ms + EPS) * g


def _params(*semantics):
    return pltpu.CompilerParams(dimension_semantics=semantics, vmem_limit_bytes=VMEM_LIMIT)


def _pre_norm_kernel(x_ref, g_ref, h_ref):
    h_ref[...] = _rms_norm(x_ref[...], g_ref[...]).astype(h_ref.dtype)


def _pre_norm(x2, g, tm=512):
    n_tok = x2.shape[0]
    return pl.pallas_call(
        _pre_norm_kernel,
        grid=(n_tok // tm,),
        in_specs=[pl.BlockSpec((tm, D_MODEL), lambda i: (i, 0)),
                  pl.BlockSpec((1, D_MODEL), lambda i: (0, 0))],
        out_specs=pl.BlockSpec((tm, D_MODEL), lambda i: (i, 0)),
        out_shape=jax.ShapeDtypeStruct((n_tok, D_MODEL), bf16),
        compiler_params=_params("arbitrary"),
        name="pre_norm",
    )(x2, g)


def _in_proj_kernel(h_ref, w_ref, z_ref, wb_scr):
    @pl.when(pl.program_id(1) == 0)
    def _():
        wb_scr[...] = w_ref[...].astype(bf16)

    z_ref[...] = jnp.dot(h_ref[...], wb_scr[...], preferred_element_type=f32).astype(z_ref.dtype)


def _in_proj(h, w_in, layer, tm=1024, tn=1024):
    n_tok = h.shape[0]
    return pl.pallas_call(
        _in_proj_kernel,
        grid=(D_IN // tn, n_tok // tm),
        in_specs=[
            pl.BlockSpec((tm, D_MODEL), lambda j, i: (i, 0)),
            pl.BlockSpec((None, D_MODEL, tn), lambda j, i: (layer, 0, j)),
        ],
        out_specs=pl.BlockSpec((tm, tn), lambda j, i: (i, j)),
        out_shape=jax.ShapeDtypeStruct((n_tok, D_IN), bf16),
        scratch_shapes=[pltpu.VMEM((D_MODEL, tn), bf16)],
        compiler_params=_params("arbitrary", "arbitrary"),
        name="in_proj",
    )(h, w_in)


def _attn_kernel(q_ref, k_ref, v_ref, g_ref, bias_ref, o_ref, *, rows):
    scale = HEAD_DIM ** -0.5
    nq = Q_ROWS * GRID_W
    nk = K_ROWS * GRID_W
    left_half = lax.broadcasted_iota(jnp.int32, (GRID_W, PAIR), 1) < GRID_W

    for blk in range(rows // Q_ROWS):
        ks = min(max(blk * Q_ROWS - WIN_H // 2, 0), rows - K_ROWS)
        q0, k0 = blk * nq, ks * GRID_W
        q = (q_ref[0, q0:q0 + nq, :].astype(f32) * scale).astype(bf16)
        kw = k_ref[0, k0:k0 + nk, :]
        vw = v_ref[0, k0:k0 + nk, :]
        s = lax.dot_general(q, kw, (((1,), (1,)), ((), ())), preferred_element_type=f32)

        p_rows, inv_l = [], []
        for rq in range(Q_ROWS):
            r = blk * Q_ROWS + rq
            rs = min(max(r - WIN_H // 2, 0), rows - WIN_H)
            pieces = {}
            for ip in range(K_ROWS // 2):
                kr = ks + 2 * ip
                ok0, ok1 = rs <= kr < rs + WIN_H, rs <= kr + 1 < rs + WIN_H
                if not (ok0 or ok1):
                    continue
                dr0 = kr - r + (WIN_H - 1)
                sub = s[rq * GRID_W:(rq + 1) * GRID_W, ip * PAIR:(ip + 1) * PAIR]
                if ok0 and ok1:
                    bias = bias_ref[0, dr0 + 1]
                elif ok0:
                    bias = jnp.where(left_half, bias_ref[0, dr0 + 1], MASKED)
                else:
                    bias = jnp.where(left_half, MASKED, bias_ref[0, dr0 + 1])
                pieces[ip] = sub + bias
            m = jnp.max(functools.reduce(jnp.maximum, pieces.values()), axis=-1, keepdims=True)
            pieces = {ip: jnp.exp(sv - m) for ip, sv in pieces.items()}
            l = jnp.sum(functools.reduce(jnp.add, pieces.values()), axis=-1, keepdims=True)
            inv_l.append(1.0 / l)
            zero = jnp.zeros((GRID_W, PAIR), bf16)
            p_rows.append(jnp.concatenate(
                [pieces[ip].astype(bf16) if ip in pieces else zero for ip in range(K_ROWS // 2)], axis=1))
        p = jnp.concatenate(p_rows, axis=0)
        o = jnp.dot(p, vw, preferred_element_type=f32)
        for rq in range(Q_ROWS):
            rows_q = slice(q0 + rq * GRID_W, q0 + (rq + 1) * GRID_W)
            g = g_ref[0, rows_q, :].astype(f32)
            o_rq = o[rq * GRID_W:(rq + 1) * GRID_W] * inv_l[rq]
            o_ref[0, rows_q, :] = (o_rq * _silu(g)).astype(o_ref.dtype)


def _attention(z3, bias):
    bsz, t, _ = z3.shape
    rows = t // GRID_W
    assert rows % Q_ROWS == 0 and rows >= K_ROWS
    blk = (1, t, HEAD_DIM)
    return pl.pallas_call(
        functools.partial(_attn_kernel, rows=rows),
        grid=(bsz, N_HEADS),
        in_specs=[
            pl.BlockSpec(blk, lambda b, h: (b, 0, h)),
            pl.BlockSpec(blk, lambda b, h: (b, 0, N_HEADS + h)),
            pl.BlockSpec(blk, lambda b, h: (b, 0, 2 * N_HEADS + h)),
            pl.BlockSpec(blk, lambda b, h: (b, 0, 3 * N_HEADS + h)),
            pl.BlockSpec((1, 2 * WIN_H, GRID_W, PAIR), lambda b, h: (h, 0, 0, 0)),
        ],
        out_specs=pl.BlockSpec(blk, lambda b, h: (b, 0, h)),
        out_shape=jax.ShapeDtypeStruct((bsz, t, D_A), bf16),
        compiler_params=_params("arbitrary", "arbitrary"),
        name="nbr_attention",
    )(z3, z3, z3, z3, bias)


def _attention_bias(rpb):
    cols = np.arange(GRID_W)
    col_start = np.clip(cols - WIN_W // 2, 0, GRID_W - WIN_W)
    kc = cols[None, :]
    in_win = (kc >= col_start[:, None]) & (kc < col_start[:, None] + WIN_W)
    dc = np.clip(kc - cols[:, None] + (WIN_W - 1), 0, 2 * WIN_W - 2)
    t = jnp.where(in_win[None, None], rpb.astype(f32)[:, :, dc], MASKED)
    pad = jnp.full((N_HEADS, 1, GRID_W, GRID_W), MASKED, f32)
    t = jnp.concatenate([pad, t, pad], axis=1)
    return jnp.concatenate([t[:, :-1], t[:, 1:]], axis=-1)


def _sgu_kernel(u_ref, v_ref, g_ref, lng_ref, lnb_ref, ws_ref, bs_ref, o_ref, *, tm):
    gd = D_B // B_GROUPS
    v = _layer_norm(_gelu(v_ref[...].astype(f32)), lng_ref[...], lnb_ref[...]).astype(bf16)
    for n in range(tm // CHUNK):
        rows = slice(n * CHUNK, (n + 1) * CHUNK)
        for grp in range(B_GROUPS):
            cols = slice(grp * gd, (grp + 1) * gd)
            s = jnp.dot(ws_ref[grp], v[rows, cols], preferred_element_type=f32) + bs_ref[:, cols]
            u = _gelu(u_ref[rows, cols].astype(f32))
            g = g_ref[rows, cols].astype(f32)
            o_ref[rows, cols] = (u * s * _silu(g)).astype(o_ref.dtype)


def _sgu(z2, ln_g, ln_b, w_s, b_map, tm=512):
    n_tok = z2.shape[0]
    c0 = 4 * D_A // D_B
    blk = (tm, D_B)
    const2 = lambda i: (0, 0)
    return pl.pallas_call(
        functools.partial(_sgu_kernel, tm=tm),
        grid=(n_tok // tm,),
        in_specs=[
            pl.BlockSpec(blk, lambda i: (i, c0)),
            pl.BlockSpec(blk, lambda i: (i, c0 + 1)),
            pl.BlockSpec(blk, lambda i: (i, c0 + 2)),
            pl.BlockSpec((1, D_B), const2),
            pl.BlockSpec((1, D_B), const2),
            pl.BlockSpec((B_GROUPS, CHUNK, CHUNK), lambda i: (0, 0, 0)),
            pl.BlockSpec((CHUNK, D_B), const2),
        ],
        out_specs=pl.BlockSpec(blk, lambda i: (i, 0)),
        out_shape=jax.ShapeDtypeStruct((n_tok, D_B), bf16),
        compiler_params=_params("arbitrary"),
        name="sgu",
    )(z2, z2, z2, ln_g, ln_b, w_s, b_map)


def _conv_kernel(a_ref, b_ref, g_ref, cw_ref, cb_ref, lng_ref, lnb_ref, pw_ref, pb_ref, o_ref,
                 h_scr, c_scr, *, t, tt, tm):
    zeros = jnp.zeros((CONV_PAD, D_C), f32)
    h_scr[pl.ds(0, CONV_PAD), :] = zeros
    h_scr[pl.ds(CONV_PAD + t, CONV_PAD), :] = zeros
    h_scr[pl.ds(CONV_PAD, t), :] = a_ref[0].astype(f32) * _sigmoid(b_ref[0].astype(f32))

    shift = CONV_PAD - CONV_W // 2
    n_al = (CONV_W + shift + SUBLANES - 1) // SUBLANES
    win_rows = tt + n_al * SUBLANES

    def conv_tile(ti, carry):
        t0 = pl.multiple_of(ti * tt, tt)
        for cb in range(D_C // LANES):
            cols = slice(cb * LANES, (cb + 1) * LANES)
            win = h_scr[pl.ds(t0, win_rows), cols]
            acc = None
            for b in range(SUBLANES):
                part = None
                for a in range(n_al):
                    k = SUBLANES * a + b - shift
                    if 0 <= k < CONV_W:
                        term = cw_ref[k:k + 1, cols] * win[SUBLANES * a:SUBLANES * a + tt + SUBLANES]
                        part = term if part is None else part + term
                part = part[b:b + tt]
                acc = part if acc is None else acc + part
            c_scr[pl.ds(t0, tt), cols] = acc
        return carry

    lax.fori_loop(0, t // tt, conv_tile, 0)

    def post_tile(ti, carry):
        t0 = pl.multiple_of(ti * tm, tm)
        c = c_scr[pl.ds(t0, tm), :] + cb_ref[...]
        c = _silu(_layer_norm(c, lng_ref[...], lnb_ref[...])).astype(bf16)
        y = jnp.dot(c, pw_ref[...], preferred_element_type=f32) + pb_ref[...]
        g = g_ref[0, pl.ds(t0, tm), :].astype(f32)
        o_ref[0, pl.ds(t0, tm), :] = (y * _silu(g)).astype(o_ref.dtype)
        return carry

    lax.fori_loop(0, t // tm, post_tile, 0)


def _conformer_conv(z3, conv_w, conv_b, ln_g, ln_b, pw_w, pw_b, tt=64, tm=512):
    bsz, t, _ = z3.shape
    assert t % tt == 0 and t % tm == 0
    c0 = (4 * D_A + 3 * D_B) // D_C
    blk = (1, t, D_C)
    const2 = lambda b: (0, 0)
    return pl.pallas_call(
        functools.partial(_conv_kernel, t=t, tt=tt, tm=tm),
        grid=(bsz,),
        in_specs=[
            pl.BlockSpec(blk, lambda b: (b, 0, c0)),
            pl.BlockSpec(blk, lambda b: (b, 0, c0 + 1)),
            pl.BlockSpec(blk, lambda b: (b, 0, c0 + 2)),
            pl.BlockSpec((CONV_W, D_C), const2),
            pl.BlockSpec((1, D_C), const2),
            pl.BlockSpec((1, D_C), const2),
            pl.BlockSpec((1, D_C), const2),
            pl.BlockSpec((D_C, D_C), const2),
            pl.BlockSpec((1, D_C), const2),
        ],
        out_specs=pl.BlockSpec(blk, lambda b: (b, 0, 0)),
        out_shape=jax.ShapeDtypeStruct((bsz, t, D_C), bf16),
        scratch_shapes=[pltpu.VMEM((t + 2 * CONV_PAD, D_C), f32), pltpu.VMEM((t, D_C), f32)],
        compiler_params=_params("arbitrary"),
        name="conformer_conv",
    )(z3, z3, z3, conv_w, conv_b, ln_g, ln_b, pw_w, pw_b)


def _out_proj_kernel(ya_ref, yb_ref, yc_ref, w_ref, g_ref, x_ref, gn_ref, o_ref, *rest, emit_h):
    cat_scr = rest[-1]
    cat_scr[:, 0:D_A] = ya_ref[...]
    cat_scr[:, D_A:D_A + D_B] = yb_ref[...]
    cat_scr[:, D_A + D_B:D_MIX] = yc_ref[...]
    y = jnp.dot(cat_scr[...], w_ref[...], preferred_element_type=f32)
    x_new = x_ref[...] + _rms_norm(y, g_ref[...])
    o_ref[...] = x_new
    if emit_h:
        h_ref = rest[0]
        h_ref[...] = _rms_norm(x_new, gn_ref[...]).astype(h_ref.dtype)


def _out_proj(ya, yb, yc, w, g, x2, g_next, emit_h, tm=512):
    n_tok = x2.shape[0]
    const2 = lambda i: (0, 0)
    tile = lambda width: pl.BlockSpec((tm, width), lambda i: (i, 0))
    out_shape = [jax.ShapeDtypeStruct((n_tok, D_MODEL), f32)]
    out_specs = [tile(D_MODEL)]
    if emit_h:
        out_shape.append(jax.ShapeDtypeStruct((n_tok, D_MODEL), bf16))
        out_specs.append(tile(D_MODEL))
    return pl.pallas_call(
        functools.partial(_out_proj_kernel, emit_h=emit_h),
        grid=(n_tok // tm,),
        in_specs=[
            tile(D_A), tile(D_B), tile(D_C),
            pl.BlockSpec((D_MIX, D_MODEL), const2),
            pl.BlockSpec((1, D_MODEL), const2),
            tile(D_MODEL),
            pl.BlockSpec((1, D_MODEL), const2),
        ],
        out_specs=out_specs,
        out_shape=out_shape,
        scratch_shapes=[pltpu.VMEM((tm, D_MIX), bf16)],
        compiler_params=_params("arbitrary"),
        name="out_proj",
    )(ya, yb, yc, w, g, x2, g_next)


def kernel(x, pre_norm_g, w_in, attn_rpb, sgu_ln_g, sgu_ln_b, sgu_w, sgu_b, conv_w, conv_b,
           conv_ln_g, conv_ln_b, conv_pw_w, conv_pw_b, w_out, post_norm_g):
    bsz, t, d = x.shape
    depth = w_in.shape[0]
    assert d == D_MODEL and t % GRID_W == 0 and t % CHUNK == 0
    n_tok = bsz * t
    x2 = x.reshape(n_tok, d)
    row = lambda a: a.reshape(1, -1).astype(f32)
    h = _pre_norm(x2, row(pre_norm_g[0]))
    for l in range(depth):
        z2 = _in_proj(h, w_in, l)
        z3 = z2.reshape(bsz, t, D_IN)
        y_a = _attention(z3, _attention_bias(attn_rpb[l]))
        b_map = jnp.repeat(sgu_b[l].astype(f32).T, D_B // B_GROUPS, axis=1)
        y_b = _sgu(z2, row(sgu_ln_g[l]), row(sgu_ln_b[l]), sgu_w[l].astype(bf16), b_map)
        y_c = _conformer_conv(z3, conv_w[l].reshape(CONV_W, D_C).astype(f32), row(conv_b[l]),
                              row(conv_ln_g[l]), row(conv_ln_b[l]), conv_pw_w[l].astype(bf16),
                              row(conv_pw_b[l]))
        last = l == depth - 1
        outs = _out_proj(y_a.reshape(n_tok, D_A), y_b, y_c.reshape(n_tok, D_C), w_out[l].astype(bf16),
                         row(post_norm_g[l]), x2, row(pre_norm_g[0 if last else l + 1]), not last)
        x2 = outs[0]
        if not last:
            h = outs[1]
    return x2.reshape(bsz, t, d)
```

```python
import functools
import math

import numpy as np
import jax
import jax.numpy as jnp
from jax import lax
from jax.experimental import pallas as pl
from jax.experimental.pallas import tpu as pltpu

D_MODEL = 2048
GRID_W = 64
WIN_H = 8
WIN_W = 16
HEAD_DIM = 128
D_A = 1024
N_HEADS = 8
CHUNK = 128
D_B = 512
B_GROUPS = 4
D_C = 512
CONV_W = 31
D_MIX = D_A + D_B + D_C
D_IN = 4 * D_A + 3 * D_B + 3 * D_C
EPS = 1e-6
MASKED = -1e30

LANES = 128
SUBLANES = 8
VMEM_LIMIT = 56 * 1024 * 1024

Q_ROWS = 4
K_ROWS = Q_ROWS + WIN_H
PAIR = 2 * GRID_W
assert PAIR == LANES

CONV_PAD = 16

f32 = jnp.float32
bf16 = jnp.bfloat16


def _sigmoid(x):
    return 1.0 / (1.0 + jnp.exp(-x))


def _silu(x):
    return x * _sigmoid(x)


def _gelu(x):
    return 0.5 * x * (1.0 + lax.erf(x * math.sqrt(0.5)))


def _layer_norm(x, g, b):
    mu = jnp.mean(x, axis=-1, keepdims=True)
    xc = x - mu
    var = jnp.mean(xc * xc, axis=-1, keepdims=True)
    return xc * lax.rsqrt(var + EPS) * g + b


def _rms_norm(x, g):
    ms = jnp.mean(x * x, axis=-1, keepdims=True)
    return x * lax.rsqrt(ms + EPS) * g


def _params(*semantics):
    return pltpu.CompilerParams(dimension_semantics=semantics, vmem_limit_bytes=VMEM_LIMIT)


def _pre_norm_kernel(x_ref, g_ref, h_ref):
    h_ref[...] = _rms_norm(x_ref[...], g_ref[...]).astype(h_ref.dtype)


def _pre_norm(x2, g, tm=512):
    n_tok = x2.shape[0]
    return pl.pallas_call(
        _pre_norm_kernel,
        grid=(n_tok // tm,),
        in_specs=[pl.BlockSpec((tm, D_MODEL), lambda i: (i, 0)),
                  pl.BlockSpec((1, D_MODEL), lambda i: (0, 0))],
        out_specs=pl.BlockSpec((tm, D_MODEL), lambda i: (i, 0)),
        out_shape=jax.ShapeDtypeStruct((n_tok, D_MODEL), bf16),
        compiler_params=_params("arbitrary"),
        name="pre_norm",
    )(x2, g)


def _in_proj_kernel(h_ref, w_ref, z_ref, wb_scr):
    @pl.when(pl.program_id(1) == 0)
    def _():
        wb_scr[...] = w_ref[...].astype(bf16)

    z_ref[...] = jnp.dot(h_ref[...], wb_scr[...], preferred_element_type=f32).astype(z_ref.dtype)


def _in_proj(h, w_in, layer, tm=1024, tn=1024):
    n_tok = h.shape[0]
    return pl.pallas_call(
        _in_proj_kernel,
        grid=(D_IN // tn, n_tok // tm),
        in_specs=[
            pl.BlockSpec((tm, D_MODEL), lambda j, i: (i, 0)),
            pl.BlockSpec((None, D_MODEL, tn), lambda j, i: (layer, 0, j)),
        ],
        out_specs=pl.BlockSpec((tm, tn), lambda j, i: (i, j)),
        out_shape=jax.ShapeDtypeStruct((n_tok, D_IN), bf16),
        scratch_shapes=[pltpu.VMEM((D_MODEL, tn), bf16)],
        compiler_params=_params("arbitrary", "arbitrary"),
        name="in_proj",
    )(h, w_in)


def _attn_kernel(q_ref, k_ref, v_ref, g_ref, vec_ref, o_ref, bias_ref, *, rows):
    scale = HEAD_DIM ** -0.5
    nq = Q_ROWS * GRID_W
    nk = K_ROWS * GRID_W
    w_idx = lax.broadcasted_iota(jnp.int32, (GRID_W, PAIR), 0)
    lane = lax.broadcasted_iota(jnp.int32, (GRID_W, PAIR), 1)
    left_half = lane < GRID_W
    kc = jnp.where(left_half, lane, lane - GRID_W)
    col_start = jnp.clip(w_idx - WIN_W // 2, 0, GRID_W - WIN_W)
    in_win = (kc >= col_start) & (kc < col_start + WIN_W)
    for e in range(2 * WIN_H):
        vec = jnp.broadcast_to(vec_ref[0, e:e + 1, :], (GRID_W, PAIR))
        bias_ref[e] = jnp.where(in_win, pltpu.roll(vec, 0, 1, stride=1, stride_axis=0), MASKED)

    for blk in range(rows // Q_ROWS):
        ks = min(max(blk * Q_ROWS - WIN_H // 2, 0), rows - K_ROWS)
        q0, k0 = blk * nq, ks * GRID_W
        q = (q_ref[0, q0:q0 + nq, :].astype(f32) * scale).astype(bf16)
        kw = k_ref[0, k0:k0 + nk, :]
        vw = v_ref[0, k0:k0 + nk, :]
        s = lax.dot_general(q, kw, (((1,), (1,)), ((), ())), preferred_element_type=f32)

        p_rows, inv_l = [], []
        for rq in range(Q_ROWS):
            r = blk * Q_ROWS + rq
            rs = min(max(r - WIN_H // 2, 0), rows - WIN_H)
            pieces = {}
            for ip in range(K_ROWS // 2):
                kr = ks + 2 * ip
                ok0, ok1 = rs <= kr < rs + WIN_H, rs <= kr + 1 < rs + WIN_H
                if not (ok0 or ok1):
                    continue
                dr0 = kr - r + (WIN_H - 1)
                sub = s[rq * GRID_W:(rq + 1) * GRID_W, ip * PAIR:(ip + 1) * PAIR]
                if ok0 and ok1:
                    bias = bias_ref[dr0 + 1]
                elif ok0:
                    bias = jnp.where(left_half, bias_ref[dr0 + 1], MASKED)
                else:
                    bias = jnp.where(left_half, MASKED, bias_ref[dr0 + 1])
                pieces[ip] = sub + bias
            m = jnp.max(functools.reduce(jnp.maximum, pieces.values()), axis=-1, keepdims=True)
            pieces = {ip: jnp.exp(sv - m) for ip, sv in pieces.items()}
            l = jnp.sum(functools.reduce(jnp.add, pieces.values()), axis=-1, keepdims=True)
            inv_l.append(1.0 / l)
            zero = jnp.zeros((GRID_W, PAIR), bf16)
            p_rows.append(jnp.concatenate(
                [pieces[ip].astype(bf16) if ip in pieces else zero for ip in range(K_ROWS // 2)], axis=1))
        p = jnp.concatenate(p_rows, axis=0)
        o = jnp.dot(p, vw, preferred_element_type=f32)
        for rq in range(Q_ROWS):
            rows_q = slice(q0 + rq * GRID_W, q0 + (rq + 1) * GRID_W)
            g = g_ref[0, rows_q, :].astype(f32)
            o_rq = o[rq * GRID_W:(rq + 1) * GRID_W] * inv_l[rq]
            o_ref[0, rows_q, :] = (o_rq * _silu(g)).astype(o_ref.dtype)


def _attention(z3, vecs):
    bsz, t, _ = z3.shape
    rows = t // GRID_W
    assert rows % Q_ROWS == 0 and rows >= K_ROWS
    blk = (1, t, HEAD_DIM)
    return pl.pallas_call(
        functools.partial(_attn_kernel, rows=rows),
        grid=(bsz, N_HEADS),
        in_specs=[
            pl.BlockSpec(blk, lambda b, h: (b, 0, h)),
            pl.BlockSpec(blk, lambda b, h: (b, 0, N_HEADS + h)),
            pl.BlockSpec(blk, lambda b, h: (b, 0, 2 * N_HEADS + h)),
            pl.BlockSpec(blk, lambda b, h: (b, 0, 3 * N_HEADS + h)),
            pl.BlockSpec((1, 2 * WIN_H, PAIR), lambda b, h: (h, 0, 0)),
        ],
        out_specs=pl.BlockSpec(blk, lambda b, h: (b, 0, h)),
        out_shape=jax.ShapeDtypeStruct((bsz, t, D_A), bf16),
        scratch_shapes=[pltpu.VMEM((2 * WIN_H, GRID_W, PAIR), f32)],
        compiler_params=_params("arbitrary", "arbitrary"),
        name="nbr_attention",
    )(z3, z3, z3, z3, vecs)


def _attention_bias_vectors(rpb):
    half = WIN_W - 1
    pad = jnp.full((N_HEADS, 1, 2 * WIN_W - 1), MASKED, f32)
    ext = jnp.concatenate([pad, rpb.astype(f32), pad], axis=1)
    lower, upper = ext[:, :-1], ext[:, 1:]
    gap = jnp.full((N_HEADS, 2 * WIN_H, GRID_W - 2 * half - 1), MASKED, f32)
    return jnp.concatenate([lower[..., half:], gap, upper, gap, lower[..., :half]], axis=-1)


def _sgu_kernel(u_ref, v_ref, g_ref, lng_ref, lnb_ref, ws_ref, bs_ref, o_ref, *, tm):
    gd = D_B // B_GROUPS
    v = _layer_norm(_gelu(v_ref[...].astype(f32)), lng_ref[...], lnb_ref[...]).astype(bf16)
    for n in range(tm // CHUNK):
        rows = slice(n * CHUNK, (n + 1) * CHUNK)
        for grp in range(B_GROUPS):
            cols = slice(grp * gd, (grp + 1) * gd)
            s = jnp.dot(ws_ref[grp], v[rows, cols], preferred_element_type=f32) + bs_ref[:, cols]
            u = _gelu(u_ref[rows, cols].astype(f32))
            g = g_ref[rows, cols].astype(f32)
            o_ref[rows, cols] = (u * s * _silu(g)).astype(o_ref.dtype)


def _sgu(z2, ln_g, ln_b, w_s, b_map, tm=512):
    n_tok = z2.shape[0]
    c0 = 4 * D_A // D_B
    blk = (tm, D_B)
    const2 = lambda i: (0, 0)
    return pl.pallas_call(
        functools.partial(_sgu_kernel, tm=tm),
        grid=(n_tok // tm,),
        in_specs=[
            pl.BlockSpec(blk, lambda i: (i, c0)),
            pl.BlockSpec(blk, lambda i: (i, c0 + 1)),
            pl.BlockSpec(blk, lambda i: (i, c0 + 2)),
            pl.BlockSpec((1, D_B), const2),
            pl.BlockSpec((1, D_B), const2),
            pl.BlockSpec((B_GROUPS, CHUNK, CHUNK), lambda i: (0, 0, 0)),
            pl.BlockSpec((CHUNK, D_B), const2),
        ],
        out_specs=pl.BlockSpec(blk, lambda i: (i, 0)),
        out_shape=jax.ShapeDtypeStruct((n_tok, D_B), bf16),
        compiler_params=_params("arbitrary"),
        name="sgu",
    )(z2, z2, z2, ln_g, ln_b, w_s, b_map)


def _conv_rows(hs_scr, cw_ref, c_scr, t0, tt):
    shift = CONV_PAD - CONV_W // 2
    n_al = (CONV_W + shift + SUBLANES - 1) // SUBLANES
    assert n_al * SUBLANES <= 2 * CONV_PAD
    for cb in range(D_C // LANES):
        cols = slice(cb * LANES, (cb + 1) * LANES)
        win = hs_scr[pl.ds(t0, tt + n_al * SUBLANES), cols]
        acc = None
        for b in range(SUBLANES):
            part = None
            for a in range(n_al):
                k = SUBLANES * a + b - shift
                if 0 <= k < CONV_W:
                    term = cw_ref[k:k + 1, cols] * win[SUBLANES * a:SUBLANES * a + tt + SUBLANES]
                    part = term if part is None else part + term
            part = part[b:b + tt]
            acc = part if acc is None else acc + part
        c_scr[pl.ds(t0, tt), cols] = acc


def _out_conv_kernel(ya_ref, yb_ref, w_ref, pg_ref, x_ref, gn_ref,
                     a_ref, b_ref, g_ref, ap_ref, bp_ref, an_ref, bn_ref,
                     cw_ref, cb_ref, lng_ref, lnb_ref, pw_ref, pb_ref,
                     *rest, tm, tt, tn, n_loop, cols_per_trip, tiles_per_seq, emit_h):
    o_ref = rest[0]
    hs_scr, c_scr, yc_scr, cat_scr, y_scr = rest[-5:]
    s = pl.program_id(0)
    n_tiles = pl.num_programs(0) - 1

    @pl.when(s == 0)
    def _():
        yc_scr[1] = jnp.zeros((tm, D_C), bf16)

    in_seq = jnp.minimum(s, n_tiles - 1) % tiles_per_seq
    glu = lambda a, b: a.astype(f32) * _sigmoid(b.astype(f32))
    hs_scr[0:CONV_PAD, :] = jnp.where(in_seq == 0, 0.0, glu(ap_ref[...], bp_ref[...]))
    hs_scr[CONV_PAD:CONV_PAD + tm, :] = glu(a_ref[...], b_ref[...])
    hs_scr[CONV_PAD + tm:, :] = jnp.where(in_seq == tiles_per_seq - 1, 0.0, glu(an_ref[...], bn_ref[...]))

    cat_scr[:, 0:D_A] = ya_ref[...]
    cat_scr[:, D_A:D_A + D_B] = yb_ref[...]
    cat_scr[:, D_A + D_B:D_MIX] = yc_scr[(s + 1) % 2]

    n_col = D_MODEL // tn
    rows_per_trip = tm // n_loop
    assert rows_per_trip % tt == 0 and n_loop * cols_per_trip <= n_col

    def trip(c, carry):
        for u in range(rows_per_trip // tt):
            _conv_rows(hs_scr, cw_ref, c_scr, pl.multiple_of(c * rows_per_trip + u * tt, tt), tt)
        for u in range(cols_per_trip):
            j = c * cols_per_trip + u
            y_scr[j] = jnp.dot(cat_scr[...], w_ref[j], preferred_element_type=f32)
        return carry

    lax.fori_loop(0, n_loop, trip, 0)
    for j in range(n_loop * cols_per_trip, n_col):
        y_scr[j] = jnp.dot(cat_scr[...], w_ref[j], preferred_element_type=f32)

    c = _silu(_layer_norm(c_scr[...] + cb_ref[...], lng_ref[...], lnb_ref[...])).astype(bf16)
    y = jnp.dot(c, pw_ref[...], preferred_element_type=f32) + pb_ref[...]
    yc_scr[s % 2] = (y * _silu(g_ref[...].astype(f32))).astype(bf16)

    ms = sum(jnp.sum(y_scr[j] * y_scr[j], axis=-1, keepdims=True) for j in range(n_col)) * (1.0 / D_MODEL)
    inv = lax.rsqrt(ms + EPS)
    x_new = [x_ref[:, j * tn:(j + 1) * tn] + y_scr[j] * inv * pg_ref[:, j * tn:(j + 1) * tn]
             for j in range(n_col)]
    for j in range(n_col):
        o_ref[:, j * tn:(j + 1) * tn] = x_new[j]
    if emit_h:
        h_ref = rest[1]
        ms = sum(jnp.sum(xj * xj, axis=-1, keepdims=True) for xj in x_new) * (1.0 / D_MODEL)
        inv = lax.rsqrt(ms + EPS)
        for j in range(n_col):
            cols = slice(j * tn, (j + 1) * tn)
            h_ref[:, cols] = (x_new[j] * inv * gn_ref[:, cols]).astype(h_ref.dtype)


def _out_proj_conv(ya, yb, z2, w, post_g, x2, next_g, conv_w, conv_b, ln_g, ln_b, pw_w, pw_b, *,
                   seq_len, emit_h, tm=512, tt=64, tn=256, n_loop=4, cols_per_trip=1):
    n_tok = x2.shape[0]
    assert seq_len % tm == 0 and tm % tt == 0 and tm % CONV_PAD == 0
    n_tiles = n_tok // tm
    w = w.astype(bf16).reshape(D_MIX, D_MODEL // tn, tn).transpose(1, 0, 2)
    hb = tm // CONV_PAD
    c0 = (4 * D_A + 3 * D_B) // D_C
    const2 = lambda s: (0, 0)
    mm = lambda s: jnp.maximum(s - 1, 0)
    cv = lambda s: jnp.minimum(s, n_tiles - 1)
    mm_tile = lambda width: pl.BlockSpec((tm, width), lambda s: (mm(s), 0))
    cv_tile = lambda col: pl.BlockSpec((tm, D_C), lambda s: (cv(s), col))
    halo_prev = lambda col: pl.BlockSpec(
        (CONV_PAD, D_C), lambda s: (jnp.maximum(cv(s) * hb - 1, 0), col))
    halo_next = lambda col: pl.BlockSpec(
        (CONV_PAD, D_C), lambda s: (jnp.minimum((cv(s) + 1) * hb, n_tok // CONV_PAD - 1), col))
    out_shape = [jax.ShapeDtypeStruct((n_tok, D_MODEL), f32)]
    out_specs = [mm_tile(D_MODEL)]
    if emit_h:
        out_shape.append(jax.ShapeDtypeStruct((n_tok, D_MODEL), bf16))
        out_specs.append(mm_tile(D_MODEL))
    return pl.pallas_call(
        functools.partial(_out_conv_kernel, tm=tm, tt=tt, tn=tn, n_loop=n_loop, cols_per_trip=cols_per_trip,
                          tiles_per_seq=seq_len // tm, emit_h=emit_h),
        grid=(n_tiles + 1,),
        in_specs=[
            mm_tile(D_A), mm_tile(D_B),
            pl.BlockSpec((D_MODEL // tn, D_MIX, tn), lambda s: (0, 0, 0), pipeline_mode=pl.Buffered(1)),
            pl.BlockSpec((1, D_MODEL), const2),
            mm_tile(D_MODEL),
            pl.BlockSpec((1, D_MODEL), const2),
            cv_tile(c0), cv_tile(c0 + 1), cv_tile(c0 + 2),
            halo_prev(c0), halo_prev(c0 + 1), halo_next(c0), halo_next(c0 + 1),
            pl.BlockSpec((CONV_W, D_C), const2),
            pl.BlockSpec((1, D_C), const2),
            pl.BlockSpec((1, D_C), const2),
            pl.BlockSpec((1, D_C), const2),
            pl.BlockSpec((D_C, D_C), const2),
            pl.BlockSpec((1, D_C), const2),
        ],
        out_specs=out_specs,
        out_shape=out_shape,
        scratch_shapes=[
            pltpu.VMEM((tm + 2 * CONV_PAD, D_C), f32),
            pltpu.VMEM((tm, D_C), f32),
            pltpu.VMEM((2, tm, D_C), bf16),
            pltpu.VMEM((tm, D_MIX), bf16),
            pltpu.VMEM((D_MODEL // tn, tm, tn), f32),
        ],
        compiler_params=_params("arbitrary"),
        name="out_proj_conv",
    )(ya, yb, w, post_g, x2, next_g, z2, z2, z2, z2, z2, z2, z2, conv_w, conv_b, ln_g, ln_b, pw_w, pw_b)


def kernel(x, pre_norm_g, w_in, attn_rpb, sgu_ln_g, sgu_ln_b, sgu_w, sgu_b, conv_w, conv_b,
           conv_ln_g, conv_ln_b, conv_pw_w, conv_pw_b, w_out, post_norm_g):
    bsz, t, d = x.shape
    depth = w_in.shape[0]
    assert d == D_MODEL and t % GRID_W == 0 and t % CHUNK == 0
    n_tok = bsz * t
    x2 = x.reshape(n_tok, d)
    row = lambda a: a.reshape(1, -1).astype(f32)
    h = _pre_norm(x2, row(pre_norm_g[0]))
    for l in range(depth):
        z2 = _in_proj(h, w_in, l)
        z3 = z2.reshape(bsz, t, D_IN)
        y_a = _attention(z3, _attention_bias_vectors(attn_rpb[l]))
        b_map = jnp.repeat(sgu_b[l].astype(f32).T, D_B // B_GROUPS, axis=1)
        y_b = _sgu(z2, row(sgu_ln_g[l]), row(sgu_ln_b[l]), sgu_w[l].astype(bf16), b_map)
        last = l == depth - 1
        next_g = post_norm_g[l] if last else pre_norm_g[l + 1]
        outs = _out_proj_conv(
            y_a.reshape(n_tok, D_A), y_b, z2, w_out[l], row(post_norm_g[l]), x2, row(next_g),
            conv_w[l].reshape(CONV_W, D_C).astype(f32), row(conv_b[l]), row(conv_ln_g[l]),
            row(conv_ln_b[l]), conv_pw_w[l].astype(bf16), row(conv_pw_b[l]),
            seq_len=t, emit_h=not last)
        x2 = outs[0]
        if not last:
            h = outs[1]
    return x2.reshape(bsz, t, d)
```

```python
import functools
import math

import numpy as np
import jax
import jax.numpy as jnp
from jax import lax
from jax.experimental import pallas as pl
from jax.experimental.pallas import tpu as pltpu

D_MODEL = 2048
GRID_W = 64
WIN_H = 8
WIN_W = 16
HEAD_DIM = 128
D_A = 1024
N_HEADS = 8
CHUNK = 128
D_B = 512
B_GROUPS = 4
D_C = 512
CONV_W = 31
D_MIX = D_A + D_B + D_C
D_IN = 4 * D_A + 3 * D_B + 3 * D_C
EPS = 1e-6
MASKED = -1e30

LANES = 128
SUBLANES = 8
VMEM_LIMIT = 56 * 1024 * 1024

Q_ROWS = 4
K_ROWS = Q_ROWS + WIN_H
PAIR = 2 * GRID_W
assert PAIR == LANES

CONV_PAD = 16

f32 = jnp.float32
bf16 = jnp.bfloat16


def _sigmoid(x):
    return 1.0 / (1.0 + jnp.exp(-x))


def _silu(x):
    return x * _sigmoid(x)


def _gelu(x):
    return 0.5 * x * (1.0 + lax.erf(x * math.sqrt(0.5)))


def _layer_norm(x, g, b):
    mu = jnp.mean(x, axis=-1, keepdims=True)
    xc = x - mu
    var = jnp.mean(xc * xc, axis=-1, keepdims=True)
    return xc * lax.rsqrt(var + EPS) * g + b


def _rms_norm(x, g):
    ms = jnp.mean(x * x, axis=-1, keepdims=True)
    return x * lax.rsqrt(ms + EPS) * g


def _params(*semantics):
    return pltpu.CompilerParams(dimension_semantics=semantics, vmem_limit_bytes=VMEM_LIMIT)


def _pre_norm_kernel(x_ref, g_ref, h_ref):
    h_ref[...] = _rms_norm(x_ref[...], g_ref[...]).astype(h_ref.dtype)


def _pre_norm(x2, g, tm=512):
    n_tok = x2.shape[0]
    return pl.pallas_call(
        _pre_norm_kernel,
        grid=(n_tok // tm,),
        in_specs=[pl.BlockSpec((tm, D_MODEL), lambda i: (i, 0)),
                  pl.BlockSpec((1, D_MODEL), lambda i: (0, 0))],
        out_specs=pl.BlockSpec((tm, D_MODEL), lambda i: (i, 0)),
        out_shape=jax.ShapeDtypeStruct((n_tok, D_MODEL), bf16),
        compiler_params=_params("arbitrary"),
        name="pre_norm",
    )(x2, g)


def _in_proj_kernel(h_ref, w_ref, z_ref, wb_scr):
    @pl.when(pl.program_id(1) == 0)
    def _():
        wb_scr[...] = w_ref[...].astype(bf16)

    z_ref[...] = jnp.dot(h_ref[...], wb_scr[...], preferred_element_type=f32).astype(z_ref.dtype)


def _in_proj(h, w_in, layer, tm=1024, tn=1024):
    n_tok = h.shape[0]
    return pl.pallas_call(
        _in_proj_kernel,
        grid=(D_IN // tn, n_tok // tm),
        in_specs=[
            pl.BlockSpec((tm, D_MODEL), lambda j, i: (i, 0)),
            pl.BlockSpec((None, D_MODEL, tn), lambda j, i: (layer, 0, j)),
        ],
        out_specs=pl.BlockSpec((tm, tn), lambda j, i: (i, j)),
        out_shape=jax.ShapeDtypeStruct((n_tok, D_IN), bf16),
        scratch_shapes=[pltpu.VMEM((D_MODEL, tn), bf16)],
        compiler_params=_params("arbitrary", "arbitrary"),
        name="in_proj",
    )(h, w_in)


def _attn_kernel(q_ref, k_ref, v_ref, g_ref, vec_ref, o_ref, bias_ref, *, rows):
    scale = HEAD_DIM ** -0.5
    nq = Q_ROWS * GRID_W
    nk = K_ROWS * GRID_W
    w_idx = lax.broadcasted_iota(jnp.int32, (GRID_W, PAIR), 0)
    lane = lax.broadcasted_iota(jnp.int32, (GRID_W, PAIR), 1)
    left_half = lane < GRID_W
    kc = jnp.where(left_half, lane, lane - GRID_W)
    col_start = jnp.clip(w_idx - WIN_W // 2, 0, GRID_W - WIN_W)
    in_win = (kc >= col_start) & (kc < col_start + WIN_W)
    for e in range(2 * WIN_H):
        vec = jnp.broadcast_to(vec_ref[0, e:e + 1, :], (GRID_W, PAIR))
        bias_ref[e] = jnp.where(in_win, pltpu.roll(vec, 0, 1, stride=1, stride_axis=0), MASKED)

    n_blk = rows // Q_ROWS
    first_key_row = lambda blk: min(max(blk * Q_ROWS - WIN_H // 2, 0), rows - K_ROWS)

    def scores(blk):
        q0, k0 = blk * nq, first_key_row(blk) * GRID_W
        q = (q_ref[0, q0:q0 + nq, :].astype(f32) * scale).astype(bf16)
        kw = k_ref[0, k0:k0 + nk, :]
        return lax.dot_general(q, kw, (((1,), (1,)), ((), ())), preferred_element_type=f32)

    s_next = scores(0)
    for blk in range(n_blk):
        ks = first_key_row(blk)
        q0, k0 = blk * nq, ks * GRID_W
        vw = v_ref[0, k0:k0 + nk, :]
        s = s_next
        if blk + 1 < n_blk:
            s_next = scores(blk + 1)

        p_rows, inv_l = [], []
        for rq in range(Q_ROWS):
            r = blk * Q_ROWS + rq
            rs = min(max(r - WIN_H // 2, 0), rows - WIN_H)
            pieces = {}
            for ip in range(K_ROWS // 2):
                kr = ks + 2 * ip
                ok0, ok1 = rs <= kr < rs + WIN_H, rs <= kr + 1 < rs + WIN_H
                if not (ok0 or ok1):
                    continue
                dr0 = kr - r + (WIN_H - 1)
                sub = s[rq * GRID_W:(rq + 1) * GRID_W, ip * PAIR:(ip + 1) * PAIR]
                if ok0 and ok1:
                    bias = bias_ref[dr0 + 1]
                elif ok0:
                    bias = jnp.where(left_half, bias_ref[dr0 + 1], MASKED)
                else:
                    bias = jnp.where(left_half, MASKED, bias_ref[dr0 + 1])
                pieces[ip] = sub + bias
            m = jnp.max(functools.reduce(jnp.maximum, pieces.values()), axis=-1, keepdims=True)
            pieces = {ip: jnp.exp(sv - m) for ip, sv in pieces.items()}
            l = jnp.sum(functools.reduce(jnp.add, pieces.values()), axis=-1, keepdims=True)
            inv_l.append(1.0 / l)
            zero = jnp.zeros((GRID_W, PAIR), bf16)
            p_rows.append(jnp.concatenate(
                [pieces[ip].astype(bf16) if ip in pieces else zero for ip in range(K_ROWS // 2)], axis=1))
        p = jnp.concatenate(p_rows, axis=0)
        o = jnp.dot(p, vw, preferred_element_type=f32)
        for rq in range(Q_ROWS):
            rows_q = slice(q0 + rq * GRID_W, q0 + (rq + 1) * GRID_W)
            g = g_ref[0, rows_q, :].astype(f32)
            o_rq = o[rq * GRID_W:(rq + 1) * GRID_W] * inv_l[rq]
            o_ref[0, rows_q, :] = (o_rq * _silu(g)).astype(o_ref.dtype)


def _attention(z3, vecs):
    bsz, t, _ = z3.shape
    rows = t // GRID_W
    assert rows % Q_ROWS == 0 and rows >= K_ROWS
    blk = (1, t, HEAD_DIM)
    return pl.pallas_call(
        functools.partial(_attn_kernel, rows=rows),
        grid=(bsz, N_HEADS),
        in_specs=[
            pl.BlockSpec(blk, lambda b, h: (b, 0, h)),
            pl.BlockSpec(blk, lambda b, h: (b, 0, N_HEADS + h)),
            pl.BlockSpec(blk, lambda b, h: (b, 0, 2 * N_HEADS + h)),
            pl.BlockSpec(blk, lambda b, h: (b, 0, 3 * N_HEADS + h)),
            pl.BlockSpec((1, 2 * WIN_H, PAIR), lambda b, h: (h, 0, 0)),
        ],
        out_specs=pl.BlockSpec(blk, lambda b, h: (b, 0, h)),
        out_shape=jax.ShapeDtypeStruct((bsz, t, D_A), bf16),
        scratch_shapes=[pltpu.VMEM((2 * WIN_H, GRID_W, PAIR), f32)],
        compiler_params=_params("arbitrary", "arbitrary"),
        name="nbr_attention",
    )(z3, z3, z3, z3, vecs)


def _attention_bias_vectors(rpb):
    half = WIN_W - 1
    pad = jnp.full((N_HEADS, 1, 2 * WIN_W - 1), MASKED, f32)
    ext = jnp.concatenate([pad, rpb.astype(f32), pad], axis=1)
    lower, upper = ext[:, :-1], ext[:, 1:]
    gap = jnp.full((N_HEADS, 2 * WIN_H, GRID_W - 2 * half - 1), MASKED, f32)
    return jnp.concatenate([lower[..., half:], gap, upper, gap, lower[..., :half]], axis=-1)


def _sgu_kernel(u_ref, v_ref, g_ref, lng_ref, lnb_ref, ws_ref, bs_ref, o_ref, *, tm):
    gd = D_B // B_GROUPS
    v = _layer_norm(_gelu(v_ref[...].astype(f32)), lng_ref[...], lnb_ref[...]).astype(bf16)
    for n in range(tm // CHUNK):
        rows = slice(n * CHUNK, (n + 1) * CHUNK)
        for grp in range(B_GROUPS):
            cols = slice(grp * gd, (grp + 1) * gd)
            s = jnp.dot(ws_ref[grp], v[rows, cols], preferred_element_type=f32) + bs_ref[:, cols]
            u = _gelu(u_ref[rows, cols].astype(f32))
            g = g_ref[rows, cols].astype(f32)
            o_ref[rows, cols] = (u * s * _silu(g)).astype(o_ref.dtype)


def _sgu(z2, ln_g, ln_b, w_s, b_map, tm=512):
    n_tok = z2.shape[0]
    c0 = 4 * D_A // D_B
    blk = (tm, D_B)
    const2 = lambda i: (0, 0)
    return pl.pallas_call(
        functools.partial(_sgu_kernel, tm=tm),
        grid=(n_tok // tm,),
        in_specs=[
            pl.BlockSpec(blk, lambda i: (i, c0)),
            pl.BlockSpec(blk, lambda i: (i, c0 + 1)),
            pl.BlockSpec(blk, lambda i: (i, c0 + 2)),
            pl.BlockSpec((1, D_B), const2),
            pl.BlockSpec((1, D_B), const2),
            pl.BlockSpec((B_GROUPS, CHUNK, CHUNK), lambda i: (0, 0, 0)),
            pl.BlockSpec((CHUNK, D_B), const2),
        ],
        out_specs=pl.BlockSpec(blk, lambda i: (i, 0)),
        out_shape=jax.ShapeDtypeStruct((n_tok, D_B), bf16),
        compiler_params=_params("arbitrary"),
        name="sgu",
    )(z2, z2, z2, ln_g, ln_b, w_s, b_map)


def _conv_kernel(a_ref, b_ref, g_ref, cw_ref, cb_ref, lng_ref, lnb_ref, pw_ref, pb_ref, o_ref,
                 h_scr, c_scr, *, t, tt, tm):
    zeros = jnp.zeros((CONV_PAD, D_C), f32)
    h_scr[pl.ds(0, CONV_PAD), :] = zeros
    h_scr[pl.ds(CONV_PAD + t, CONV_PAD), :] = zeros
    h_scr[pl.ds(CONV_PAD, t), :] = a_ref[0].astype(f32) * _sigmoid(b_ref[0].astype(f32))

    shift = CONV_PAD - CONV_W // 2
    n_al = (CONV_W + shift + SUBLANES - 1) // SUBLANES
    win_rows = tt + n_al * SUBLANES

    def conv_tile(ti, carry):
        t0 = pl.multiple_of(ti * tt, tt)
        for cb in range(D_C // LANES):
            cols = slice(cb * LANES, (cb + 1) * LANES)
            win = h_scr[pl.ds(t0, win_rows), cols]
            acc = None
            for b in range(SUBLANES):
                part = None
                for a in range(n_al):
                    k = SUBLANES * a + b - shift
                    if 0 <= k < CONV_W:
                        term = cw_ref[k:k + 1, cols] * win[SUBLANES * a:SUBLANES * a + tt + SUBLANES]
                        part = term if part is None else part + term
                part = part[b:b + tt]
                acc = part if acc is None else acc + part
            c_scr[pl.ds(t0, tt), cols] = acc
        return carry

    lax.fori_loop(0, t // tt, conv_tile, 0)

    def post_tile(ti, carry):
        t0 = pl.multiple_of(ti * tm, tm)
        c = c_scr[pl.ds(t0, tm), :] + cb_ref[...]
        c = _silu(_layer_norm(c, lng_ref[...], lnb_ref[...])).astype(bf16)
        y = jnp.dot(c, pw_ref[...], preferred_element_type=f32) + pb_ref[...]
        g = g_ref[0, pl.ds(t0, tm), :].astype(f32)
        o_ref[0, pl.ds(t0, tm), :] = (y * _silu(g)).astype(o_ref.dtype)
        return carry

    lax.fori_loop(0, t // tm, post_tile, 0)


def _conformer_conv(z3, conv_w, conv_b, ln_g, ln_b, pw_w, pw_b, tt=64, tm=512):
    bsz, t, _ = z3.shape
    assert t % tt == 0 and t % tm == 0
    c0 = (4 * D_A + 3 * D_B) // D_C
    blk = (1, t, D_C)
    const2 = lambda b: (0, 0)
    return pl.pallas_call(
        functools.partial(_conv_kernel, t=t, tt=tt, tm=tm),
        grid=(bsz,),
        in_specs=[
            pl.BlockSpec(blk, lambda b: (b, 0, c0)),
            pl.BlockSpec(blk, lambda b: (b, 0, c0 + 1)),
            pl.BlockSpec(blk, lambda b: (b, 0, c0 + 2)),
            pl.BlockSpec((CONV_W, D_C), const2),
            pl.BlockSpec((1, D_C), const2),
            pl.BlockSpec((1, D_C), const2),
            pl.BlockSpec((1, D_C), const2),
            pl.BlockSpec((D_C, D_C), const2),
            pl.BlockSpec((1, D_C), const2),
        ],
        out_specs=pl.BlockSpec(blk, lambda b: (b, 0, 0)),
        out_shape=jax.ShapeDtypeStruct((bsz, t, D_C), bf16),
        scratch_shapes=[pltpu.VMEM((t + 2 * CONV_PAD, D_C), f32), pltpu.VMEM((t, D_C), f32)],
        compiler_params=_params("arbitrary"),
        name="conformer_conv",
    )(z3, z3, z3, conv_w, conv_b, ln_g, ln_b, pw_w, pw_b)


def _out_proj_kernel(ya_ref, yb_ref, yc_ref, w_ref, g_ref, x_ref, gn_ref, o_ref, *rest, emit_h):
    cat_scr = rest[-1]
    cat_scr[:, 0:D_A] = ya_ref[...]
    cat_scr[:, D_A:D_A + D_B] = yb_ref[...]
    cat_scr[:, D_A + D_B:D_MIX] = yc_ref[...]
    y = jnp.dot(cat_scr[...], w_ref[...], preferred_element_type=f32)
    x_new = x_ref[...] + _rms_norm(y, g_ref[...])
    o_ref[...] = x_new
    if emit_h:
        h_ref = rest[0]
        h_ref[...] = _rms_norm(x_new, gn_ref[...]).astype(h_ref.dtype)


def _out_proj(ya, yb, yc, w, g, x2, g_next, emit_h, tm=512):
    n_tok = x2.shape[0]
    const2 = lambda i: (0, 0)
    tile = lambda width: pl.BlockSpec((tm, width), lambda i: (i, 0))
    out_shape = [jax.ShapeDtypeStruct((n_tok, D_MODEL), f32)]
    out_specs = [tile(D_MODEL)]
    if emit_h:
        out_shape.append(jax.ShapeDtypeStruct((n_tok, D_MODEL), bf16))
        out_specs.append(tile(D_MODEL))
    return pl.pallas_call(
        functools.partial(_out_proj_kernel, emit_h=emit_h),
        grid=(n_tok // tm,),
        in_specs=[
            tile(D_A), tile(D_B), tile(D_C),
            pl.BlockSpec((D_MIX, D_MODEL), const2),
            pl.BlockSpec((1, D_MODEL), const2),
            tile(D_MODEL),
            pl.BlockSpec((1, D_MODEL), const2),
        ],
        out_specs=out_specs,
        out_shape=out_shape,
        scratch_shapes=[pltpu.VMEM((tm, D_MIX), bf16)],
        compiler_params=_params("arbitrary"),
        name="out_proj",
    )(ya, yb, yc, w, g, x2, g_next)


def kernel(x, pre_norm_g, w_in, attn_rpb, sgu_ln_g, sgu_ln_b, sgu_w, sgu_b, conv_w, conv_b,
           conv_ln_g, conv_ln_b, conv_pw_w, conv_pw_b, w_out, post_norm_g):
    bsz, t, d = x.shape
    depth = w_in.shape[0]
    assert d == D_MODEL and t % GRID_W == 0 and t % CHUNK == 0
    n_tok = bsz * t
    x2 = x.reshape(n_tok, d)
    row = lambda a: a.reshape(1, -1).astype(f32)
    h = _pre_norm(x2, row(pre_norm_g[0]))
    for l in range(depth):
        z2 = _in_proj(h, w_in, l)
        z3 = z2.reshape(bsz, t, D_IN)
        y_a = _attention(z3, _attention_bias_vectors(attn_rpb[l]))
        b_map = jnp.repeat(sgu_b[l].astype(f32).T, D_B // B_GROUPS, axis=1)
        y_b = _sgu(z2, row(sgu_ln_g[l]), row(sgu_ln_b[l]), sgu_w[l].astype(bf16), b_map)
        y_c = _conformer_conv(z3, conv_w[l].reshape(CONV_W, D_C).astype(f32), row(conv_b[l]),
                              row(conv_ln_g[l]), row(conv_ln_b[l]), conv_pw_w[l].astype(bf16),
                              row(conv_pw_b[l]))
        last = l == depth - 1
        next_g = post_norm_g[l] if last else pre_norm_g[l + 1]
        outs = _out_proj(y_a.reshape(n_tok, D_A), y_b, y_c.reshape(n_tok, D_C), w_out[l].astype(bf16),
                         row(post_norm_g[l]), x2, row(next_g), not last)
        x2 = outs[0]
        if not last:
            h = outs[1]
    return x2.reshape(bsz, t, d)
```

```python
import functools
import math

import jax
import jax.numpy as jnp
from jax import lax
from jax.experimental import pallas as pl
from jax.experimental.pallas import tpu as pltpu

D_MODEL = 2048
GRID_W = 64
WIN_H = 8
WIN_W = 16
HEAD_DIM = 128
D_A = 1024
N_HEADS = 8
CHUNK = 128
D_B = 512
B_GROUPS = 4
D_C = 512
CONV_W = 31
D_MIX = D_A + D_B + D_C
D_IN = 4 * D_A + 3 * D_B + 3 * D_C
EPS = 1e-6
MASKED = -1e30

LANES = 128
SUBLANES = 8
VMEM_LIMIT = 56 * 1024 * 1024

Q_ROWS = 4
K_ROWS = Q_ROWS + WIN_H
PAIR = 2 * GRID_W
assert PAIR == LANES

CONV_PAD = 16

f32 = jnp.float32
bf16 = jnp.bfloat16


def _sigmoid(x):
    return 1.0 / (1.0 + jnp.exp(-x))


def _silu(x):
    return x * _sigmoid(x)


def _gelu(x):
    return 0.5 * x * (1.0 + lax.erf(x * math.sqrt(0.5)))


def _layer_norm(x, g, b):
    mu = jnp.mean(x, axis=-1, keepdims=True)
    xc = x - mu
    var = jnp.mean(xc * xc, axis=-1, keepdims=True)
    return xc * lax.rsqrt(var + EPS) * g + b


def _rms_norm(x, g):
    ms = jnp.mean(x * x, axis=-1, keepdims=True)
    return x * lax.rsqrt(ms + EPS) * g


def _params(*semantics):
    return pltpu.CompilerParams(dimension_semantics=semantics, vmem_limit_bytes=VMEM_LIMIT)


def _pre_norm_kernel(x_ref, g_ref, h_ref):
    h_ref[...] = _rms_norm(x_ref[...], g_ref[...]).astype(h_ref.dtype)


def _pre_norm(x2, g, tm=512):
    n_tok = x2.shape[0]
    return pl.pallas_call(
        _pre_norm_kernel,
        grid=(n_tok // tm,),
        in_specs=[pl.BlockSpec((tm, D_MODEL), lambda i: (i, 0)),
                  pl.BlockSpec((1, D_MODEL), lambda i: (0, 0))],
        out_specs=pl.BlockSpec((tm, D_MODEL), lambda i: (i, 0)),
        out_shape=jax.ShapeDtypeStruct((n_tok, D_MODEL), bf16),
        compiler_params=_params("arbitrary"),
        name="pre_norm",
    )(x2, g)


def _in_proj_kernel(h_ref, w_ref, z_ref, wb_scr):
    @pl.when(pl.program_id(1) == 0)
    def _():
        wb_scr[...] = w_ref[...].astype(bf16)

    z_ref[...] = jnp.dot(h_ref[...], wb_scr[...], preferred_element_type=f32).astype(z_ref.dtype)


def _in_proj(h, w_in, layer, tm=1024, tn=1024):
    n_tok = h.shape[0]
    return pl.pallas_call(
        _in_proj_kernel,
        grid=(D_IN // tn, n_tok // tm),
        in_specs=[
            pl.BlockSpec((tm, D_MODEL), lambda j, i: (i, 0)),
            pl.BlockSpec((None, D_MODEL, tn), lambda j, i: (layer, 0, j)),
        ],
        out_specs=pl.BlockSpec((tm, tn), lambda j, i: (i, j)),
        out_shape=jax.ShapeDtypeStruct((n_tok, D_IN), bf16),
        scratch_shapes=[pltpu.VMEM((D_MODEL, tn), bf16)],
        compiler_params=_params("arbitrary", "arbitrary"),
        name="in_proj",
    )(h, w_in)


def _attn_kernel(q_ref, k_ref, v_ref, g_ref, vec_ref, o_ref, bias_ref, *, rows):
    scale = HEAD_DIM ** -0.5
    nq = Q_ROWS * GRID_W
    nk = K_ROWS * GRID_W
    w_idx = lax.broadcasted_iota(jnp.int32, (GRID_W, PAIR), 0)
    lane = lax.broadcasted_iota(jnp.int32, (GRID_W, PAIR), 1)
    left_half = lane < GRID_W
    kc = jnp.where(left_half, lane, lane - GRID_W)
    col_start = jnp.clip(w_idx - WIN_W // 2, 0, GRID_W - WIN_W)
    in_win = (kc >= col_start) & (kc < col_start + WIN_W)
    for e in range(2 * WIN_H):
        vec = jnp.broadcast_to(vec_ref[0, e:e + 1, :], (GRID_W, PAIR))
        bias_ref[e] = jnp.where(in_win, pltpu.roll(vec, 0, 1, stride=1, stride_axis=0), MASKED)

    n_blk = rows // Q_ROWS
    first_key_row = lambda blk: min(max(blk * Q_ROWS - WIN_H // 2, 0), rows - K_ROWS)

    def scores(blk):
        q0, k0 = blk * nq, first_key_row(blk) * GRID_W
        q = (q_ref[0, q0:q0 + nq, :].astype(f32) * scale).astype(bf16)
        kw = k_ref[0, k0:k0 + nk, :]
        return lax.dot_general(q, kw, (((1,), (1,)), ((), ())), preferred_element_type=f32)

    s_next = scores(0)
    for blk in range(n_blk):
        ks = first_key_row(blk)
        q0, k0 = blk * nq, ks * GRID_W
        vw = v_ref[0, k0:k0 + nk, :]
        s = s_next
        if blk + 1 < n_blk:
            s_next = scores(blk + 1)

        p_rows, inv_l = [], []
        for rq in range(Q_ROWS):
            r = blk * Q_ROWS + rq
            rs = min(max(r - WIN_H // 2, 0), rows - WIN_H)
            pieces = {}
            for ip in range(K_ROWS // 2):
                kr = ks + 2 * ip
                ok0, ok1 = rs <= kr < rs + WIN_H, rs <= kr + 1 < rs + WIN_H
                if not (ok0 or ok1):
                    continue
                dr0 = kr - r + (WIN_H - 1)
                sub = s[rq * GRID_W:(rq + 1) * GRID_W, ip * PAIR:(ip + 1) * PAIR]
                if ok0 and ok1:
                    bias = bias_ref[dr0 + 1]
                elif ok0:
                    bias = jnp.where(left_half, bias_ref[dr0 + 1], MASKED)
                else:
                    bias = jnp.where(left_half, MASKED, bias_ref[dr0 + 1])
                pieces[ip] = sub + bias
            m = jnp.max(functools.reduce(jnp.maximum, pieces.values()), axis=-1, keepdims=True)
            pieces = {ip: jnp.exp(sv - m) for ip, sv in pieces.items()}
            l = jnp.sum(functools.reduce(jnp.add, pieces.values()), axis=-1, keepdims=True)
            inv_l.append(1.0 / l)
            zero = jnp.zeros((GRID_W, PAIR), bf16)
            p_rows.append(jnp.concatenate(
                [pieces[ip].astype(bf16) if ip in pieces else zero for ip in range(K_ROWS // 2)], axis=1))
        p = jnp.concatenate(p_rows, axis=0)
        o = jnp.dot(p, vw, preferred_element_type=f32)
        for rq in range(Q_ROWS):
            rows_q = slice(q0 + rq * GRID_W, q0 + (rq + 1) * GRID_W)
            g = g_ref[0, rows_q, :].astype(f32)
            o_rq = o[rq * GRID_W:(rq + 1) * GRID_W] * inv_l[rq]
            o_ref[0, rows_q, :] = (o_rq * _silu(g)).astype(o_ref.dtype)


def _attention(z3, vecs):
    bsz, t, _ = z3.shape
    rows = t // GRID_W
    assert rows % Q_ROWS == 0 and rows >= K_ROWS
    blk = (1, t, HEAD_DIM)
    return pl.pallas_call(
        functools.partial(_attn_kernel, rows=rows),
        grid=(bsz, N_HEADS),
        in_specs=[
            pl.BlockSpec(blk, lambda b, h: (b, 0, h)),
            pl.BlockSpec(blk, lambda b, h: (b, 0, N_HEADS + h)),
            pl.BlockSpec(blk, lambda b, h: (b, 0, 2 * N_HEADS + h)),
            pl.BlockSpec(blk, lambda b, h: (b, 0, 3 * N_HEADS + h)),
            pl.BlockSpec((1, 2 * WIN_H, PAIR), lambda b, h: (h, 0, 0)),
        ],
        out_specs=pl.BlockSpec(blk, lambda b, h: (b, 0, h)),
        out_shape=jax.ShapeDtypeStruct((bsz, t, D_A), bf16),
        scratch_shapes=[pltpu.VMEM((2 * WIN_H, GRID_W, PAIR), f32)],
        compiler_params=_params("arbitrary", "arbitrary"),
        name="nbr_attention",
    )(z3, z3, z3, z3, vecs)


def _attention_bias_vectors(rpb):
    half = WIN_W - 1
    pad = jnp.full((N_HEADS, 1, 2 * WIN_W - 1), MASKED, f32)
    ext = jnp.concatenate([pad, rpb.astype(f32), pad], axis=1)
    lower, upper = ext[:, :-1], ext[:, 1:]
    gap = jnp.full((N_HEADS, 2 * WIN_H, GRID_W - 2 * half - 1), MASKED, f32)
    return jnp.concatenate([lower[..., half:], gap, upper, gap, lower[..., :half]], axis=-1)


def _conv_kernel(a_ref, b_ref, g_ref, cw_ref, cb_ref, lng_ref, lnb_ref, pw_ref, pb_ref, o_ref,
                 h_scr, c_scr, *, t, tt, tm):
    zeros = jnp.zeros((CONV_PAD, D_C), f32)
    h_scr[pl.ds(0, CONV_PAD), :] = zeros
    h_scr[pl.ds(CONV_PAD + t, CONV_PAD), :] = zeros
    h_scr[pl.ds(CONV_PAD, t), :] = a_ref[0].astype(f32) * _sigmoid(b_ref[0].astype(f32))

    shift = CONV_PAD - CONV_W // 2
    n_al = (CONV_W + shift + SUBLANES - 1) // SUBLANES
    win_rows = tt + n_al * SUBLANES

    def conv_tile(ti, carry):
        t0 = pl.multiple_of(ti * tt, tt)
        for cb in range(D_C // LANES):
            cols = slice(cb * LANES, (cb + 1) * LANES)
            win = h_scr[pl.ds(t0, win_rows), cols]
            acc = None
            for b in range(SUBLANES):
                part = None
                for a in range(n_al):
                    k = SUBLANES * a + b - shift
                    if 0 <= k < CONV_W:
                        term = cw_ref[k:k + 1, cols] * win[SUBLANES * a:SUBLANES * a + tt + SUBLANES]
                        part = term if part is None else part + term
                part = part[b:b + tt]
                acc = part if acc is None else acc + part
            c_scr[pl.ds(t0, tt), cols] = acc
        return carry

    lax.fori_loop(0, t // tt, conv_tile, 0)

    def post_tile(ti, carry):
        t0 = pl.multiple_of(ti * tm, tm)
        c = c_scr[pl.ds(t0, tm), :] + cb_ref[...]
        c = _silu(_layer_norm(c, lng_ref[...], lnb_ref[...])).astype(bf16)
        y = jnp.dot(c, pw_ref[...], preferred_element_type=f32) + pb_ref[...]
        g = g_ref[0, pl.ds(t0, tm), :].astype(f32)
        o_ref[0, pl.ds(t0, tm), :] = (y * _silu(g)).astype(o_ref.dtype)
        return carry

    lax.fori_loop(0, t // tm, post_tile, 0)


def _conformer_conv(z3, conv_w, conv_b, ln_g, ln_b, pw_w, pw_b, tt=64, tm=512):
    bsz, t, _ = z3.shape
    assert t % tt == 0 and t % tm == 0
    c0 = (4 * D_A + 3 * D_B) // D_C
    blk = (1, t, D_C)
    const2 = lambda b: (0, 0)
    return pl.pallas_call(
        functools.partial(_conv_kernel, t=t, tt=tt, tm=tm),
        grid=(bsz,),
        in_specs=[
            pl.BlockSpec(blk, lambda b: (b, 0, c0)),
            pl.BlockSpec(blk, lambda b: (b, 0, c0 + 1)),
            pl.BlockSpec(blk, lambda b: (b, 0, c0 + 2)),
            pl.BlockSpec((CONV_W, D_C), const2),
            pl.BlockSpec((1, D_C), const2),
            pl.BlockSpec((1, D_C), const2),
            pl.BlockSpec((1, D_C), const2),
            pl.BlockSpec((D_C, D_C), const2),
            pl.BlockSpec((1, D_C), const2),
        ],
        out_specs=pl.BlockSpec(blk, lambda b: (b, 0, 0)),
        out_shape=jax.ShapeDtypeStruct((bsz, t, D_C), bf16),
        scratch_shapes=[pltpu.VMEM((t + 2 * CONV_PAD, D_C), f32), pltpu.VMEM((t, D_C), f32)],
        compiler_params=_params("arbitrary"),
        name="conformer_conv",
    )(z3, z3, z3, conv_w, conv_b, ln_g, ln_b, pw_w, pw_b)


def _spatial_gating(u_ref, v_ref, g_ref, lng_ref, lnb_ref, ws_ref, bs_ref, o_ref, col0, r0, nr):
    gd = D_B // B_GROUPS
    for n in range(r0 // CHUNK, (r0 + nr) // CHUNK):
        rows = slice(n * CHUNK, (n + 1) * CHUNK)
        v = _layer_norm(_gelu(v_ref[rows, :].astype(f32)), lng_ref[...], lnb_ref[...]).astype(bf16)
        for grp in range(B_GROUPS):
            cols = slice(grp * gd, (grp + 1) * gd)
            s = jnp.dot(ws_ref[grp], v[:, cols], preferred_element_type=f32) + bs_ref[:, cols]
            u = _gelu(u_ref[rows, cols].astype(f32))
            g = g_ref[rows, cols].astype(f32)
            o_ref[rows, col0 + grp * gd:col0 + (grp + 1) * gd] = (u * s * _silu(g)).astype(o_ref.dtype)


def _out_sgu_kernel(ya_ref, yc_ref, w_ref, pg_ref, x_ref, gn_ref,
                    u_ref, v_ref, g_ref, lng_ref, lnb_ref, ws_ref, bs_ref, o_ref, *rest, tm, n_sub, emit_h):
    cat_scr = rest[-1]
    sub = tm // n_sub

    def project(i):
        rows = slice(i * sub, (i + 1) * sub)
        cat_scr[rows, 0:D_A] = ya_ref[rows, :]
        _spatial_gating(u_ref, v_ref, g_ref, lng_ref, lnb_ref, ws_ref, bs_ref, cat_scr, D_A, i * sub, sub)
        cat_scr[rows, D_A + D_B:D_MIX] = yc_ref[rows, :]
        return jnp.dot(cat_scr[rows, :], w_ref[...], preferred_element_type=f32)

    def finish(i, y):
        rows = slice(i * sub, (i + 1) * sub)
        x_new = x_ref[rows, :] + _rms_norm(y, pg_ref[...])
        o_ref[rows, :] = x_new
        if emit_h:
            h_ref = rest[0]
            h_ref[rows, :] = _rms_norm(x_new, gn_ref[...]).astype(h_ref.dtype)

    y_prev = project(0)
    for i in range(1, n_sub):
        y_cur = project(i)
        finish(i - 1, y_prev)
        y_prev = y_cur
    finish(n_sub - 1, y_prev)


def _out_proj_sgu(ya, yc, z2, w, post_g, x2, next_g, ln_g, ln_b, w_s, b_map, *, emit_h, tm=512, n_sub=2):
    n_tok = x2.shape[0]
    assert tm % (n_sub * CHUNK) == 0
    c0 = 4 * D_A // D_B
    const2 = lambda i: (0, 0)
    tile = lambda width, col=0: pl.BlockSpec((tm, width), lambda i: (i, col))
    out_shape = [jax.ShapeDtypeStruct((n_tok, D_MODEL), f32)]
    out_specs = [tile(D_MODEL)]
    if emit_h:
        out_shape.append(jax.ShapeDtypeStruct((n_tok, D_MODEL), bf16))
        out_specs.append(tile(D_MODEL))
    return pl.pallas_call(
        functools.partial(_out_sgu_kernel, tm=tm, n_sub=n_sub, emit_h=emit_h),
        grid=(n_tok // tm,),
        in_specs=[
            tile(D_A), tile(D_C),
            pl.BlockSpec((D_MIX, D_MODEL), const2),
            pl.BlockSpec((1, D_MODEL), const2),
            tile(D_MODEL),
            pl.BlockSpec((1, D_MODEL), const2),
            tile(D_B, c0), tile(D_B, c0 + 1), tile(D_B, c0 + 2),
            pl.BlockSpec((1, D_B), const2),
            pl.BlockSpec((1, D_B), const2),
            pl.BlockSpec((B_GROUPS, CHUNK, CHUNK), lambda i: (0, 0, 0)),
            pl.BlockSpec((CHUNK, D_B), const2),
        ],
        out_specs=out_specs,
        out_shape=out_shape,
        scratch_shapes=[pltpu.VMEM((tm, D_MIX), bf16)],
        compiler_params=_params("arbitrary"),
        name="out_proj_sgu",
    )(ya, yc, w, post_g, x2, next_g, z2, z2, z2, ln_g, ln_b, w_s, b_map)


def kernel(x, pre_norm_g, w_in, attn_rpb, sgu_ln_g, sgu_ln_b, sgu_w, sgu_b, conv_w, conv_b,
           conv_ln_g, conv_ln_b, conv_pw_w, conv_pw_b, w_out, post_norm_g):
    bsz, t, d = x.shape
    depth = w_in.shape[0]
    assert d == D_MODEL and t % GRID_W == 0 and t % CHUNK == 0
    n_tok = bsz * t
    x2 = x.reshape(n_tok, d)
    row = lambda a: a.reshape(1, -1).astype(f32)
    h = _pre_norm(x2, row(pre_norm_g[0]))
    for l in range(depth):
        z2 = _in_proj(h, w_in, l)
        z3 = z2.reshape(bsz, t, D_IN)
        y_a = _attention(z3, _attention_bias_vectors(attn_rpb[l]))
        b_map = jnp.repeat(sgu_b[l].astype(f32).T, D_B // B_GROUPS, axis=1)
        y_c = _conformer_conv(z3, conv_w[l].reshape(CONV_W, D_C).astype(f32), row(conv_b[l]),
                              row(conv_ln_g[l]), row(conv_ln_b[l]), conv_pw_w[l].astype(bf16),
                              row(conv_pw_b[l]))
        last = l == depth - 1
        next_g = post_norm_g[l] if last else pre_norm_g[l + 1]
        outs = _out_proj_sgu(
            y_a.reshape(n_tok, D_A), y_c.reshape(n_tok, D_C), z2, w_out[l].astype(bf16),
            row(post_norm_g[l]), x2, row(next_g), row(sgu_ln_g[l]), row(sgu_ln_b[l]),
            sgu_w[l].astype(bf16), b_map, emit_h=not last)
        x2 = outs[0]
        if not last:
            h = outs[1]
    return x2.reshape(bsz, t, d)
```

```python
import functools
import math

import jax
import jax.numpy as jnp
from jax import lax
from jax.experimental import pallas as pl
from jax.experimental.pallas import tpu as pltpu

D_MODEL = 2048
GRID_W = 64
WIN_H = 8
WIN_W = 16
HEAD_DIM = 128
D_A = 1024
N_HEADS = 8
CHUNK = 128
D_B = 512
B_GROUPS = 4
D_C = 512
CONV_W = 31
D_MIX = D_A + D_B + D_C
D_IN = 4 * D_A + 3 * D_B + 3 * D_C
EPS = 1e-6
MASKED = -1e30

LANES = 128
SUBLANES = 8
VMEM_LIMIT = 56 * 1024 * 1024

Q_ROWS = 4
K_ROWS = Q_ROWS + WIN_H
PAIR = 2 * GRID_W
assert PAIR == LANES

CONV_PAD = 16

f32 = jnp.float32
bf16 = jnp.bfloat16


def _sigmoid(x):
    return 1.0 / (1.0 + jnp.exp(-x))


def _silu(x):
    return x * _sigmoid(x)


def _gelu(x):
    return 0.5 * x * (1.0 + lax.erf(x * math.sqrt(0.5)))


def _layer_norm(x, g, b):
    mu = jnp.mean(x, axis=-1, keepdims=True)
    xc = x - mu
    var = jnp.mean(xc * xc, axis=-1, keepdims=True)
    return xc * lax.rsqrt(var + EPS) * g + b


def _rms_norm(x, g):
    ms = jnp.mean(x * x, axis=-1, keepdims=True)
    return x * lax.rsqrt(ms + EPS) * g


def _params(*semantics):
    return pltpu.CompilerParams(dimension_semantics=semantics, vmem_limit_bytes=VMEM_LIMIT)


def _pre_norm_kernel(x_ref, g_ref, h_ref):
    h_ref[...] = _rms_norm(x_ref[...], g_ref[...]).astype(h_ref.dtype)


def _pre_norm(x2, g, tm=512):
    n_tok = x2.shape[0]
    return pl.pallas_call(
        _pre_norm_kernel,
        grid=(n_tok // tm,),
        in_specs=[pl.BlockSpec((tm, D_MODEL), lambda i: (i, 0)),
                  pl.BlockSpec((1, D_MODEL), lambda i: (0, 0))],
        out_specs=pl.BlockSpec((tm, D_MODEL), lambda i: (i, 0)),
        out_shape=jax.ShapeDtypeStruct((n_tok, D_MODEL), bf16),
        compiler_params=_params("arbitrary"),
        name="pre_norm",
    )(x2, g)


def _in_proj_kernel(h_ref, w_ref, z_ref, wb_scr):
    @pl.when(pl.program_id(1) == 0)
    def _():
        wb_scr[...] = w_ref[...].astype(bf16)

    z_ref[...] = jnp.dot(h_ref[...], wb_scr[...], preferred_element_type=f32).astype(z_ref.dtype)


def _in_proj(h, w_in, layer, tm=2048, tn=1024):
    n_tok = h.shape[0]
    return pl.pallas_call(
        _in_proj_kernel,
        grid=(D_IN // tn, n_tok // tm),
        in_specs=[
            pl.BlockSpec((tm, D_MODEL), lambda j, i: (i, 0)),
            pl.BlockSpec((None, D_MODEL, tn), lambda j, i: (layer, 0, j)),
        ],
        out_specs=pl.BlockSpec((tm, tn), lambda j, i: (i, j)),
        out_shape=jax.ShapeDtypeStruct((n_tok, D_IN), bf16),
        scratch_shapes=[pltpu.VMEM((D_MODEL, tn), bf16)],
        compiler_params=_params("arbitrary", "arbitrary"),
        name="in_proj",
    )(h, w_in)


def _attn_kernel(q_ref, k_ref, v_ref, g_ref, vec_ref, o_ref, bias_ref, *, rows):
    scale = HEAD_DIM ** -0.5
    nq = Q_ROWS * GRID_W
    nk = K_ROWS * GRID_W
    w_idx = lax.broadcasted_iota(jnp.int32, (GRID_W, PAIR), 0)
    lane = lax.broadcasted_iota(jnp.int32, (GRID_W, PAIR), 1)
    left_half = lane < GRID_W
    kc = jnp.where(left_half, lane, lane - GRID_W)
    col_start = jnp.clip(w_idx - WIN_W // 2, 0, GRID_W - WIN_W)
    in_win = (kc >= col_start) & (kc < col_start + WIN_W)
    for e in range(2 * WIN_H):
        vec = jnp.broadcast_to(vec_ref[0, e:e + 1, :], (GRID_W, PAIR))
        bias_ref[e] = jnp.where(in_win, pltpu.roll(vec, 0, 1, stride=1, stride_axis=0), MASKED)

    n_blk = rows // Q_ROWS
    first_key_row = lambda blk: min(max(blk * Q_ROWS - WIN_H // 2, 0), rows - K_ROWS)

    def scores(blk):
        q0, k0 = blk * nq, first_key_row(blk) * GRID_W
        q = (q_ref[0, q0:q0 + nq, :].astype(f32) * scale).astype(bf16)
        kw = k_ref[0, k0:k0 + nk, :]
        return lax.dot_general(q, kw, (((1,), (1,)), ((), ())), preferred_element_type=f32)

    s_next = scores(0)
    for blk in range(n_blk):
        ks = first_key_row(blk)
        q0, k0 = blk * nq, ks * GRID_W
        vw = v_ref[0, k0:k0 + nk, :]
        s = s_next
        if blk + 1 < n_blk:
            s_next = scores(blk + 1)

        p_rows, inv_l = [], []
        for rq in range(Q_ROWS):
            r = blk * Q_ROWS + rq
            rs = min(max(r - WIN_H // 2, 0), rows - WIN_H)
            pieces = {}
            for ip in range(K_ROWS // 2):
                kr = ks + 2 * ip
                ok0, ok1 = rs <= kr < rs + WIN_H, rs <= kr + 1 < rs + WIN_H
                if not (ok0 or ok1):
                    continue
                dr0 = kr - r + (WIN_H - 1)
                sub = s[rq * GRID_W:(rq + 1) * GRID_W, ip * PAIR:(ip + 1) * PAIR]
                if ok0 and ok1:
                    bias = bias_ref[dr0 + 1]
                elif ok0:
                    bias = jnp.where(left_half, bias_ref[dr0 + 1], MASKED)
                else:
                    bias = jnp.where(left_half, MASKED, bias_ref[dr0 + 1])
                pieces[ip] = sub + bias
            m = jnp.max(functools.reduce(jnp.maximum, pieces.values()), axis=-1, keepdims=True)
            pieces = {ip: jnp.exp(sv - m) for ip, sv in pieces.items()}
            l = jnp.sum(functools.reduce(jnp.add, pieces.values()), axis=-1, keepdims=True)
            inv_l.append(1.0 / l)
            zero = jnp.zeros((GRID_W, PAIR), bf16)
            p_rows.append(jnp.concatenate(
                [pieces[ip].astype(bf16) if ip in pieces else zero for ip in range(K_ROWS // 2)], axis=1))
        p = jnp.concatenate(p_rows, axis=0)
        o = jnp.dot(p, vw, preferred_element_type=f32)
        for rq in range(Q_ROWS):
            rows_q = slice(q0 + rq * GRID_W, q0 + (rq + 1) * GRID_W)
            g = g_ref[0, rows_q, :].astype(f32)
            o_rq = o[rq * GRID_W:(rq + 1) * GRID_W] * inv_l[rq]
            o_ref[0, rows_q, :] = (o_rq * _silu(g)).astype(o_ref.dtype)


def _attention(z3, vecs):
    bsz, t, _ = z3.shape
    rows = t // GRID_W
    assert rows % Q_ROWS == 0 and rows >= K_ROWS
    blk = (1, t, HEAD_DIM)
    return pl.pallas_call(
        functools.partial(_attn_kernel, rows=rows),
        grid=(bsz, N_HEADS),
        in_specs=[
            pl.BlockSpec(blk, lambda b, h: (b, 0, h)),
            pl.BlockSpec(blk, lambda b, h: (b, 0, N_HEADS + h)),
            pl.BlockSpec(blk, lambda b, h: (b, 0, 2 * N_HEADS + h)),
            pl.BlockSpec(blk, lambda b, h: (b, 0, 3 * N_HEADS + h)),
            pl.BlockSpec((1, 2 * WIN_H, PAIR), lambda b, h: (h, 0, 0)),
        ],
        out_specs=pl.BlockSpec(blk, lambda b, h: (b, 0, h)),
        out_shape=jax.ShapeDtypeStruct((bsz, t, D_A), bf16),
        scratch_shapes=[pltpu.VMEM((2 * WIN_H, GRID_W, PAIR), f32)],
        compiler_params=_params("arbitrary", "arbitrary"),
        name="nbr_attention",
    )(z3, z3, z3, z3, vecs)


def _attention_bias_vectors(rpb):
    half = WIN_W - 1
    pad = jnp.full((N_HEADS, 1, 2 * WIN_W - 1), MASKED, f32)
    ext = jnp.concatenate([pad, rpb.astype(f32), pad], axis=1)
    lower, upper = ext[:, :-1], ext[:, 1:]
    gap = jnp.full((N_HEADS, 2 * WIN_H, GRID_W - 2 * half - 1), MASKED, f32)
    return jnp.concatenate([lower[..., half:], gap, upper, gap, lower[..., :half]], axis=-1)


def _conv_kernel(a_ref, b_ref, g_ref, cw_ref, cb_ref, lng_ref, lnb_ref, pw_ref, pb_ref, o_ref,
                 h_scr, c_scr, *, t, tt, tm):
    zeros = jnp.zeros((CONV_PAD, D_C), f32)
    h_scr[pl.ds(0, CONV_PAD), :] = zeros
    h_scr[pl.ds(CONV_PAD + t, CONV_PAD), :] = zeros
    h_scr[pl.ds(CONV_PAD, t), :] = a_ref[0].astype(f32) * _sigmoid(b_ref[0].astype(f32))

    shift = CONV_PAD - CONV_W // 2
    n_al = (CONV_W + shift + SUBLANES - 1) // SUBLANES
    win_rows = tt + n_al * SUBLANES

    def conv_tile(ti, carry):
        t0 = pl.multiple_of(ti * tt, tt)
        for cb in range(D_C // LANES):
            cols = slice(cb * LANES, (cb + 1) * LANES)
            win = h_scr[pl.ds(t0, win_rows), cols]
            acc = None
            for b in range(SUBLANES):
                part = None
                for a in range(n_al):
                    k = SUBLANES * a + b - shift
                    if 0 <= k < CONV_W:
                        term = cw_ref[k:k + 1, cols] * win[SUBLANES * a:SUBLANES * a + tt + SUBLANES]
                        part = term if part is None else part + term
                part = part[b:b + tt]
                acc = part if acc is None else acc + part
            c_scr[pl.ds(t0, tt), cols] = acc
        return carry

    lax.fori_loop(0, t // tt, conv_tile, 0)

    def post_tile(ti, carry):
        t0 = pl.multiple_of(ti * tm, tm)
        c = c_scr[pl.ds(t0, tm), :] + cb_ref[...]
        c = _silu(_layer_norm(c, lng_ref[...], lnb_ref[...])).astype(bf16)
        y = jnp.dot(c, pw_ref[...], preferred_element_type=f32) + pb_ref[...]
        g = g_ref[0, pl.ds(t0, tm), :].astype(f32)
        o_ref[0, pl.ds(t0, tm), :] = (y * _silu(g)).astype(o_ref.dtype)
        return carry

    lax.fori_loop(0, t // tm, post_tile, 0)


def _conformer_conv(z3, conv_w, conv_b, ln_g, ln_b, pw_w, pw_b, tt=64, tm=512):
    bsz, t, _ = z3.shape
    assert t % tt == 0 and t % tm == 0
    c0 = (4 * D_A + 3 * D_B) // D_C
    blk = (1, t, D_C)
    const2 = lambda b: (0, 0)
    return pl.pallas_call(
        functools.partial(_conv_kernel, t=t, tt=tt, tm=tm),
        grid=(bsz,),
        in_specs=[
            pl.BlockSpec(blk, lambda b: (b, 0, c0)),
            pl.BlockSpec(blk, lambda b: (b, 0, c0 + 1)),
            pl.BlockSpec(blk, lambda b: (b, 0, c0 + 2)),
            pl.BlockSpec((CONV_W, D_C), const2),
            pl.BlockSpec((1, D_C), const2),
            pl.BlockSpec((1, D_C), const2),
            pl.BlockSpec((1, D_C), const2),
            pl.BlockSpec((D_C, D_C), const2),
            pl.BlockSpec((1, D_C), const2),
        ],
        out_specs=pl.BlockSpec(blk, lambda b: (b, 0, 0)),
        out_shape=jax.ShapeDtypeStruct((bsz, t, D_C), bf16),
        scratch_shapes=[pltpu.VMEM((t + 2 * CONV_PAD, D_C), f32), pltpu.VMEM((t, D_C), f32)],
        compiler_params=_params("arbitrary"),
        name="conformer_conv",
    )(z3, z3, z3, conv_w, conv_b, ln_g, ln_b, pw_w, pw_b)


def _spatial_gating(u_ref, v_ref, g_ref, lng_ref, lnb_ref, ws_ref, bs_ref, o_ref, col0, r0, nr):
    gd = D_B // B_GROUPS
    for n in range(r0 // CHUNK, (r0 + nr) // CHUNK):
        rows = slice(n * CHUNK, (n + 1) * CHUNK)
        v = _layer_norm(_gelu(v_ref[rows, :].astype(f32)), lng_ref[...], lnb_ref[...]).astype(bf16)
        for grp in range(B_GROUPS):
            cols = slice(grp * gd, (grp + 1) * gd)
            s = jnp.dot(ws_ref[grp], v[:, cols], preferred_element_type=f32) + bs_ref[:, cols]
            u = _gelu(u_ref[rows, cols].astype(f32))
            g = g_ref[rows, cols].astype(f32)
            o_ref[rows, col0 + grp * gd:col0 + (grp + 1) * gd] = (u * s * _silu(g)).astype(o_ref.dtype)


def _out_sgu_kernel(ya_ref, yc_ref, w_ref, pg_ref, x_ref, gn_ref,
                    u_ref, v_ref, g_ref, lng_ref, lnb_ref, ws_ref, bs_ref, o_ref, *rest, tm, n_sub, emit_h):
    cat_scr = rest[-1]
    sub = tm // n_sub

    def project(i):
        rows = slice(i * sub, (i + 1) * sub)
        cat_scr[rows, 0:D_A] = ya_ref[rows, :]
        _spatial_gating(u_ref, v_ref, g_ref, lng_ref, lnb_ref, ws_ref, bs_ref, cat_scr, D_A, i * sub, sub)
        cat_scr[rows, D_A + D_B:D_MIX] = yc_ref[rows, :]
        return jnp.dot(cat_scr[rows, :], w_ref[...], preferred_element_type=f32)

    def finish(i, y):
        rows = slice(i * sub, (i + 1) * sub)
        x_new = x_ref[rows, :] + _rms_norm(y, pg_ref[...])
        o_ref[rows, :] = x_new
        if emit_h:
            h_ref = rest[0]
            h_ref[rows, :] = _rms_norm(x_new, gn_ref[...]).astype(h_ref.dtype)

    y_prev = project(0)
    for i in range(1, n_sub):
        y_cur = project(i)
        finish(i - 1, y_prev)
        y_prev = y_cur
    finish(n_sub - 1, y_prev)


def _out_proj_sgu(ya, yc, z2, w, post_g, x2, next_g, ln_g, ln_b, w_s, b_map, *, emit_h, tm=512, n_sub=2):
    n_tok = x2.shape[0]
    assert tm % (n_sub * CHUNK) == 0
    c0 = 4 * D_A // D_B
    const2 = lambda i: (0, 0)
    tile = lambda width, col=0: pl.BlockSpec((tm, width), lambda i: (i, col))
    out_shape = [jax.ShapeDtypeStruct((n_tok, D_MODEL), f32)]
    out_specs = [tile(D_MODEL)]
    if emit_h:
        out_shape.append(jax.ShapeDtypeStruct((n_tok, D_MODEL), bf16))
        out_specs.append(tile(D_MODEL))
    return pl.pallas_call(
        functools.partial(_out_sgu_kernel, tm=tm, n_sub=n_sub, emit_h=emit_h),
        grid=(n_tok // tm,),
        in_specs=[
            tile(D_A), tile(D_C),
            pl.BlockSpec((D_MIX, D_MODEL), const2),
            pl.BlockSpec((1, D_MODEL), const2),
            tile(D_MODEL),
            pl.BlockSpec((1, D_MODEL), const2),
            tile(D_B, c0), tile(D_B, c0 + 1), tile(D_B, c0 + 2),
            pl.BlockSpec((1, D_B), const2),
            pl.BlockSpec((1, D_B), const2),
            pl.BlockSpec((B_GROUPS, CHUNK, CHUNK), lambda i: (0, 0, 0)),
            pl.BlockSpec((CHUNK, D_B), const2),
        ],
        out_specs=out_specs,
        out_shape=out_shape,
        scratch_shapes=[pltpu.VMEM((tm, D_MIX), bf16)],
        compiler_params=_params("arbitrary"),
        name="out_proj_sgu",
    )(ya, yc, w, post_g, x2, next_g, z2, z2, z2, ln_g, ln_b, w_s, b_map)


def kernel(x, pre_norm_g, w_in, attn_rpb, sgu_ln_g, sgu_ln_b, sgu_w, sgu_b, conv_w, conv_b,
           conv_ln_g, conv_ln_b, conv_pw_w, conv_pw_b, w_out, post_norm_g):
    bsz, t, d = x.shape
    depth = w_in.shape[0]
    assert d == D_MODEL and t % GRID_W == 0 and t % CHUNK == 0
    n_tok = bsz * t
    x2 = x.reshape(n_tok, d)
    row = lambda a: a.reshape(1, -1).astype(f32)
    h = _pre_norm(x2, row(pre_norm_g[0]))
    for l in range(depth):
        z2 = _in_proj(h, w_in, l)
        z3 = z2.reshape(bsz, t, D_IN)
        y_a = _attention(z3, _attention_bias_vectors(attn_rpb[l]))
        b_map = jnp.repeat(sgu_b[l].astype(f32).T, D_B // B_GROUPS, axis=1)
        y_c = _conformer_conv(z3, conv_w[l].reshape(CONV_W, D_C).astype(f32), row(conv_b[l]),
                              row(conv_ln_g[l]), row(conv_ln_b[l]), conv_pw_w[l].astype(bf16),
                              row(conv_pw_b[l]))
        last = l == depth - 1
        next_g = post_norm_g[l] if last else pre_norm_g[l + 1]
        outs = _out_proj_sgu(
            y_a.reshape(n_tok, D_A), y_c.reshape(n_tok, D_C), z2, w_out[l].astype(bf16),
            row(post_norm_g[l]), x2, row(next_g), row(sgu_ln_g[l]), row(sgu_ln_b[l]),
            sgu_w[l].astype(bf16), b_map, emit_h=not last)
        x2 = outs[0]
        if not last:
            h = outs[1]
    return x2.reshape(bsz, t, d)
```

```python
import functools
import math

import jax
import jax.numpy as jnp
from jax import lax
from jax.experimental import pallas as pl
from jax.experimental.pallas import tpu as pltpu

D_MODEL = 2048
GRID_W = 64
WIN_H = 8
WIN_W = 16
HEAD_DIM = 128
D_A = 1024
N_HEADS = 8
CHUNK = 128
D_B = 512
B_GROUPS = 4
D_C = 512
CONV_W = 31
D_MIX = D_A + D_B + D_C
D_IN = 4 * D_A + 3 * D_B + 3 * D_C
EPS = 1e-6
MASKED = -1e30

LANES = 128
SUBLANES = 8
VMEM_LIMIT = 56 * 1024 * 1024

Q_ROWS = 4
K_ROWS = Q_ROWS + WIN_H
PAIR = 2 * GRID_W
assert PAIR == LANES

CONV_PAD = 16

f32 = jnp.float32
bf16 = jnp.bfloat16


def _sigmoid(x):
    return 1.0 / (1.0 + jnp.exp(-x))


def _silu(x):
    return x * _sigmoid(x)


def _gelu(x):
    return 0.5 * x * (1.0 + lax.erf(x * math.sqrt(0.5)))


def _layer_norm(x, g, b):
    mu = jnp.mean(x, axis=-1, keepdims=True)
    xc = x - mu
    var = jnp.mean(xc * xc, axis=-1, keepdims=True)
    return xc * lax.rsqrt(var + EPS) * g + b


def _rms_norm(x, g):
    ms = jnp.mean(x * x, axis=-1, keepdims=True)
    return x * lax.rsqrt(ms + EPS) * g


def _params(*semantics):
    return pltpu.CompilerParams(dimension_semantics=semantics, vmem_limit_bytes=VMEM_LIMIT)


def _pre_norm_kernel(x_ref, g_ref, h_ref):
    h_ref[...] = _rms_norm(x_ref[...], g_ref[...]).astype(h_ref.dtype)


def _pre_norm(x2, g, tm=1024):
    n_tok = x2.shape[0]
    return pl.pallas_call(
        _pre_norm_kernel,
        grid=(n_tok // tm,),
        in_specs=[pl.BlockSpec((tm, D_MODEL), lambda i: (i, 0)),
                  pl.BlockSpec((1, D_MODEL), lambda i: (0, 0))],
        out_specs=pl.BlockSpec((tm, D_MODEL), lambda i: (i, 0)),
        out_shape=jax.ShapeDtypeStruct((n_tok, D_MODEL), bf16),
        compiler_params=_params("arbitrary"),
        name="pre_norm",
    )(x2, g)


def _in_proj_kernel(h_ref, w_ref, z_ref, wb_scr):
    @pl.when(pl.program_id(1) == 0)
    def _():
        wb_scr[...] = w_ref[...].astype(bf16)

    z_ref[...] = jnp.dot(h_ref[...], wb_scr[...], preferred_element_type=f32).astype(z_ref.dtype)


def _in_proj(h, w_in, layer, tm=2048, tn=1024):
    n_tok = h.shape[0]
    return pl.pallas_call(
        _in_proj_kernel,
        grid=(D_IN // tn, n_tok // tm),
        in_specs=[
            pl.BlockSpec((tm, D_MODEL), lambda j, i: (i, 0)),
            pl.BlockSpec((None, D_MODEL, tn), lambda j, i: (layer, 0, j)),
        ],
        out_specs=pl.BlockSpec((tm, tn), lambda j, i: (i, j)),
        out_shape=jax.ShapeDtypeStruct((n_tok, D_IN), bf16),
        scratch_shapes=[pltpu.VMEM((D_MODEL, tn), bf16)],
        compiler_params=_params("arbitrary", "arbitrary"),
        name="in_proj",
    )(h, w_in)


def _attn_kernel(q_ref, k_ref, v_ref, g_ref, vec_ref, o_ref, bias_ref, *, rows, heads):
    scale = HEAD_DIM ** -0.5
    nq = Q_ROWS * GRID_W
    nk = K_ROWS * GRID_W
    w_idx = lax.broadcasted_iota(jnp.int32, (GRID_W, PAIR), 0)
    lane = lax.broadcasted_iota(jnp.int32, (GRID_W, PAIR), 1)
    left_half = lane < GRID_W
    kc = jnp.where(left_half, lane, lane - GRID_W)
    col_start = jnp.clip(w_idx - WIN_W // 2, 0, GRID_W - WIN_W)
    in_win = (kc >= col_start) & (kc < col_start + WIN_W)
    for hh in range(heads):
        for e in range(2 * WIN_H):
            vec = jnp.broadcast_to(vec_ref[hh, e:e + 1, :], (GRID_W, PAIR))
            bias_ref[hh, e] = jnp.where(in_win, pltpu.roll(vec, 0, 1, stride=1, stride_axis=0), MASKED)

    n_blk = rows // Q_ROWS
    first_key_row = lambda blk: min(max(blk * Q_ROWS - WIN_H // 2, 0), rows - K_ROWS)

    head_cols = lambda hh: slice(hh * HEAD_DIM, (hh + 1) * HEAD_DIM)
    items = [(hh, blk) for hh in range(heads) for blk in range(n_blk)]

    def scores(item):
        hh, blk = item
        q0, k0 = blk * nq, first_key_row(blk) * GRID_W
        q = (q_ref[0, q0:q0 + nq, head_cols(hh)].astype(f32) * scale).astype(bf16)
        kw = k_ref[0, k0:k0 + nk, head_cols(hh)]
        return lax.dot_general(q, kw, (((1,), (1,)), ((), ())), preferred_element_type=f32)

    s_next = scores(items[0])
    for pos, (hh, blk) in enumerate(items):
        ks = first_key_row(blk)
        q0, k0 = blk * nq, ks * GRID_W
        vw = v_ref[0, k0:k0 + nk, head_cols(hh)]
        s = s_next
        if pos + 1 < len(items):
            s_next = scores(items[pos + 1])

        p_rows, inv_l = [], []
        for rq in range(Q_ROWS):
            r = blk * Q_ROWS + rq
            rs = min(max(r - WIN_H // 2, 0), rows - WIN_H)
            pieces = {}
            for ip in range(K_ROWS // 2):
                kr = ks + 2 * ip
                ok0, ok1 = rs <= kr < rs + WIN_H, rs <= kr + 1 < rs + WIN_H
                if not (ok0 or ok1):
                    continue
                dr0 = kr - r + (WIN_H - 1)
                sub = s[rq * GRID_W:(rq + 1) * GRID_W, ip * PAIR:(ip + 1) * PAIR]
                if ok0 and ok1:
                    bias = bias_ref[hh, dr0 + 1]
                elif ok0:
                    bias = jnp.where(left_half, bias_ref[hh, dr0 + 1], MASKED)
                else:
                    bias = jnp.where(left_half, MASKED, bias_ref[hh, dr0 + 1])
                pieces[ip] = sub + bias
            m = jnp.max(functools.reduce(jnp.maximum, pieces.values()), axis=-1, keepdims=True)
            pieces = {ip: jnp.exp(sv - m) for ip, sv in pieces.items()}
            l = jnp.sum(functools.reduce(jnp.add, pieces.values()), axis=-1, keepdims=True)
            inv_l.append(1.0 / l)
            zero = jnp.zeros((GRID_W, PAIR), bf16)
            p_rows.append(jnp.concatenate(
                [pieces[ip].astype(bf16) if ip in pieces else zero for ip in range(K_ROWS // 2)], axis=1))
        p = jnp.concatenate(p_rows, axis=0)
        o = jnp.dot(p, vw, preferred_element_type=f32)
        for rq in range(Q_ROWS):
            rows_q = slice(q0 + rq * GRID_W, q0 + (rq + 1) * GRID_W)
            g = g_ref[0, rows_q, head_cols(hh)].astype(f32)
            o_rq = o[rq * GRID_W:(rq + 1) * GRID_W] * inv_l[rq]
            o_ref[0, rows_q, head_cols(hh)] = (o_rq * _silu(g)).astype(o_ref.dtype)


def _attention(z3, vecs, heads=4):
    bsz, t, _ = z3.shape
    rows = t // GRID_W
    assert rows % Q_ROWS == 0 and rows >= K_ROWS and N_HEADS % heads == 0
    groups = N_HEADS // heads
    blk = (1, t, heads * HEAD_DIM)
    return pl.pallas_call(
        functools.partial(_attn_kernel, rows=rows, heads=heads),
        grid=(bsz, groups),
        in_specs=[
            pl.BlockSpec(blk, lambda b, h: (b, 0, h)),
            pl.BlockSpec(blk, lambda b, h: (b, 0, groups + h)),
            pl.BlockSpec(blk, lambda b, h: (b, 0, 2 * groups + h)),
            pl.BlockSpec(blk, lambda b, h: (b, 0, 3 * groups + h)),
            pl.BlockSpec((heads, 2 * WIN_H, PAIR), lambda b, h: (h, 0, 0)),
        ],
        out_specs=pl.BlockSpec(blk, lambda b, h: (b, 0, h)),
        out_shape=jax.ShapeDtypeStruct((bsz, t, D_A), bf16),
        scratch_shapes=[pltpu.VMEM((heads, 2 * WIN_H, GRID_W, PAIR), f32)],
        compiler_params=_params("arbitrary", "arbitrary"),
        name="nbr_attention",
    )(z3, z3, z3, z3, vecs)


def _attention_bias_vectors(rpb):
    half = WIN_W - 1
    pad = jnp.full((N_HEADS, 1, 2 * WIN_W - 1), MASKED, f32)
    ext = jnp.concatenate([pad, rpb.astype(f32), pad], axis=1)
    lower, upper = ext[:, :-1], ext[:, 1:]
    gap = jnp.full((N_HEADS, 2 * WIN_H, GRID_W - 2 * half - 1), MASKED, f32)
    return jnp.concatenate([lower[..., half:], gap, upper, gap, lower[..., :half]], axis=-1)


def _conv_kernel(a_ref, b_ref, g_ref, cw_ref, cb_ref, lng_ref, lnb_ref, pw_ref, pb_ref, o_ref,
                 h_scr, c_scr, *, t, tt, tm):
    zeros = jnp.zeros((CONV_PAD, D_C), f32)
    h_scr[pl.ds(0, CONV_PAD), :] = zeros
    h_scr[pl.ds(CONV_PAD + t, CONV_PAD), :] = zeros
    h_scr[pl.ds(CONV_PAD, t), :] = a_ref[0].astype(f32) * _sigmoid(b_ref[0].astype(f32))

    shift = CONV_PAD - CONV_W // 2
    n_al = (CONV_W + shift + SUBLANES - 1) // SUBLANES
    win_rows = tt + n_al * SUBLANES

    def conv_tile(ti, carry):
        t0 = pl.multiple_of(ti * tt, tt)
        for cb in range(D_C // LANES):
            cols = slice(cb * LANES, (cb + 1) * LANES)
            win = h_scr[pl.ds(t0, win_rows), cols]
            acc = None
            for b in range(SUBLANES):
                part = None
                for a in range(n_al):
                    k = SUBLANES * a + b - shift
                    if 0 <= k < CONV_W:
                        term = cw_ref[k:k + 1, cols] * win[SUBLANES * a:SUBLANES * a + tt + SUBLANES]
                        part = term if part is None else part + term
                part = part[b:b + tt]
                acc = part if acc is None else acc + part
            c_scr[pl.ds(t0, tt), cols] = acc
        return carry

    lax.fori_loop(0, t // tt, conv_tile, 0)

    def post_tile(ti, carry):
        t0 = pl.multiple_of(ti * tm, tm)
        c = c_scr[pl.ds(t0, tm), :] + cb_ref[...]
        c = _silu(_layer_norm(c, lng_ref[...], lnb_ref[...])).astype(bf16)
        y = jnp.dot(c, pw_ref[...], preferred_element_type=f32) + pb_ref[...]
        g = g_ref[0, pl.ds(t0, tm), :].astype(f32)
        o_ref[0, pl.ds(t0, tm), :] = (y * _silu(g)).astype(o_ref.dtype)
        return carry

    lax.fori_loop(0, t // tm, post_tile, 0)


def _conformer_conv(z3, conv_w, conv_b, ln_g, ln_b, pw_w, pw_b, tt=64, tm=512):
    bsz, t, _ = z3.shape
    assert t % tt == 0 and t % tm == 0
    c0 = (4 * D_A + 3 * D_B) // D_C
    blk = (1, t, D_C)
    const2 = lambda b: (0, 0)
    return pl.pallas_call(
        functools.partial(_conv_kernel, t=t, tt=tt, tm=tm),
        grid=(bsz,),
        in_specs=[
            pl.BlockSpec(blk, lambda b: (b, 0, c0)),
            pl.BlockSpec(blk, lambda b: (b, 0, c0 + 1)),
            pl.BlockSpec(blk, lambda b: (b, 0, c0 + 2)),
            pl.BlockSpec((CONV_W, D_C), const2),
            pl.BlockSpec((1, D_C), const2),
            pl.BlockSpec((1, D_C), const2),
            pl.BlockSpec((1, D_C), const2),
            pl.BlockSpec((D_C, D_C), const2),
            pl.BlockSpec((1, D_C), const2),
        ],
        out_specs=pl.BlockSpec(blk, lambda b: (b, 0, 0)),
        out_shape=jax.ShapeDtypeStruct((bsz, t, D_C), bf16),
        scratch_shapes=[pltpu.VMEM((t + 2 * CONV_PAD, D_C), f32), pltpu.VMEM((t, D_C), f32)],
        compiler_params=_params("arbitrary"),
        name="conformer_conv",
    )(z3, z3, z3, conv_w, conv_b, ln_g, ln_b, pw_w, pw_b)


def _spatial_gating(u_ref, v_ref, g_ref, lng_ref, lnb_ref, ws_ref, bs_ref, o_ref, col0, r0, nr):
    gd = D_B // B_GROUPS
    for n in range(r0 // CHUNK, (r0 + nr) // CHUNK):
        rows = slice(n * CHUNK, (n + 1) * CHUNK)
        v = _layer_norm(_gelu(v_ref[rows, :].astype(f32)), lng_ref[...], lnb_ref[...]).astype(bf16)
        for grp in range(B_GROUPS):
            cols = slice(grp * gd, (grp + 1) * gd)
            s = jnp.dot(ws_ref[grp], v[:, cols], preferred_element_type=f32) + bs_ref[:, cols]
            u = _gelu(u_ref[rows, cols].astype(f32))
            g = g_ref[rows, cols].astype(f32)
            o_ref[rows, col0 + grp * gd:col0 + (grp + 1) * gd] = (u * s * _silu(g)).astype(o_ref.dtype)


def _out_sgu_kernel(ya_ref, yc_ref, w_ref, pg_ref, x_ref, gn_ref,
                    u_ref, v_ref, g_ref, lng_ref, lnb_ref, ws_ref, bs_ref, o_ref, *rest, tm, n_sub, emit_h):
    cat_scr = rest[-1]
    sub = tm // n_sub

    cat_scr[:, 0:D_A] = ya_ref[...]
    _spatial_gating(u_ref, v_ref, g_ref, lng_ref, lnb_ref, ws_ref, bs_ref, cat_scr, D_A, 0, tm)
    cat_scr[:, D_A + D_B:D_MIX] = yc_ref[...]

    def project(i):
        rows = slice(i * sub, (i + 1) * sub)
        return jnp.dot(cat_scr[rows, :], w_ref[...], preferred_element_type=f32)

    def finish(i, y):
        rows = slice(i * sub, (i + 1) * sub)
        x_new = x_ref[rows, :] + _rms_norm(y, pg_ref[...])
        o_ref[rows, :] = x_new
        if emit_h:
            h_ref = rest[0]
            h_ref[rows, :] = _rms_norm(x_new, gn_ref[...]).astype(h_ref.dtype)

    y_prev = project(0)
    for i in range(1, n_sub):
        y_cur = project(i)
        finish(i - 1, y_prev)
        y_prev = y_cur
    finish(n_sub - 1, y_prev)


def _out_proj_sgu(ya, yc, z2, w, post_g, x2, next_g, ln_g, ln_b, w_s, b_map, *, emit_h, tm=512, n_sub=2):
    n_tok = x2.shape[0]
    assert tm % (n_sub * CHUNK) == 0
    c0 = 4 * D_A // D_B
    const2 = lambda i: (0, 0)
    tile = lambda width, col=0: pl.BlockSpec((tm, width), lambda i: (i, col))
    out_shape = [jax.ShapeDtypeStruct((n_tok, D_MODEL), f32)]
    out_specs = [tile(D_MODEL)]
    if emit_h:
        out_shape.append(jax.ShapeDtypeStruct((n_tok, D_MODEL), bf16))
        out_specs.append(tile(D_MODEL))
    return pl.pallas_call(
        functools.partial(_out_sgu_kernel, tm=tm, n_sub=n_sub, emit_h=emit_h),
        grid=(n_tok // tm,),
        in_specs=[
            tile(D_A), tile(D_C),
            pl.BlockSpec((D_MIX, D_MODEL), const2),
            pl.BlockSpec((1, D_MODEL), const2),
            tile(D_MODEL),
            pl.BlockSpec((1, D_MODEL), const2),
            tile(D_B, c0), tile(D_B, c0 + 1), tile(D_B, c0 + 2),
            pl.BlockSpec((1, D_B), const2),
            pl.BlockSpec((1, D_B), const2),
            pl.BlockSpec((B_GROUPS, CHUNK, CHUNK), lambda i: (0, 0, 0)),
            pl.BlockSpec((CHUNK, D_B), const2),
        ],
        out_specs=out_specs,
        out_shape=out_shape,
        scratch_shapes=[pltpu.VMEM((tm, D_MIX), bf16)],
        compiler_params=_params("arbitrary"),
        name="out_proj_sgu",
    )(ya, yc, w, post_g, x2, next_g, z2, z2, z2, ln_g, ln_b, w_s, b_map)


def kernel(x, pre_norm_g, w_in, attn_rpb, sgu_ln_g, sgu_ln_b, sgu_w, sgu_b, conv_w, conv_b,
           conv_ln_g, conv_ln_b, conv_pw_w, conv_pw_b, w_out, post_norm_g):
    bsz, t, d = x.shape
    depth = w_in.shape[0]
    assert d == D_MODEL and t % GRID_W == 0 and t % CHUNK == 0
    n_tok = bsz * t
    x2 = x.reshape(n_tok, d)
    row = lambda a: a.reshape(1, -1).astype(f32)
    h = _pre_norm(x2, row(pre_norm_g[0]))
    for l in range(depth):
        z2 = _in_proj(h, w_in, l)
        z3 = z2.reshape(bsz, t, D_IN)
        y_a = _attention(z3, _attention_bias_vectors(attn_rpb[l]))
        b_map = jnp.repeat(sgu_b[l].astype(f32).T, D_B // B_GROUPS, axis=1)
        y_c = _conformer_conv(z3, conv_w[l].reshape(CONV_W, D_C).astype(f32), row(conv_b[l]),
                              row(conv_ln_g[l]), row(conv_ln_b[l]), conv_pw_w[l].astype(bf16),
                              row(conv_pw_b[l]))
        last = l == depth - 1
        next_g = post_norm_g[l] if last else pre_norm_g[l + 1]
        outs = _out_proj_sgu(
            y_a.reshape(n_tok, D_A), y_c.reshape(n_tok, D_C), z2, w_out[l].astype(bf16),
            row(post_norm_g[l]), x2, row(next_g), row(sgu_ln_g[l]), row(sgu_ln_b[l]),
            sgu_w[l].astype(bf16), b_map, emit_h=not last)
        x2 = outs[0]
        if not last:
            h = outs[1]
    return x2.reshape(bsz, t, d)
```

```python
import functools
import math

import jax
import jax.numpy as jnp
from jax import lax
from jax.experimental import pallas as pl
from jax.experimental.pallas import tpu as pltpu

D_MODEL = 2048
GRID_W = 64
WIN_H = 8
WIN_W = 16
HEAD_DIM = 128
D_A = 1024
N_HEADS = 8
CHUNK = 128
D_B = 512
B_GROUPS = 4
D_C = 512
CONV_W = 31
D_MIX = D_A + D_B + D_C
D_IN = 4 * D_A + 3 * D_B + 3 * D_C
EPS = 1e-6
MASKED = -1e30

LANES = 128
SUBLANES = 8
VMEM_LIMIT = 56 * 1024 * 1024

Q_ROWS = 4
K_ROWS = Q_ROWS + WIN_H
PAIR = 2 * GRID_W
assert PAIR == LANES

CONV_PAD = 16

f32 = jnp.float32
bf16 = jnp.bfloat16


def _sigmoid(x):
    return 1.0 / (1.0 + jnp.exp(-x))


def _silu(x):
    return x * _sigmoid(x)


def _gelu(x):
    return 0.5 * x * (1.0 + lax.erf(x * math.sqrt(0.5)))


def _layer_norm(x, g, b):
    mu = jnp.mean(x, axis=-1, keepdims=True)
    xc = x - mu
    var = jnp.mean(xc * xc, axis=-1, keepdims=True)
    return xc * lax.rsqrt(var + EPS) * g + b


def _rms_norm(x, g):
    ms = jnp.mean(x * x, axis=-1, keepdims=True)
    return x * lax.rsqrt(ms + EPS) * g


def _params(*semantics):
    return pltpu.CompilerParams(dimension_semantics=semantics, vmem_limit_bytes=VMEM_LIMIT)


def _pre_norm_kernel(x_ref, g_ref, h_ref):
    h_ref[...] = _rms_norm(x_ref[...], g_ref[...]).astype(h_ref.dtype)


def _pre_norm(x2, g, tm=1024):
    n_tok = x2.shape[0]
    return pl.pallas_call(
        _pre_norm_kernel,
        grid=(n_tok // tm,),
        in_specs=[pl.BlockSpec((tm, D_MODEL), lambda i: (i, 0)),
                  pl.BlockSpec((1, D_MODEL), lambda i: (0, 0))],
        out_specs=pl.BlockSpec((tm, D_MODEL), lambda i: (i, 0)),
        out_shape=jax.ShapeDtypeStruct((n_tok, D_MODEL), bf16),
        compiler_params=_params("arbitrary"),
        name="pre_norm",
    )(x2, g)


def _in_proj_kernel(h_ref, w_ref, z_ref, wb_scr):
    @pl.when(pl.program_id(1) == 0)
    def _():
        wb_scr[...] = w_ref[...].astype(bf16)

    z_ref[...] = jnp.dot(h_ref[...], wb_scr[...], preferred_element_type=f32).astype(z_ref.dtype)


def _in_proj(h, w_in, layer, tm=2048, tn=1024):
    n_tok = h.shape[0]
    return pl.pallas_call(
        _in_proj_kernel,
        grid=(D_IN // tn, n_tok // tm),
        in_specs=[
            pl.BlockSpec((tm, D_MODEL), lambda j, i: (i, 0)),
            pl.BlockSpec((None, D_MODEL, tn), lambda j, i: (layer, 0, j)),
        ],
        out_specs=pl.BlockSpec((tm, tn), lambda j, i: (i, j)),
        out_shape=jax.ShapeDtypeStruct((n_tok, D_IN), bf16),
        scratch_shapes=[pltpu.VMEM((D_MODEL, tn), bf16)],
        compiler_params=_params("arbitrary", "arbitrary"),
        name="in_proj",
    )(h, w_in)


def _attn_kernel(q_ref, k_ref, v_ref, g_ref, vec_ref, o_ref, bias_ref, *, rows, heads):
    scale = HEAD_DIM ** -0.5
    nq = Q_ROWS * GRID_W
    nk = K_ROWS * GRID_W
    w_idx = lax.broadcasted_iota(jnp.int32, (GRID_W, PAIR), 0)
    lane = lax.broadcasted_iota(jnp.int32, (GRID_W, PAIR), 1)
    left_half = lane < GRID_W
    kc = jnp.where(left_half, lane, lane - GRID_W)
    col_start = jnp.clip(w_idx - WIN_W // 2, 0, GRID_W - WIN_W)
    in_win = (kc >= col_start) & (kc < col_start + WIN_W)
    for hh in range(heads):
        for e in range(2 * WIN_H):
            vec = jnp.broadcast_to(vec_ref[hh, e:e + 1, :], (GRID_W, PAIR))
            bias_ref[hh, e] = jnp.where(in_win, pltpu.roll(vec, 0, 1, stride=1, stride_axis=0), MASKED)

    n_blk = rows // Q_ROWS
    first_key_row = lambda blk: min(max(blk * Q_ROWS - WIN_H // 2, 0), rows - K_ROWS)

    head_cols = lambda hh: slice(hh * HEAD_DIM, (hh + 1) * HEAD_DIM)
    items = [(hh, blk) for hh in range(heads) for blk in range(n_blk)]

    def scores(item):
        hh, blk = item
        q0, k0 = blk * nq, first_key_row(blk) * GRID_W
        q = (q_ref[0, q0:q0 + nq, head_cols(hh)].astype(f32) * scale).astype(bf16)
        kw = k_ref[0, k0:k0 + nk, head_cols(hh)]
        return lax.dot_general(q, kw, (((1,), (1,)), ((), ())), preferred_element_type=f32)

    s_next = scores(items[0])
    for pos, (hh, blk) in enumerate(items):
        ks = first_key_row(blk)
        q0, k0 = blk * nq, ks * GRID_W
        vw = v_ref[0, k0:k0 + nk, head_cols(hh)]
        s = s_next
        if pos + 1 < len(items):
            s_next = scores(items[pos + 1])

        p_rows, inv_l = [], []
        for rq in range(Q_ROWS):
            r = blk * Q_ROWS + rq
            rs = min(max(r - WIN_H // 2, 0), rows - WIN_H)
            pieces = {}
            for ip in range(K_ROWS // 2):
                kr = ks + 2 * ip
                ok0, ok1 = rs <= kr < rs + WIN_H, rs <= kr + 1 < rs + WIN_H
                if not (ok0 or ok1):
                    continue
                dr0 = kr - r + (WIN_H - 1)
                sub = s[rq * GRID_W:(rq + 1) * GRID_W, ip * PAIR:(ip + 1) * PAIR]
                if ok0 and ok1:
                    bias = bias_ref[hh, dr0 + 1]
                elif ok0:
                    bias = jnp.where(left_half, bias_ref[hh, dr0 + 1], MASKED)
                else:
                    bias = jnp.where(left_half, MASKED, bias_ref[hh, dr0 + 1])
                pieces[ip] = sub + bias
            m = jnp.max(functools.reduce(jnp.maximum, pieces.values()), axis=-1, keepdims=True)
            pieces = {ip: jnp.exp(sv - m) for ip, sv in pieces.items()}
            l = jnp.sum(functools.reduce(jnp.add, pieces.values()), axis=-1, keepdims=True)
            inv_l.append(1.0 / l)
            zero = jnp.zeros((GRID_W, PAIR), bf16)
            p_rows.append(jnp.concatenate(
                [pieces[ip].astype(bf16) if ip in pieces else zero for ip in range(K_ROWS // 2)], axis=1))
        p = jnp.concatenate(p_rows, axis=0)
        o = jnp.dot(p, vw, preferred_element_type=f32)
        for rq in range(Q_ROWS):
            rows_q = slice(q0 + rq * GRID_W, q0 + (rq + 1) * GRID_W)
            g = g_ref[0, rows_q, head_cols(hh)].astype(f32)
            o_rq = o[rq * GRID_W:(rq + 1) * GRID_W] * inv_l[rq]
            o_ref[0, rows_q, head_cols(hh)] = (o_rq * _silu(g)).astype(o_ref.dtype)


def _attention(z3, vecs, heads=4):
    bsz, t, _ = z3.shape
    rows = t // GRID_W
    assert rows % Q_ROWS == 0 and rows >= K_ROWS and N_HEADS % heads == 0
    groups = N_HEADS // heads
    blk = (1, t, heads * HEAD_DIM)
    return pl.pallas_call(
        functools.partial(_attn_kernel, rows=rows, heads=heads),
        grid=(bsz, groups),
        in_specs=[
            pl.BlockSpec(blk, lambda b, h: (b, 0, h)),
            pl.BlockSpec(blk, lambda b, h: (b, 0, groups + h)),
            pl.BlockSpec(blk, lambda b, h: (b, 0, 2 * groups + h)),
            pl.BlockSpec(blk, lambda b, h: (b, 0, 3 * groups + h)),
            pl.BlockSpec((heads, 2 * WIN_H, PAIR), lambda b, h: (h, 0, 0)),
        ],
        out_specs=pl.BlockSpec(blk, lambda b, h: (b, 0, h)),
        out_shape=jax.ShapeDtypeStruct((bsz, t, D_A), bf16),
        scratch_shapes=[pltpu.VMEM((heads, 2 * WIN_H, GRID_W, PAIR), f32)],
        compiler_params=_params("arbitrary", "arbitrary"),
        name="nbr_attention",
    )(z3, z3, z3, z3, vecs)


def _attention_bias_vectors(rpb):
    half = WIN_W - 1
    pad = jnp.full((N_HEADS, 1, 2 * WIN_W - 1), MASKED, f32)
    ext = jnp.concatenate([pad, rpb.astype(f32), pad], axis=1)
    lower, upper = ext[:, :-1], ext[:, 1:]
    gap = jnp.full((N_HEADS, 2 * WIN_H, GRID_W - 2 * half - 1), MASKED, f32)
    return jnp.concatenate([lower[..., half:], gap, upper, gap, lower[..., :half]], axis=-1)


def _conv_kernel(a_ref, b_ref, g_ref, cw_ref, cb_ref, lng_ref, lnb_ref, pw_ref, pb_ref, o_ref,
                 h_scr, c_scr, *, t, tt, tm):
    zeros = jnp.zeros((CONV_PAD, D_C), f32)
    h_scr[pl.ds(0, CONV_PAD), :] = zeros
    h_scr[pl.ds(CONV_PAD + t, CONV_PAD), :] = zeros
    h_scr[pl.ds(CONV_PAD, t), :] = a_ref[0].astype(f32) * _sigmoid(b_ref[0].astype(f32))

    shift = CONV_PAD - CONV_W // 2
    n_al = (CONV_W + shift + SUBLANES - 1) // SUBLANES
    win_rows = tt + n_al * SUBLANES

    def conv_tile(ti, carry):
        t0 = pl.multiple_of(ti * tt, tt)
        for cb in range(D_C // LANES):
            cols = slice(cb * LANES, (cb + 1) * LANES)
            win = h_scr[pl.ds(t0, win_rows), cols]
            acc = None
            for b in range(SUBLANES):
                part = None
                for a in range(n_al):
                    k = SUBLANES * a + b - shift
                    if 0 <= k < CONV_W:
                        term = cw_ref[k:k + 1, cols] * win[SUBLANES * a:SUBLANES * a + tt + SUBLANES]
                        part = term if part is None else part + term
                part = part[b:b + tt]
                acc = part if acc is None else acc + part
            c_scr[pl.ds(t0, tt), cols] = acc
        return carry

    lax.fori_loop(0, t // tt, conv_tile, 0)

    def post_tile(ti, carry):
        t0 = pl.multiple_of(ti * tm, tm)
        c = c_scr[pl.ds(t0, tm), :] + cb_ref[...]
        c = _silu(_layer_norm(c, lng_ref[...], lnb_ref[...])).astype(bf16)
        y = jnp.dot(c, pw_ref[...], preferred_element_type=f32) + pb_ref[...]
        g = g_ref[0, pl.ds(t0, tm), :].astype(f32)
        o_ref[0, pl.ds(t0, tm), :] = (y * _silu(g)).astype(o_ref.dtype)
        return carry

    lax.fori_loop(0, t // tm, post_tile, 0)


def _conformer_conv(z3, conv_w, conv_b, ln_g, ln_b, pw_w, pw_b, tt=64, tm=512):
    bsz, t, _ = z3.shape
    assert t % tt == 0 and t % tm == 0
    c0 = (4 * D_A + 3 * D_B) // D_C
    blk = (1, t, D_C)
    const2 = lambda b: (0, 0)
    return pl.pallas_call(
        functools.partial(_conv_kernel, t=t, tt=tt, tm=tm),
        grid=(bsz,),
        in_specs=[
            pl.BlockSpec(blk, lambda b: (b, 0, c0)),
            pl.BlockSpec(blk, lambda b: (b, 0, c0 + 1)),
            pl.BlockSpec(blk, lambda b: (b, 0, c0 + 2)),
            pl.BlockSpec((CONV_W, D_C), const2),
            pl.BlockSpec((1, D_C), const2),
            pl.BlockSpec((1, D_C), const2),
            pl.BlockSpec((1, D_C), const2),
            pl.BlockSpec((D_C, D_C), const2),
            pl.BlockSpec((1, D_C), const2),
        ],
        out_specs=pl.BlockSpec(blk, lambda b: (b, 0, 0)),
        out_shape=jax.ShapeDtypeStruct((bsz, t, D_C), bf16),
        scratch_shapes=[pltpu.VMEM((t + 2 * CONV_PAD, D_C), f32), pltpu.VMEM((t, D_C), f32)],
        compiler_params=_params("arbitrary"),
        name="conformer_conv",
    )(z3, z3, z3, conv_w, conv_b, ln_g, ln_b, pw_w, pw_b)


def _spatial_gating(u_ref, v_ref, g_ref, lng_ref, lnb_ref, ws_ref, bs_ref, o_ref, col0, r0, nr):
    gd = D_B // B_GROUPS
    for n in range(r0 // CHUNK, (r0 + nr) // CHUNK):
        rows = slice(n * CHUNK, (n + 1) * CHUNK)
        v = _layer_norm(_gelu(v_ref[rows, :].astype(f32)), lng_ref[...], lnb_ref[...]).astype(bf16)
        for grp in range(B_GROUPS):
            cols = slice(grp * gd, (grp + 1) * gd)
            s = jnp.dot(ws_ref[grp], v[:, cols], preferred_element_type=f32) + bs_ref[:, cols]
            u = _gelu(u_ref[rows, cols].astype(f32))
            g = g_ref[rows, cols].astype(f32)
            o_ref[rows, col0 + grp * gd:col0 + (grp + 1) * gd] = (u * s * _silu(g)).astype(o_ref.dtype)


def _out_sgu_kernel(ya_ref, yc_ref, w_ref, pg_ref, x_ref, gn_ref,
                    u_ref, v_ref, g_ref, lng_ref, lnb_ref, ws_ref, bs_ref, o_ref, *rest, tm, n_sub, emit_h):
    cat_scr = rest[-1]
    sub = tm // n_sub

    def project(i):
        rows = slice(i * sub, (i + 1) * sub)
        cat_scr[rows, 0:D_A] = ya_ref[rows, :]
        _spatial_gating(u_ref, v_ref, g_ref, lng_ref, lnb_ref, ws_ref, bs_ref, cat_scr, D_A, i * sub, sub)
        cat_scr[rows, D_A + D_B:D_MIX] = yc_ref[rows, :]
        return jnp.dot(cat_scr[rows, :], w_ref[...], preferred_element_type=f32)

    def finish(i, y):
        rows = slice(i * sub, (i + 1) * sub)
        x_new = x_ref[rows, :] + _rms_norm(y, pg_ref[...])
        o_ref[rows, :] = x_new
        if emit_h:
            h_ref = rest[0]
            h_ref[rows, :] = _rms_norm(x_new, gn_ref[...]).astype(h_ref.dtype)

    y_prev = project(0)
    for i in range(1, n_sub):
        y_cur = project(i)
        finish(i - 1, y_prev)
        y_prev = y_cur
    finish(n_sub - 1, y_prev)


def _out_proj_sgu(ya, yc, z2, w_all, layer, post_g, x2, next_g, ln_g, ln_b, w_s, b_map, *, emit_h,
                  tm=512, n_sub=2):
    n_tok = x2.shape[0]
    assert tm % (n_sub * CHUNK) == 0
    c0 = 4 * D_A // D_B
    const2 = lambda i: (0, 0)
    tile = lambda width, col=0: pl.BlockSpec((tm, width), lambda i: (i, col))
    out_shape = [jax.ShapeDtypeStruct((n_tok, D_MODEL), f32)]
    out_specs = [tile(D_MODEL)]
    if emit_h:
        out_shape.append(jax.ShapeDtypeStruct((n_tok, D_MODEL), bf16))
        out_specs.append(tile(D_MODEL))
    return pl.pallas_call(
        functools.partial(_out_sgu_kernel, tm=tm, n_sub=n_sub, emit_h=emit_h),
        grid=(n_tok // tm,),
        in_specs=[
            tile(D_A), tile(D_C),
            pl.BlockSpec((None, D_MIX, D_MODEL), lambda i: (layer, 0, 0)),
            pl.BlockSpec((1, D_MODEL), const2),
            tile(D_MODEL),
            pl.BlockSpec((1, D_MODEL), const2),
            tile(D_B, c0), tile(D_B, c0 + 1), tile(D_B, c0 + 2),
            pl.BlockSpec((1, D_B), const2),
            pl.BlockSpec((1, D_B), const2),
            pl.BlockSpec((B_GROUPS, CHUNK, CHUNK), lambda i: (0, 0, 0)),
            pl.BlockSpec((CHUNK, D_B), const2),
        ],
        out_specs=out_specs,
        out_shape=out_shape,
        scratch_shapes=[pltpu.VMEM((tm, D_MIX), bf16)],
        compiler_params=_params("arbitrary"),
        name="out_proj_sgu",
    )(ya, yc, w_all, post_g, x2, next_g, z2, z2, z2, ln_g, ln_b, w_s, b_map)


def kernel(x, pre_norm_g, w_in, attn_rpb, sgu_ln_g, sgu_ln_b, sgu_w, sgu_b, conv_w, conv_b,
           conv_ln_g, conv_ln_b, conv_pw_w, conv_pw_b, w_out, post_norm_g):
    bsz, t, d = x.shape
    depth = w_in.shape[0]
    assert d == D_MODEL and t % GRID_W == 0 and t % CHUNK == 0
    n_tok = bsz * t
    x2 = x.reshape(n_tok, d)
    row = lambda a: a.reshape(1, -1).astype(f32)
    h = _pre_norm(x2, row(pre_norm_g[0]))
    w_out_bf = w_out.astype(bf16)
    for l in range(depth):
        z2 = _in_proj(h, w_in, l)
        z3 = z2.reshape(bsz, t, D_IN)
        y_a = _attention(z3, _attention_bias_vectors(attn_rpb[l]))
        b_map = jnp.repeat(sgu_b[l].astype(f32).T, D_B // B_GROUPS, axis=1)
        y_c = _conformer_conv(z3, conv_w[l].reshape(CONV_W, D_C).astype(f32), row(conv_b[l]),
                              row(conv_ln_g[l]), row(conv_ln_b[l]), conv_pw_w[l].astype(bf16),
                              row(conv_pw_b[l]))
        last = l == depth - 1
        next_g = post_norm_g[l] if last else pre_norm_g[l + 1]
        outs = _out_proj_sgu(
            y_a.reshape(n_tok, D_A), y_c.reshape(n_tok, D_C), z2, w_out_bf, l,
            row(post_norm_g[l]), x2, row(next_g), row(sgu_ln_g[l]), row(sgu_ln_b[l]),
            sgu_w[l].astype(bf16), b_map, emit_h=not last)
        x2 = outs[0]
        if not last:
            h = outs[1]
    return x2.reshape(bsz, t, d)
```

```python
import functools
import math

import jax
import jax.numpy as jnp
from jax import lax
from jax.experimental import pallas as pl
from jax.experimental.pallas import tpu as pltpu

D_MODEL = 2048
GRID_W = 64
WIN_H = 8
WIN_W = 16
HEAD_DIM = 128
D_A = 1024
N_HEADS = 8
CHUNK = 128
D_B = 512
B_GROUPS = 4
D_C = 512
CONV_W = 31
D_MIX = D_A + D_B + D_C
D_IN = 4 * D_A + 3 * D_B + 3 * D_C
EPS = 1e-6
MASKED = -1e30

LANES = 128
SUBLANES = 8
VMEM_LIMIT = 56 * 1024 * 1024

Q_ROWS = 4
K_ROWS = Q_ROWS + WIN_H
PAIR = 2 * GRID_W
assert PAIR == LANES

CONV_PAD = 16

f32 = jnp.float32
bf16 = jnp.bfloat16


def _sigmoid(x):
    return 1.0 / (1.0 + jnp.exp(-x))


def _silu(x):
    return x * _sigmoid(x)


def _gelu(x):
    return 0.5 * x * (1.0 + lax.erf(x * math.sqrt(0.5)))


def _layer_norm(x, g, b):
    mu = jnp.mean(x, axis=-1, keepdims=True)
    xc = x - mu
    var = jnp.mean(xc * xc, axis=-1, keepdims=True)
    return xc * lax.rsqrt(var + EPS) * g + b


def _rms_norm(x, g):
    ms = jnp.mean(x * x, axis=-1, keepdims=True)
    return x * lax.rsqrt(ms + EPS) * g


def _params(*semantics):
    return pltpu.CompilerParams(dimension_semantics=semantics, vmem_limit_bytes=VMEM_LIMIT)


def _pre_norm_kernel(x_ref, g_ref, h_ref):
    h_ref[...] = _rms_norm(x_ref[...], g_ref[...]).astype(h_ref.dtype)


def _pre_norm(x2, g, tm=1024):
    n_tok = x2.shape[0]
    return pl.pallas_call(
        _pre_norm_kernel,
        grid=(n_tok // tm,),
        in_specs=[pl.BlockSpec((tm, D_MODEL), lambda i: (i, 0)),
                  pl.BlockSpec((1, D_MODEL), lambda i: (0, 0))],
        out_specs=pl.BlockSpec((tm, D_MODEL), lambda i: (i, 0)),
        out_shape=jax.ShapeDtypeStruct((n_tok, D_MODEL), bf16),
        compiler_params=_params("arbitrary"),
        name="pre_norm",
    )(x2, g)


def _in_proj_kernel(h_ref, w_ref, z_ref, wb_scr):
    @pl.when(pl.program_id(1) == 0)
    def _():
        wb_scr[...] = w_ref[...].astype(bf16)

    z_ref[...] = jnp.dot(h_ref[...], wb_scr[...], preferred_element_type=f32).astype(z_ref.dtype)


def _in_proj(h, w_in, layer, tm=2048, tn=1024):
    n_tok = h.shape[0]
    return pl.pallas_call(
        _in_proj_kernel,
        grid=(D_IN // tn, n_tok // tm),
        in_specs=[
            pl.BlockSpec((tm, D_MODEL), lambda j, i: (i, 0)),
            pl.BlockSpec((None, D_MODEL, tn), lambda j, i: (layer, 0, j)),
        ],
        out_specs=pl.BlockSpec((tm, tn), lambda j, i: (i, j)),
        out_shape=jax.ShapeDtypeStruct((n_tok, D_IN), bf16),
        scratch_shapes=[pltpu.VMEM((D_MODEL, tn), bf16)],
        compiler_params=_params("arbitrary", "arbitrary"),
        name="in_proj",
    )(h, w_in)


def _attn_kernel(q_ref, k_ref, v_ref, g_ref, vec_ref, o_ref, bias_ref, *, rows, heads):
    scale = HEAD_DIM ** -0.5
    nq = Q_ROWS * GRID_W
    nk = K_ROWS * GRID_W
    w_idx = lax.broadcasted_iota(jnp.int32, (GRID_W, PAIR), 0)
    lane = lax.broadcasted_iota(jnp.int32, (GRID_W, PAIR), 1)
    left_half = lane < GRID_W
    kc = jnp.where(left_half, lane, lane - GRID_W)
    col_start = jnp.clip(w_idx - WIN_W // 2, 0, GRID_W - WIN_W)
    in_win = (kc >= col_start) & (kc < col_start + WIN_W)
    for hh in range(heads):
        for e in range(2 * WIN_H):
            vec = jnp.broadcast_to(vec_ref[hh, e:e + 1, :], (GRID_W, PAIR))
            bias_ref[hh, e] = jnp.where(in_win, pltpu.roll(vec, 0, 1, stride=1, stride_axis=0), MASKED)

    n_blk = rows // Q_ROWS
    first_key_row = lambda blk: min(max(blk * Q_ROWS - WIN_H // 2, 0), rows - K_ROWS)

    head_cols = lambda hh: slice(hh * HEAD_DIM, (hh + 1) * HEAD_DIM)
    items = [(hh, blk) for hh in range(heads) for blk in range(n_blk)]

    def scores(item):
        hh, blk = item
        q0, k0 = blk * nq, first_key_row(blk) * GRID_W
        q = (q_ref[0, q0:q0 + nq, head_cols(hh)].astype(f32) * scale).astype(bf16)
        kw = k_ref[0, k0:k0 + nk, head_cols(hh)]
        return lax.dot_general(q, kw, (((1,), (1,)), ((), ())), preferred_element_type=f32)

    s_next = scores(items[0])
    for pos, (hh, blk) in enumerate(items):
        ks = first_key_row(blk)
        q0, k0 = blk * nq, ks * GRID_W
        vw = v_ref[0, k0:k0 + nk, head_cols(hh)]
        s = s_next
        if pos + 1 < len(items):
            s_next = scores(items[pos + 1])

        p_rows, inv_l = [], []
        for rq in range(Q_ROWS):
            r = blk * Q_ROWS + rq
            rs = min(max(r - WIN_H // 2, 0), rows - WIN_H)
            pieces = {}
            for ip in range(K_ROWS // 2):
                kr = ks + 2 * ip
                ok0, ok1 = rs <= kr < rs + WIN_H, rs <= kr + 1 < rs + WIN_H
                if not (ok0 or ok1):
                    continue
                dr0 = kr - r + (WIN_H - 1)
                sub = s[rq * GRID_W:(rq + 1) * GRID_W, ip * PAIR:(ip + 1) * PAIR]
                if ok0 and ok1:
                    bias = bias_ref[hh, dr0 + 1]
                elif ok0:
                    bias = jnp.where(left_half, bias_ref[hh, dr0 + 1], MASKED)
                else:
                    bias = jnp.where(left_half, MASKED, bias_ref[hh, dr0 + 1])
                pieces[ip] = sub + bias
            m = jnp.max(functools.reduce(jnp.maximum, pieces.values()), axis=-1, keepdims=True)
            pieces = {ip: jnp.exp(sv - m) for ip, sv in pieces.items()}
            l = jnp.sum(functools.reduce(jnp.add, pieces.values()), axis=-1, keepdims=True)
            inv_l.append(1.0 / l)
            zero = jnp.zeros((GRID_W, PAIR), bf16)
            p_rows.append(jnp.concatenate(
                [pieces[ip].astype(bf16) if ip in pieces else zero for ip in range(K_ROWS // 2)], axis=1))
        p = jnp.concatenate(p_rows, axis=0)
        o = jnp.dot(p, vw, preferred_element_type=f32)
        for rq in range(Q_ROWS):
            rows_q = slice(q0 + rq * GRID_W, q0 + (rq + 1) * GRID_W)
            g = g_ref[0, rows_q, head_cols(hh)].astype(f32)
            o_rq = o[rq * GRID_W:(rq + 1) * GRID_W] * inv_l[rq]
            o_ref[0, rows_q, head_cols(hh)] = (o_rq * _silu(g)).astype(o_ref.dtype)


def _attention(z3, vecs, heads=4):
    bsz, t, _ = z3.shape
    rows = t // GRID_W
    assert rows % Q_ROWS == 0 and rows >= K_ROWS and N_HEADS % heads == 0
    groups = N_HEADS // heads
    blk = (1, t, heads * HEAD_DIM)
    return pl.pallas_call(
        functools.partial(_attn_kernel, rows=rows, heads=heads),
        grid=(bsz, groups),
        in_specs=[
            pl.BlockSpec(blk, lambda b, h: (b, 0, h)),
            pl.BlockSpec(blk, lambda b, h: (b, 0, groups + h)),
            pl.BlockSpec(blk, lambda b, h: (b, 0, 2 * groups + h)),
            pl.BlockSpec(blk, lambda b, h: (b, 0, 3 * groups + h)),
            pl.BlockSpec((heads, 2 * WIN_H, PAIR), lambda b, h: (h, 0, 0)),
        ],
        out_specs=pl.BlockSpec(blk, lambda b, h: (b, 0, h)),
        out_shape=jax.ShapeDtypeStruct((bsz, t, D_A), bf16),
        scratch_shapes=[pltpu.VMEM((heads, 2 * WIN_H, GRID_W, PAIR), f32)],
        compiler_params=_params("arbitrary", "arbitrary"),
        name="nbr_attention",
    )(z3, z3, z3, z3, vecs)


def _attention_bias_vectors(rpb):
    half = WIN_W - 1
    pad = jnp.full((N_HEADS, 1, 2 * WIN_W - 1), MASKED, f32)
    ext = jnp.concatenate([pad, rpb.astype(f32), pad], axis=1)
    lower, upper = ext[:, :-1], ext[:, 1:]
    gap = jnp.full((N_HEADS, 2 * WIN_H, GRID_W - 2 * half - 1), MASKED, f32)
    return jnp.concatenate([lower[..., half:], gap, upper, gap, lower[..., :half]], axis=-1)


def _conv_kernel(a_ref, b_ref, g_ref, cw_ref, cb_ref, lng_ref, lnb_ref, pw_ref, pb_ref, o_ref,
                 h_scr, c_scr, *, t, tt, tm):
    zeros = jnp.zeros((CONV_PAD, D_C), f32)
    h_scr[pl.ds(0, CONV_PAD), :] = zeros
    h_scr[pl.ds(CONV_PAD + t, CONV_PAD), :] = zeros
    h_scr[pl.ds(CONV_PAD, t), :] = a_ref[0].astype(f32) * _sigmoid(b_ref[0].astype(f32))

    shift = CONV_PAD - CONV_W // 2
    n_al = (CONV_W + shift + SUBLANES - 1) // SUBLANES
    win_rows = tt + n_al * SUBLANES

    def conv_tile(ti, carry):
        t0 = pl.multiple_of(ti * tt, tt)
        for cb in range(D_C // LANES):
            cols = slice(cb * LANES, (cb + 1) * LANES)
            win = h_scr[pl.ds(t0, win_rows), cols]
            acc = None
            for b in range(SUBLANES):
                part = None
                for a in range(n_al):
                    k = SUBLANES * a + b - shift
                    if 0 <= k < CONV_W:
                        term = cw_ref[k:k + 1, cols] * win[SUBLANES * a:SUBLANES * a + tt + SUBLANES]
                        part = term if part is None else part + term
                part = part[b:b + tt]
                acc = part if acc is None else acc + part
            c_scr[pl.ds(t0, tt), cols] = acc
        return carry

    lax.fori_loop(0, t // tt, conv_tile, 0)

    def post_tile(ti, carry):
        t0 = pl.multiple_of(ti * tm, tm)
        c = c_scr[pl.ds(t0, tm), :] + cb_ref[...]
        c = _silu(_layer_norm(c, lng_ref[...], lnb_ref[...])).astype(bf16)
        y = jnp.dot(c, pw_ref[...], preferred_element_type=f32) + pb_ref[...]
        g = g_ref[0, pl.ds(t0, tm), :].astype(f32)
        o_ref[0, pl.ds(t0, tm), :] = (y * _silu(g)).astype(o_ref.dtype)
        return carry

    lax.fori_loop(0, t // tm, post_tile, 0)


def _conformer_conv(z3, conv_w, conv_b, ln_g, ln_b, pw_w, pw_b, tt=64, tm=512):
    bsz, t, _ = z3.shape
    assert t % tt == 0 and t % tm == 0
    c0 = (4 * D_A + 3 * D_B) // D_C
    blk = (1, t, D_C)
    const2 = lambda b: (0, 0)
    return pl.pallas_call(
        functools.partial(_conv_kernel, t=t, tt=tt, tm=tm),
        grid=(bsz,),
        in_specs=[
            pl.BlockSpec(blk, lambda b: (b, 0, c0)),
            pl.BlockSpec(blk, lambda b: (b, 0, c0 + 1)),
            pl.BlockSpec(blk, lambda b: (b, 0, c0 + 2)),
            pl.BlockSpec((CONV_W, D_C), const2),
            pl.BlockSpec((1, D_C), const2),
            pl.BlockSpec((1, D_C), const2),
            pl.BlockSpec((1, D_C), const2),
            pl.BlockSpec((D_C, D_C), const2),
            pl.BlockSpec((1, D_C), const2),
        ],
        out_specs=pl.BlockSpec(blk, lambda b: (b, 0, 0)),
        out_shape=jax.ShapeDtypeStruct((bsz, t, D_C), bf16),
        scratch_shapes=[pltpu.VMEM((t + 2 * CONV_PAD, D_C), f32), pltpu.VMEM((t, D_C), f32)],
        compiler_params=_params("arbitrary"),
        name="conformer_conv",
    )(z3, z3, z3, conv_w, conv_b, ln_g, ln_b, pw_w, pw_b)


def _spatial_gating(u_ref, v_ref, g_ref, lng_ref, lnb_ref, ws_ref, bs_ref, o_ref, col0, r0, nr):
    gd = D_B // B_GROUPS
    for n in range(r0 // CHUNK, (r0 + nr) // CHUNK):
        rows = slice(n * CHUNK, (n + 1) * CHUNK)
        v = _layer_norm(_gelu(v_ref[rows, :].astype(f32)), lng_ref[...], lnb_ref[...]).astype(bf16)
        for grp in range(B_GROUPS):
            cols = slice(grp * gd, (grp + 1) * gd)
            s = jnp.dot(ws_ref[grp], v[:, cols], preferred_element_type=f32) + bs_ref[:, cols]
            u = _gelu(u_ref[rows, cols].astype(f32))
            g = g_ref[rows, cols].astype(f32)
            o_ref[rows, col0 + grp * gd:col0 + (grp + 1) * gd] = (u * s * _silu(g)).astype(o_ref.dtype)


def _out_sgu_kernel(ya_ref, yc_ref, w_ref, pg_ref, x_ref, gn_ref,
                    zb_ref, lng_ref, lnb_ref, ws_ref, bs_ref, o_ref, *rest, tm, n_sub, emit_h):
    cat_scr = rest[-1]
    sub = tm // n_sub
    u_ref, v_ref, g_ref = (zb_ref.at[:, k * D_B:(k + 1) * D_B] for k in range(3))

    def project(i):
        rows = slice(i * sub, (i + 1) * sub)
        cat_scr[rows, 0:D_A] = ya_ref[rows, :]
        _spatial_gating(u_ref, v_ref, g_ref, lng_ref, lnb_ref, ws_ref, bs_ref, cat_scr, D_A, i * sub, sub)
        cat_scr[rows, D_A + D_B:D_MIX] = yc_ref[rows, :]
        return jnp.dot(cat_scr[rows, :], w_ref[...], preferred_element_type=f32)

    def finish(i, y):
        rows = slice(i * sub, (i + 1) * sub)
        x_new = x_ref[rows, :] + _rms_norm(y, pg_ref[...])
        o_ref[rows, :] = x_new
        if emit_h:
            h_ref = rest[0]
            h_ref[rows, :] = _rms_norm(x_new, gn_ref[...]).astype(h_ref.dtype)

    y_prev = project(0)
    for i in range(1, n_sub):
        y_cur = project(i)
        finish(i - 1, y_prev)
        y_prev = y_cur
    finish(n_sub - 1, y_prev)


def _out_proj_sgu(ya, yc, z2, w_all, layer, post_g, x2, next_g, ln_g, ln_b, w_s, b_map, *, emit_h,
                  tm=512, n_sub=2):
    n_tok = x2.shape[0]
    assert tm % (n_sub * CHUNK) == 0
    zb_width = 4 * D_B
    assert (4 * D_A) % zb_width == 0 and 4 * D_A + 3 * D_B <= (4 * D_A // zb_width + 1) * zb_width
    const2 = lambda i: (0, 0)
    tile = lambda width, col=0: pl.BlockSpec((tm, width), lambda i: (i, col))
    out_shape = [jax.ShapeDtypeStruct((n_tok, D_MODEL), f32)]
    out_specs = [tile(D_MODEL)]
    if emit_h:
        out_shape.append(jax.ShapeDtypeStruct((n_tok, D_MODEL), bf16))
        out_specs.append(tile(D_MODEL))
    return pl.pallas_call(
        functools.partial(_out_sgu_kernel, tm=tm, n_sub=n_sub, emit_h=emit_h),
        grid=(n_tok // tm,),
        in_specs=[
            tile(D_A), tile(D_C),
            pl.BlockSpec((None, D_MIX, D_MODEL), lambda i: (layer, 0, 0)),
            pl.BlockSpec((1, D_MODEL), const2),
            tile(D_MODEL),
            pl.BlockSpec((1, D_MODEL), const2),
            tile(zb_width, 4 * D_A // zb_width),
            pl.BlockSpec((1, D_B), const2),
            pl.BlockSpec((1, D_B), const2),
            pl.BlockSpec((B_GROUPS, CHUNK, CHUNK), lambda i: (0, 0, 0)),
            pl.BlockSpec((CHUNK, D_B), const2),
        ],
        out_specs=out_specs,
        out_shape=out_shape,
        scratch_shapes=[pltpu.VMEM((tm, D_MIX), bf16)],
        compiler_params=_params("arbitrary"),
        name="out_proj_sgu",
    )(ya, yc, w_all, post_g, x2, next_g, z2, ln_g, ln_b, w_s, b_map)


def kernel(x, pre_norm_g, w_in, attn_rpb, sgu_ln_g, sgu_ln_b, sgu_w, sgu_b, conv_w, conv_b,
           conv_ln_g, conv_ln_b, conv_pw_w, conv_pw_b, w_out, post_norm_g):
    bsz, t, d = x.shape
    depth = w_in.shape[0]
    assert d == D_MODEL and t % GRID_W == 0 and t % CHUNK == 0
    n_tok = bsz * t
    x2 = x.reshape(n_tok, d)
    row = lambda a: a.reshape(1, -1).astype(f32)
    h = _pre_norm(x2, row(pre_norm_g[0]))
    w_out_bf = w_out.astype(bf16)
    for l in range(depth):
        z2 = _in_proj(h, w_in, l)
        z3 = z2.reshape(bsz, t, D_IN)
        y_a = _attention(z3, _attention_bias_vectors(attn_rpb[l]))
        b_map = jnp.repeat(sgu_b[l].astype(f32).T, D_B // B_GROUPS, axis=1)
        y_c = _conformer_conv(z3, conv_w[l].reshape(CONV_W, D_C).astype(f32), row(conv_b[l]),
                              row(conv_ln_g[l]), row(conv_ln_b[l]), conv_pw_w[l].astype(bf16),
                              row(conv_pw_b[l]))
        last = l == depth - 1
        next_g = post_norm_g[l] if last else pre_norm_g[l + 1]
        outs = _out_proj_sgu(
            y_a.reshape(n_tok, D_A), y_c.reshape(n_tok, D_C), z2, w_out_bf, l,
            row(post_norm_g[l]), x2, row(next_g), row(sgu_ln_g[l]), row(sgu_ln_b[l]),
            sgu_w[l].astype(bf16), b_map, emit_h=not last)
        x2 = outs[0]
        if not last:
            h = outs[1]
    return x2.reshape(bsz, t, d)
```

```python
import functools
import math

import jax
import jax.numpy as jnp
from jax import lax
from jax.experimental import pallas as pl
from jax.experimental.pallas import tpu as pltpu

D_MODEL = 2048
GRID_W = 64
WIN_H = 8
WIN_W = 16
HEAD_DIM = 128
D_A = 1024
N_HEADS = 8
CHUNK = 128
D_B = 512
B_GROUPS = 4
D_C = 512
CONV_W = 31
D_MIX = D_A + D_B + D_C
D_IN = 4 * D_A + 3 * D_B + 3 * D_C
EPS = 1e-6
MASKED = -1e30

LANES = 128
SUBLANES = 8
VMEM_LIMIT = 56 * 1024 * 1024

Q_ROWS = 4
K_ROWS = Q_ROWS + WIN_H
PAIR = 2 * GRID_W
assert PAIR == LANES

CONV_PAD = 16

f32 = jnp.float32
bf16 = jnp.bfloat16


def _sigmoid(x):
    return 1.0 / (1.0 + jnp.exp(-x))


def _silu(x):
    return x * _sigmoid(x)


def _gelu(x):
    return 0.5 * x * (1.0 + lax.erf(x * math.sqrt(0.5)))


def _layer_norm(x, g, b):
    mu = jnp.mean(x, axis=-1, keepdims=True)
    xc = x - mu
    var = jnp.mean(xc * xc, axis=-1, keepdims=True)
    return xc * lax.rsqrt(var + EPS) * g + b


def _rms_norm(x, g):
    ms = jnp.mean(x * x, axis=-1, keepdims=True)
    return x * lax.rsqrt(ms + EPS) * g


def _params(*semantics):
    return pltpu.CompilerParams(dimension_semantics=semantics, vmem_limit_bytes=VMEM_LIMIT)


def _pre_norm_kernel(x_ref, g_ref, h_ref):
    h_ref[...] = _rms_norm(x_ref[...], g_ref[...]).astype(h_ref.dtype)


def _pre_norm(x2, g, tm=1024):
    n_tok = x2.shape[0]
    return pl.pallas_call(
        _pre_norm_kernel,
        grid=(n_tok // tm,),
        in_specs=[pl.BlockSpec((tm, D_MODEL), lambda i: (i, 0)),
                  pl.BlockSpec((1, D_MODEL), lambda i: (0, 0))],
        out_specs=pl.BlockSpec((tm, D_MODEL), lambda i: (i, 0)),
        out_shape=jax.ShapeDtypeStruct((n_tok, D_MODEL), bf16),
        compiler_params=_params("arbitrary"),
        name="pre_norm",
    )(x2, g)


def _in_proj_kernel(h_ref, w_ref, z_ref, wb_scr):
    @pl.when(pl.program_id(1) == 0)
    def _():
        wb_scr[...] = w_ref[...].astype(bf16)

    z_ref[...] = jnp.dot(h_ref[...], wb_scr[...], preferred_element_type=f32).astype(z_ref.dtype)


def _in_proj(h, w_in, layer, tm=2048, tn=1024):
    n_tok = h.shape[0]
    return pl.pallas_call(
        _in_proj_kernel,
        grid=(D_IN // tn, n_tok // tm),
        in_specs=[
            pl.BlockSpec((tm, D_MODEL), lambda j, i: (i, 0)),
            pl.BlockSpec((None, D_MODEL, tn), lambda j, i: (layer, 0, j)),
        ],
        out_specs=pl.BlockSpec((tm, tn), lambda j, i: (i, j)),
        out_shape=jax.ShapeDtypeStruct((n_tok, D_IN), bf16),
        scratch_shapes=[pltpu.VMEM((D_MODEL, tn), bf16)],
        compiler_params=_params("arbitrary", "arbitrary"),
        name="in_proj",
    )(h, w_in)


def _attn_kernel(q_ref, k_ref, v_ref, g_ref, vec_ref, o_ref, bias_ref, *, rows, heads):
    scale = HEAD_DIM ** -0.5
    nq = Q_ROWS * GRID_W
    nk = K_ROWS * GRID_W
    w_idx = lax.broadcasted_iota(jnp.int32, (GRID_W, PAIR), 0)
    lane = lax.broadcasted_iota(jnp.int32, (GRID_W, PAIR), 1)
    left_half = lane < GRID_W
    kc = jnp.where(left_half, lane, lane - GRID_W)
    col_start = jnp.clip(w_idx - WIN_W // 2, 0, GRID_W - WIN_W)
    in_win = (kc >= col_start) & (kc < col_start + WIN_W)
    for hh in range(heads):
        for e in range(2 * WIN_H):
            vec = jnp.broadcast_to(vec_ref[hh, e:e + 1, :], (GRID_W, PAIR))
            bias_ref[hh, e] = jnp.where(in_win, pltpu.roll(vec, 0, 1, stride=1, stride_axis=0), MASKED)

    n_blk = rows // Q_ROWS
    first_key_row = lambda blk: min(max(blk * Q_ROWS - WIN_H // 2, 0), rows - K_ROWS)

    head_cols = lambda hh: slice(hh * HEAD_DIM, (hh + 1) * HEAD_DIM)
    items = [(hh, blk) for hh in range(heads) for blk in range(n_blk)]

    def scores(item):
        hh, blk = item
        q0, k0 = blk * nq, first_key_row(blk) * GRID_W
        q = (q_ref[0, q0:q0 + nq, head_cols(hh)].astype(f32) * scale).astype(bf16)
        kw = k_ref[0, k0:k0 + nk, head_cols(hh)]
        return lax.dot_general(q, kw, (((1,), (1,)), ((), ())), preferred_element_type=f32)

    s_next = scores(items[0])
    for pos, (hh, blk) in enumerate(items):
        ks = first_key_row(blk)
        q0, k0 = blk * nq, ks * GRID_W
        vw = v_ref[0, k0:k0 + nk, head_cols(hh)]
        s = s_next
        if pos + 1 < len(items):
            s_next = scores(items[pos + 1])

        p_rows, inv_l = [], []
        for rq in range(Q_ROWS):
            r = blk * Q_ROWS + rq
            rs = min(max(r - WIN_H // 2, 0), rows - WIN_H)
            pieces = {}
            for ip in range(K_ROWS // 2):
                kr = ks + 2 * ip
                ok0, ok1 = rs <= kr < rs + WIN_H, rs <= kr + 1 < rs + WIN_H
                if not (ok0 or ok1):
                    continue
                dr0 = kr - r + (WIN_H - 1)
                sub = s[rq * GRID_W:(rq + 1) * GRID_W, ip * PAIR:(ip + 1) * PAIR]
                if ok0 and ok1:
                    bias = bias_ref[hh, dr0 + 1]
                elif ok0:
                    bias = jnp.where(left_half, bias_ref[hh, dr0 + 1], MASKED)
                else:
                    bias = jnp.where(left_half, MASKED, bias_ref[hh, dr0 + 1])
                pieces[ip] = sub + bias
            m = jnp.max(functools.reduce(jnp.maximum, pieces.values()), axis=-1, keepdims=True)
            pieces = {ip: jnp.exp(sv - m) for ip, sv in pieces.items()}
            l = jnp.sum(functools.reduce(jnp.add, pieces.values()), axis=-1, keepdims=True)
            inv_l.append(1.0 / l)
            zero = jnp.zeros((GRID_W, PAIR), bf16)
            p_rows.append(jnp.concatenate(
                [pieces[ip].astype(bf16) if ip in pieces else zero for ip in range(K_ROWS // 2)], axis=1))
        p = jnp.concatenate(p_rows, axis=0)
        o = jnp.dot(p, vw, preferred_element_type=f32)
        for rq in range(Q_ROWS):
            rows_q = slice(q0 + rq * GRID_W, q0 + (rq + 1) * GRID_W)
            g = g_ref[0, rows_q, head_cols(hh)].astype(f32)
            o_rq = o[rq * GRID_W:(rq + 1) * GRID_W] * inv_l[rq]
            o_ref[0, rows_q, head_cols(hh)] = (o_rq * _silu(g)).astype(o_ref.dtype)


def _attention(z3, vecs, heads=2):
    bsz, t, _ = z3.shape
    rows = t // GRID_W
    assert rows % Q_ROWS == 0 and rows >= K_ROWS and N_HEADS % heads == 0
    groups = N_HEADS // heads
    blk = (1, t, heads * HEAD_DIM)
    return pl.pallas_call(
        functools.partial(_attn_kernel, rows=rows, heads=heads),
        grid=(bsz, groups),
        in_specs=[
            pl.BlockSpec(blk, lambda b, h: (b, 0, h)),
            pl.BlockSpec(blk, lambda b, h: (b, 0, groups + h)),
            pl.BlockSpec(blk, lambda b, h: (b, 0, 2 * groups + h)),
            pl.BlockSpec(blk, lambda b, h: (b, 0, 3 * groups + h)),
            pl.BlockSpec((heads, 2 * WIN_H, PAIR), lambda b, h: (h, 0, 0)),
        ],
        out_specs=pl.BlockSpec(blk, lambda b, h: (b, 0, h)),
        out_shape=jax.ShapeDtypeStruct((bsz, t, D_A), bf16),
        scratch_shapes=[pltpu.VMEM((heads, 2 * WIN_H, GRID_W, PAIR), f32)],
        compiler_params=_params("arbitrary", "arbitrary"),
        name="nbr_attention",
    )(z3, z3, z3, z3, vecs)


def _attention_bias_vectors(rpb):
    half = WIN_W - 1
    pad = jnp.full((N_HEADS, 1, 2 * WIN_W - 1), MASKED, f32)
    ext = jnp.concatenate([pad, rpb.astype(f32), pad], axis=1)
    lower, upper = ext[:, :-1], ext[:, 1:]
    gap = jnp.full((N_HEADS, 2 * WIN_H, GRID_W - 2 * half - 1), MASKED, f32)
    return jnp.concatenate([lower[..., half:], gap, upper, gap, lower[..., :half]], axis=-1)


def _conv_kernel(a_ref, b_ref, g_ref, cw_ref, cb_ref, lng_ref, lnb_ref, pw_ref, pb_ref, o_ref,
                 h_scr, c_scr, *, t, tt, tm):
    zeros = jnp.zeros((CONV_PAD, D_C), f32)
    h_scr[pl.ds(0, CONV_PAD), :] = zeros
    h_scr[pl.ds(CONV_PAD + t, CONV_PAD), :] = zeros
    h_scr[pl.ds(CONV_PAD, t), :] = a_ref[0].astype(f32) * _sigmoid(b_ref[0].astype(f32))

    shift = CONV_PAD - CONV_W // 2
    n_al = (CONV_W + shift + SUBLANES - 1) // SUBLANES
    win_rows = tt + n_al * SUBLANES

    def conv_tile(ti, carry):
        t0 = pl.multiple_of(ti * tt, tt)
        for cb in range(D_C // LANES):
            cols = slice(cb * LANES, (cb + 1) * LANES)
            win = h_scr[pl.ds(t0, win_rows), cols]
            acc = None
            for b in range(SUBLANES):
                part = None
                for a in range(n_al):
                    k = SUBLANES * a + b - shift
                    if 0 <= k < CONV_W:
                        term = cw_ref[k:k + 1, cols] * win[SUBLANES * a:SUBLANES * a + tt + SUBLANES]
                        part = term if part is None else part + term
                part = part[b:b + tt]
                acc = part if acc is None else acc + part
            c_scr[pl.ds(t0, tt), cols] = acc
        return carry

    lax.fori_loop(0, t // tt, conv_tile, 0)

    def post_tile(ti, carry):
        t0 = pl.multiple_of(ti * tm, tm)
        c = c_scr[pl.ds(t0, tm), :] + cb_ref[...]
        c = _silu(_layer_norm(c, lng_ref[...], lnb_ref[...])).astype(bf16)
        y = jnp.dot(c, pw_ref[...], preferred_element_type=f32) + pb_ref[...]
        g = g_ref[0, pl.ds(t0, tm), :].astype(f32)
        o_ref[0, pl.ds(t0, tm), :] = (y * _silu(g)).astype(o_ref.dtype)
        return carry

    lax.fori_loop(0, t // tm, post_tile, 0)


def _conformer_conv(z3, conv_w, conv_b, ln_g, ln_b, pw_w, pw_b, tt=64, tm=512):
    bsz, t, _ = z3.shape
    assert t % tt == 0 and t % tm == 0
    c0 = (4 * D_A + 3 * D_B) // D_C
    blk = (1, t, D_C)
    const2 = lambda b: (0, 0)
    return pl.pallas_call(
        functools.partial(_conv_kernel, t=t, tt=tt, tm=tm),
        grid=(bsz,),
        in_specs=[
            pl.BlockSpec(blk, lambda b: (b, 0, c0)),
            pl.BlockSpec(blk, lambda b: (b, 0, c0 + 1)),
            pl.BlockSpec(blk, lambda b: (b, 0, c0 + 2)),
            pl.BlockSpec((CONV_W, D_C), const2),
            pl.BlockSpec((1, D_C), const2),
            pl.BlockSpec((1, D_C), const2),
            pl.BlockSpec((1, D_C), const2),
            pl.BlockSpec((D_C, D_C), const2),
            pl.BlockSpec((1, D_C), const2),
        ],
        out_specs=pl.BlockSpec(blk, lambda b: (b, 0, 0)),
        out_shape=jax.ShapeDtypeStruct((bsz, t, D_C), bf16),
        scratch_shapes=[pltpu.VMEM((t + 2 * CONV_PAD, D_C), f32), pltpu.VMEM((t, D_C), f32)],
        compiler_params=_params("arbitrary"),
        name="conformer_conv",
    )(z3, z3, z3, conv_w, conv_b, ln_g, ln_b, pw_w, pw_b)


def _spatial_gating(u_ref, v_ref, g_ref, lng_ref, lnb_ref, ws_ref, bs_ref, o_ref, col0, r0, nr):
    gd = D_B // B_GROUPS
    for n in range(r0 // CHUNK, (r0 + nr) // CHUNK):
        rows = slice(n * CHUNK, (n + 1) * CHUNK)
        v = _layer_norm(_gelu(v_ref[rows, :].astype(f32)), lng_ref[...], lnb_ref[...]).astype(bf16)
        for grp in range(B_GROUPS):
            cols = slice(grp * gd, (grp + 1) * gd)
            s = jnp.dot(ws_ref[grp], v[:, cols], preferred_element_type=f32) + bs_ref[:, cols]
            u = _gelu(u_ref[rows, cols].astype(f32))
            g = g_ref[rows, cols].astype(f32)
            o_ref[rows, col0 + grp * gd:col0 + (grp + 1) * gd] = (u * s * _silu(g)).astype(o_ref.dtype)


def _out_sgu_kernel(ya_ref, yc_ref, w_ref, pg_ref, x_ref, gn_ref,
                    zb_ref, lng_ref, lnb_ref, ws_ref, bs_ref, o_ref, *rest, tm, n_sub, emit_h):
    cat_scr = rest[-1]
    sub = tm // n_sub
    u_ref, v_ref, g_ref = (zb_ref.at[:, k * D_B:(k + 1) * D_B] for k in range(3))

    def project(i):
        rows = slice(i * sub, (i + 1) * sub)
        cat_scr[rows, 0:D_A] = ya_ref[rows, :]
        _spatial_gating(u_ref, v_ref, g_ref, lng_ref, lnb_ref, ws_ref, bs_ref, cat_scr, D_A, i * sub, sub)
        cat_scr[rows, D_A + D_B:D_MIX] = yc_ref[rows, :]
        return jnp.dot(cat_scr[rows, :], w_ref[...], preferred_element_type=f32)

    def finish(i, y):
        rows = slice(i * sub, (i + 1) * sub)
        x_new = x_ref[rows, :] + _rms_norm(y, pg_ref[...])
        o_ref[rows, :] = x_new
        if emit_h:
            h_ref = rest[0]
            h_ref[rows, :] = _rms_norm(x_new, gn_ref[...]).astype(h_ref.dtype)

    y_prev = project(0)
    for i in range(1, n_sub):
        y_cur = project(i)
        finish(i - 1, y_prev)
        y_prev = y_cur
    finish(n_sub - 1, y_prev)


def _out_proj_sgu(ya, yc, z2, w_all, layer, post_g, x2, next_g, ln_g, ln_b, w_s, b_map, *, emit_h,
                  tm=512, n_sub=2):
    n_tok = x2.shape[0]
    assert tm % (n_sub * CHUNK) == 0
    zb_width = 4 * D_B
    assert (4 * D_A) % zb_width == 0 and 4 * D_A + 3 * D_B <= (4 * D_A // zb_width + 1) * zb_width
    const2 = lambda i: (0, 0)
    tile = lambda width, col=0: pl.BlockSpec((tm, width), lambda i: (i, col))
    out_shape = [jax.ShapeDtypeStruct((n_tok, D_MODEL), f32)]
    out_specs = [tile(D_MODEL)]
    if emit_h:
        out_shape.append(jax.ShapeDtypeStruct((n_tok, D_MODEL), bf16))
        out_specs.append(tile(D_MODEL))
    return pl.pallas_call(
        functools.partial(_out_sgu_kernel, tm=tm, n_sub=n_sub, emit_h=emit_h),
        grid=(n_tok // tm,),
        in_specs=[
            tile(D_A), tile(D_C),
            pl.BlockSpec((None, D_MIX, D_MODEL), lambda i: (layer, 0, 0)),
            pl.BlockSpec((1, D_MODEL), const2),
            tile(D_MODEL),
            pl.BlockSpec((1, D_MODEL), const2),
            tile(zb_width, 4 * D_A // zb_width),
            pl.BlockSpec((1, D_B), const2),
            pl.BlockSpec((1, D_B), const2),
            pl.BlockSpec((B_GROUPS, CHUNK, CHUNK), lambda i: (0, 0, 0)),
            pl.BlockSpec((CHUNK, D_B), const2),
        ],
        out_specs=out_specs,
        out_shape=out_shape,
        scratch_shapes=[pltpu.VMEM((tm, D_MIX), bf16)],
        compiler_params=_params("arbitrary"),
        name="out_proj_sgu",
    )(ya, yc, w_all, post_g, x2, next_g, z2, ln_g, ln_b, w_s, b_map)


def kernel(x, pre_norm_g, w_in, attn_rpb, sgu_ln_g, sgu_ln_b, sgu_w, sgu_b, conv_w, conv_b,
           conv_ln_g, conv_ln_b, conv_pw_w, conv_pw_b, w_out, post_norm_g):
    bsz, t, d = x.shape
    depth = w_in.shape[0]
    assert d == D_MODEL and t % GRID_W == 0 and t % CHUNK == 0
    n_tok = bsz * t
    x2 = x.reshape(n_tok, d)
    row = lambda a: a.reshape(1, -1).astype(f32)
    h = _pre_norm(x2, row(pre_norm_g[0]))
    w_out_bf = w_out.astype(bf16)
    for l in range(depth):
        z2 = _in_proj(h, w_in, l)
        z3 = z2.reshape(bsz, t, D_IN)
        y_a = _attention(z3, _attention_bias_vectors(attn_rpb[l]))
        b_map = jnp.repeat(sgu_b[l].astype(f32).T, D_B // B_GROUPS, axis=1)
        y_c = _conformer_conv(z3, conv_w[l].reshape(CONV_W, D_C).astype(f32), row(conv_b[l]),
                              row(conv_ln_g[l]), row(conv_ln_b[l]), conv_pw_w[l].astype(bf16),
                              row(conv_pw_b[l]))
        last = l == depth - 1
        next_g = post_norm_g[l] if last else pre_norm_g[l + 1]
        outs = _out_proj_sgu(
            y_a.reshape(n_tok, D_A), y_c.reshape(n_tok, D_C), z2, w_out_bf, l,
            row(post_norm_g[l]), x2, row(next_g), row(sgu_ln_g[l]), row(sgu_ln_b[l]),
            sgu_w[l].astype(bf16), b_map, emit_h=not last)
        x2 = outs[0]
        if not last:
            h = outs[1]
    return x2.reshape(bsz, t, d)
```

```python
import functools
import math

import jax
import jax.numpy as jnp
from jax import lax
from jax.experimental import pallas as pl
from jax.experimental.pallas import tpu as pltpu

D_MODEL = 2048
GRID_W = 64
WIN_H = 8
WIN_W = 16
HEAD_DIM = 128
D_A = 1024
N_HEADS = 8
CHUNK = 128
D_B = 512
B_GROUPS = 4
D_C = 512
CONV_W = 31
D_MIX = D_A + D_B + D_C
D_IN = 4 * D_A + 3 * D_B + 3 * D_C
EPS = 1e-6
MASKED = -1e30

LANES = 128
SUBLANES = 8
VMEM_LIMIT = 56 * 1024 * 1024

Q_ROWS = 4
K_ROWS = Q_ROWS + WIN_H
PAIR = 2 * GRID_W
assert PAIR == LANES

CONV_PAD = 16

f32 = jnp.float32
bf16 = jnp.bfloat16


def _sigmoid(x):
    return 1.0 / (1.0 + jnp.exp(-x))


def _silu(x):
    return x * _sigmoid(x)


def _gelu(x):
    return 0.5 * x * (1.0 + lax.erf(x * math.sqrt(0.5)))


def _layer_norm(x, g, b):
    mu = jnp.mean(x, axis=-1, keepdims=True)
    xc = x - mu
    var = jnp.mean(xc * xc, axis=-1, keepdims=True)
    return xc * lax.rsqrt(var + EPS) * g + b


def _rms_norm(x, g):
    ms = jnp.mean(x * x, axis=-1, keepdims=True)
    return x * lax.rsqrt(ms + EPS) * g


def _params(*semantics):
    return pltpu.CompilerParams(dimension_semantics=semantics, vmem_limit_bytes=VMEM_LIMIT)


def _pre_norm_kernel(x_ref, g_ref, h_ref):
    h_ref[...] = _rms_norm(x_ref[...], g_ref[...]).astype(h_ref.dtype)


def _pre_norm(x2, g, tm=1024):
    n_tok = x2.shape[0]
    return pl.pallas_call(
        _pre_norm_kernel,
        grid=(n_tok // tm,),
        in_specs=[pl.BlockSpec((tm, D_MODEL), lambda i: (i, 0)),
                  pl.BlockSpec((1, D_MODEL), lambda i: (0, 0))],
        out_specs=pl.BlockSpec((tm, D_MODEL), lambda i: (i, 0)),
        out_shape=jax.ShapeDtypeStruct((n_tok, D_MODEL), bf16),
        compiler_params=_params("arbitrary"),
        name="pre_norm",
    )(x2, g)


def _in_proj_kernel(h_ref, w_ref, z_ref, wb_scr):
    @pl.when(pl.program_id(1) == 0)
    def _():
        wb_scr[...] = w_ref[...].astype(bf16)

    z_ref[...] = jnp.dot(h_ref[...], wb_scr[...], preferred_element_type=f32).astype(z_ref.dtype)


def _in_proj(h, w_in, layer, tm=2048, tn=1024):
    n_tok = h.shape[0]
    return pl.pallas_call(
        _in_proj_kernel,
        grid=(D_IN // tn, n_tok // tm),
        in_specs=[
            pl.BlockSpec((tm, D_MODEL), lambda j, i: (i, 0)),
            pl.BlockSpec((None, D_MODEL, tn), lambda j, i: (layer, 0, j)),
        ],
        out_specs=pl.BlockSpec((tm, tn), lambda j, i: (i, j)),
        out_shape=jax.ShapeDtypeStruct((n_tok, D_IN), bf16),
        scratch_shapes=[pltpu.VMEM((D_MODEL, tn), bf16)],
        compiler_params=_params("arbitrary", "arbitrary"),
        name="in_proj",
    )(h, w_in)


def _attn_kernel(q_ref, k_ref, v_ref, g_ref, vec_ref, wo_ref, o_ref, wob_ref, bias_ref, *, rows, heads):
    wob_ref[...] = wo_ref[...].astype(bf16)
    scale = HEAD_DIM ** -0.5
    nq = Q_ROWS * GRID_W
    nk = K_ROWS * GRID_W
    w_idx = lax.broadcasted_iota(jnp.int32, (GRID_W, PAIR), 0)
    lane = lax.broadcasted_iota(jnp.int32, (GRID_W, PAIR), 1)
    left_half = lane < GRID_W
    kc = jnp.where(left_half, lane, lane - GRID_W)
    col_start = jnp.clip(w_idx - WIN_W // 2, 0, GRID_W - WIN_W)
    in_win = (kc >= col_start) & (kc < col_start + WIN_W)
    for hh in range(heads):
        for e in range(2 * WIN_H):
            vec = jnp.broadcast_to(vec_ref[hh, e:e + 1, :], (GRID_W, PAIR))
            bias_ref[hh, e] = jnp.where(in_win, pltpu.roll(vec, 0, 1, stride=1, stride_axis=0), MASKED)

    n_blk = rows // Q_ROWS
    first_key_row = lambda blk: min(max(blk * Q_ROWS - WIN_H // 2, 0), rows - K_ROWS)

    head_cols = lambda hh: slice(hh * HEAD_DIM, (hh + 1) * HEAD_DIM)
    items = [(hh, blk) for hh in range(heads) for blk in range(n_blk)]

    def scores(item):
        hh, blk = item
        q0, k0 = blk * nq, first_key_row(blk) * GRID_W
        q = (q_ref[0, q0:q0 + nq, head_cols(hh)].astype(f32) * scale).astype(bf16)
        kw = k_ref[0, k0:k0 + nk, head_cols(hh)]
        return lax.dot_general(q, kw, (((1,), (1,)), ((), ())), preferred_element_type=f32)

    s_next = scores(items[0])
    for pos, (hh, blk) in enumerate(items):
        ks = first_key_row(blk)
        q0, k0 = blk * nq, ks * GRID_W
        vw = v_ref[0, k0:k0 + nk, head_cols(hh)]
        s = s_next
        if pos + 1 < len(items):
            s_next = scores(items[pos + 1])

        p_rows, inv_l = [], []
        for rq in range(Q_ROWS):
            r = blk * Q_ROWS + rq
            rs = min(max(r - WIN_H // 2, 0), rows - WIN_H)
            pieces = {}
            for ip in range(K_ROWS // 2):
                kr = ks + 2 * ip
                ok0, ok1 = rs <= kr < rs + WIN_H, rs <= kr + 1 < rs + WIN_H
                if not (ok0 or ok1):
                    continue
                dr0 = kr - r + (WIN_H - 1)
                sub = s[rq * GRID_W:(rq + 1) * GRID_W, ip * PAIR:(ip + 1) * PAIR]
                if ok0 and ok1:
                    bias = bias_ref[hh, dr0 + 1]
                elif ok0:
                    bias = jnp.where(left_half, bias_ref[hh, dr0 + 1], MASKED)
                else:
                    bias = jnp.where(left_half, MASKED, bias_ref[hh, dr0 + 1])
                pieces[ip] = sub + bias
            m = jnp.max(functools.reduce(jnp.maximum, pieces.values()), axis=-1, keepdims=True)
            pieces = {ip: jnp.exp(sv - m) for ip, sv in pieces.items()}
            l = jnp.sum(functools.reduce(jnp.add, pieces.values()), axis=-1, keepdims=True)
            inv_l.append(1.0 / l)
            zero = jnp.zeros((GRID_W, PAIR), bf16)
            p_rows.append(jnp.concatenate(
                [pieces[ip].astype(bf16) if ip in pieces else zero for ip in range(K_ROWS // 2)], axis=1))
        p = jnp.concatenate(p_rows, axis=0)
        o = jnp.dot(p, vw, preferred_element_type=f32)
        for rq in range(Q_ROWS):
            rows_q = slice(q0 + rq * GRID_W, q0 + (rq + 1) * GRID_W)
            g = g_ref[0, rows_q, head_cols(hh)].astype(f32)
            o_rq = o[rq * GRID_W:(rq + 1) * GRID_W] * inv_l[rq]
            o_ref[0, rows_q, head_cols(hh)] = (o_rq * _silu(g)).astype(o_ref.dtype)


def _attention(z3, vecs, w_out, layer, heads=2):
    bsz, t, _ = z3.shape
    rows = t // GRID_W
    assert rows % Q_ROWS == 0 and rows >= K_ROWS and N_HEADS % heads == 0
    groups = N_HEADS // heads
    slab = D_MIX // (bsz * groups)
    assert slab * bsz * groups == D_MIX and slab % (2 * SUBLANES) == 0
    blk = (1, t, heads * HEAD_DIM)
    return pl.pallas_call(
        functools.partial(_attn_kernel, rows=rows, heads=heads),
        grid=(bsz, groups),
        in_specs=[
            pl.BlockSpec(blk, lambda b, h: (b, 0, h)),
            pl.BlockSpec(blk, lambda b, h: (b, 0, groups + h)),
            pl.BlockSpec(blk, lambda b, h: (b, 0, 2 * groups + h)),
            pl.BlockSpec(blk, lambda b, h: (b, 0, 3 * groups + h)),
            pl.BlockSpec((heads, 2 * WIN_H, PAIR), lambda b, h: (h, 0, 0)),
            pl.BlockSpec((None, slab, D_MODEL), lambda b, h: (layer, b * groups + h, 0)),
        ],
        out_specs=[pl.BlockSpec(blk, lambda b, h: (b, 0, h)),
                   pl.BlockSpec((slab, D_MODEL), lambda b, h: (b * groups + h, 0))],
        out_shape=[jax.ShapeDtypeStruct((bsz, t, D_A), bf16),
                   jax.ShapeDtypeStruct((D_MIX, D_MODEL), bf16)],
        scratch_shapes=[pltpu.VMEM((heads, 2 * WIN_H, GRID_W, PAIR), f32)],
        compiler_params=_params("arbitrary", "arbitrary"),
        name="nbr_attention",
    )(z3, z3, z3, z3, vecs, w_out)


def _attention_bias_vectors(rpb):
    half = WIN_W - 1
    pad = jnp.full((N_HEADS, 1, 2 * WIN_W - 1), MASKED, f32)
    ext = jnp.concatenate([pad, rpb.astype(f32), pad], axis=1)
    lower, upper = ext[:, :-1], ext[:, 1:]
    gap = jnp.full((N_HEADS, 2 * WIN_H, GRID_W - 2 * half - 1), MASKED, f32)
    return jnp.concatenate([lower[..., half:], gap, upper, gap, lower[..., :half]], axis=-1)


def _conv_kernel(a_ref, b_ref, g_ref, cw_ref, cb_ref, lng_ref, lnb_ref, pw_ref, pb_ref, o_ref,
                 h_scr, c_scr, *, t, tt, tm):
    zeros = jnp.zeros((CONV_PAD, D_C), f32)
    h_scr[pl.ds(0, CONV_PAD), :] = zeros
    h_scr[pl.ds(CONV_PAD + t, CONV_PAD), :] = zeros
    h_scr[pl.ds(CONV_PAD, t), :] = a_ref[0].astype(f32) * _sigmoid(b_ref[0].astype(f32))

    shift = CONV_PAD - CONV_W // 2
    n_al = (CONV_W + shift + SUBLANES - 1) // SUBLANES
    win_rows = tt + n_al * SUBLANES

    def conv_tile(ti, carry):
        t0 = pl.multiple_of(ti * tt, tt)
        for cb in range(D_C // LANES):
            cols = slice(cb * LANES, (cb + 1) * LANES)
            win = h_scr[pl.ds(t0, win_rows), cols]
            acc = None
            for b in range(SUBLANES):
                part = None
                for a in range(n_al):
                    k = SUBLANES * a + b - shift
                    if 0 <= k < CONV_W:
                        term = cw_ref[k:k + 1, cols] * win[SUBLANES * a:SUBLANES * a + tt + SUBLANES]
                        part = term if part is None else part + term
                part = part[b:b + tt]
                acc = part if acc is None else acc + part
            c_scr[pl.ds(t0, tt), cols] = acc
        return carry

    lax.fori_loop(0, t // tt, conv_tile, 0)

    def post_tile(ti, carry):
        t0 = pl.multiple_of(ti * tm, tm)
        c = c_scr[pl.ds(t0, tm), :] + cb_ref[...]
        c = _silu(_layer_norm(c, lng_ref[...], lnb_ref[...])).astype(bf16)
        y = jnp.dot(c, pw_ref[...], preferred_element_type=f32) + pb_ref[...]
        g = g_ref[0, pl.ds(t0, tm), :].astype(f32)
        o_ref[0, pl.ds(t0, tm), :] = (y * _silu(g)).astype(o_ref.dtype)
        return carry

    lax.fori_loop(0, t // tm, post_tile, 0)


def _conformer_conv(z3, conv_w, conv_b, ln_g, ln_b, pw_w, pw_b, tt=128, tm=512):
    bsz, t, _ = z3.shape
    assert t % tt == 0 and t % tm == 0
    c0 = (4 * D_A + 3 * D_B) // D_C
    blk = (1, t, D_C)
    const2 = lambda b: (0, 0)
    return pl.pallas_call(
        functools.partial(_conv_kernel, t=t, tt=tt, tm=tm),
        grid=(bsz,),
        in_specs=[
            pl.BlockSpec(blk, lambda b: (b, 0, c0)),
            pl.BlockSpec(blk, lambda b: (b, 0, c0 + 1)),
            pl.BlockSpec(blk, lambda b: (b, 0, c0 + 2)),
            pl.BlockSpec((CONV_W, D_C), const2),
            pl.BlockSpec((1, D_C), const2),
            pl.BlockSpec((1, D_C), const2),
            pl.BlockSpec((1, D_C), const2),
            pl.BlockSpec((D_C, D_C), const2),
            pl.BlockSpec((1, D_C), const2),
        ],
        out_specs=pl.BlockSpec(blk, lambda b: (b, 0, 0)),
        out_shape=jax.ShapeDtypeStruct((bsz, t, D_C), bf16),
        scratch_shapes=[pltpu.VMEM((t + 2 * CONV_PAD, D_C), f32), pltpu.VMEM((t, D_C), f32)],
        compiler_params=_params("arbitrary"),
        name="conformer_conv",
    )(z3, z3, z3, conv_w, conv_b, ln_g, ln_b, pw_w, pw_b)


def _spatial_gating(u_ref, v_ref, g_ref, lng_ref, lnb_ref, ws_ref, bs_ref, o_ref, col0, r0, nr):
    gd = D_B // B_GROUPS
    for n in range(r0 // CHUNK, (r0 + nr) // CHUNK):
        rows = slice(n * CHUNK, (n + 1) * CHUNK)
        v = _layer_norm(_gelu(v_ref[rows, :].astype(f32)), lng_ref[...], lnb_ref[...]).astype(bf16)
        for grp in range(B_GROUPS):
            cols = slice(grp * gd, (grp + 1) * gd)
            s = jnp.dot(ws_ref[grp], v[:, cols], preferred_element_type=f32) + bs_ref[:, cols]
            u = _gelu(u_ref[rows, cols].astype(f32))
            g = g_ref[rows, cols].astype(f32)
            o_ref[rows, col0 + grp * gd:col0 + (grp + 1) * gd] = (u * s * _silu(g)).astype(o_ref.dtype)


def _out_sgu_kernel(ya_ref, yc_ref, w_ref, pg_ref, x_ref, gn_ref,
                    zb_ref, lng_ref, lnb_ref, ws_ref, bs_ref, o_ref, *rest, tm, n_sub, emit_h):
    cat_scr = rest[-1]
    sub = tm // n_sub
    u_ref, v_ref, g_ref = (zb_ref.at[:, k * D_B:(k + 1) * D_B] for k in range(3))

    def project(i):
        rows = slice(i * sub, (i + 1) * sub)
        cat_scr[rows, 0:D_A] = ya_ref[rows, :]
        _spatial_gating(u_ref, v_ref, g_ref, lng_ref, lnb_ref, ws_ref, bs_ref, cat_scr, D_A, i * sub, sub)
        cat_scr[rows, D_A + D_B:D_MIX] = yc_ref[rows, :]
        return jnp.dot(cat_scr[rows, :], w_ref[...], preferred_element_type=f32)

    def finish(i, y):
        rows = slice(i * sub, (i + 1) * sub)
        x_new = x_ref[rows, :] + _rms_norm(y, pg_ref[...])
        o_ref[rows, :] = x_new
        if emit_h:
            h_ref = rest[0]
            h_ref[rows, :] = _rms_norm(x_new, gn_ref[...]).astype(h_ref.dtype)

    y_prev = project(0)
    for i in range(1, n_sub):
        y_cur = project(i)
        finish(i - 1, y_prev)
        y_prev = y_cur
    finish(n_sub - 1, y_prev)


def _out_proj_sgu(ya, yc, z2, w, post_g, x2, next_g, ln_g, ln_b, w_s, b_map, *, emit_h, tm=512, n_sub=2):
    n_tok = x2.shape[0]
    assert tm % (n_sub * CHUNK) == 0
    zb_width = 4 * D_B
    assert (4 * D_A) % zb_width == 0 and 4 * D_A + 3 * D_B <= (4 * D_A // zb_width + 1) * zb_width
    const2 = lambda i: (0, 0)
    tile = lambda width, col=0: pl.BlockSpec((tm, width), lambda i: (i, col))
    out_shape = [jax.ShapeDtypeStruct((n_tok, D_MODEL), f32)]
    out_specs = [tile(D_MODEL)]
    if emit_h:
        out_shape.append(jax.ShapeDtypeStruct((n_tok, D_MODEL), bf16))
        out_specs.append(tile(D_MODEL))
    return pl.pallas_call(
        functools.partial(_out_sgu_kernel, tm=tm, n_sub=n_sub, emit_h=emit_h),
        grid=(n_tok // tm,),
        in_specs=[
            tile(D_A), tile(D_C),
            pl.BlockSpec((D_MIX, D_MODEL), const2),
            pl.BlockSpec((1, D_MODEL), const2),
            tile(D_MODEL),
            pl.BlockSpec((1, D_MODEL), const2),
            tile(zb_width, 4 * D_A // zb_width),
            pl.BlockSpec((1, D_B), const2),
            pl.BlockSpec((1, D_B), const2),
            pl.BlockSpec((B_GROUPS, CHUNK, CHUNK), lambda i: (0, 0, 0)),
            pl.BlockSpec((CHUNK, D_B), const2),
        ],
        out_specs=out_specs,
        out_shape=out_shape,
        scratch_shapes=[pltpu.VMEM((tm, D_MIX), bf16)],
        compiler_params=_params("arbitrary"),
        name="out_proj_sgu",
    )(ya, yc, w, post_g, x2, next_g, z2, ln_g, ln_b, w_s, b_map)


def kernel(x, pre_norm_g, w_in, attn_rpb, sgu_ln_g, sgu_ln_b, sgu_w, sgu_b, conv_w, conv_b,
           conv_ln_g, conv_ln_b, conv_pw_w, conv_pw_b, w_out, post_norm_g):
    bsz, t, d = x.shape
    depth = w_in.shape[0]
    assert d == D_MODEL and t % GRID_W == 0 and t % CHUNK == 0
    n_tok = bsz * t
    x2 = x.reshape(n_tok, d)
    row = lambda a: a.reshape(1, -1).astype(f32)
    h = _pre_norm(x2, row(pre_norm_g[0]))
    for l in range(depth):
        z2 = _in_proj(h, w_in, l)
        z3 = z2.reshape(bsz, t, D_IN)
        y_a, w_out_bf = _attention(z3, _attention_bias_vectors(attn_rpb[l]), w_out, l)
        b_map = jnp.repeat(sgu_b[l].astype(f32).T, D_B // B_GROUPS, axis=1)
        y_c = _conformer_conv(z3, conv_w[l].reshape(CONV_W, D_C).astype(f32), row(conv_b[l]),
                              row(conv_ln_g[l]), row(conv_ln_b[l]), conv_pw_w[l].astype(bf16),
                              row(conv_pw_b[l]))
        last = l == depth - 1
        next_g = post_norm_g[l] if last else pre_norm_g[l + 1]
        outs = _out_proj_sgu(
            y_a.reshape(n_tok, D_A), y_c.reshape(n_tok, D_C), z2, w_out_bf,
            row(post_norm_g[l]), x2, row(next_g), row(sgu_ln_g[l]), row(sgu_ln_b[l]),
            sgu_w[l].astype(bf16), b_map, emit_h=not last)
        x2 = outs[0]
        if not last:
            h = outs[1]
    return x2.reshape(bsz, t, d)
```

```python
import functools
import math

import jax
import jax.numpy as jnp
from jax import lax
from jax.experimental import pallas as pl
from jax.experimental.pallas import tpu as pltpu

D_MODEL = 2048
GRID_W = 64
WIN_H = 8
WIN_W = 16
HEAD_DIM = 128
D_A = 1024
N_HEADS = 8
CHUNK = 128
D_B = 512
B_GROUPS = 4
D_C = 512
CONV_W = 31
D_MIX = D_A + D_B + D_C
D_IN = 4 * D_A + 3 * D_B + 3 * D_C
EPS = 1e-6
MASKED = -1e30

LANES = 128
SUBLANES = 8
VMEM_LIMIT = 56 * 1024 * 1024

Q_ROWS = 4
K_ROWS = Q_ROWS + WIN_H
PAIR = 2 * GRID_W
assert PAIR == LANES

CONV_PAD = 16

f32 = jnp.float32
bf16 = jnp.bfloat16


def _sigmoid(x):
    return 1.0 / (1.0 + jnp.exp2(x * -math.log2(math.e)))


def _silu(x):
    return x * _sigmoid(x)


def _gelu(x):
    return 0.5 * x * (1.0 + lax.erf(x * math.sqrt(0.5)))


def _layer_norm(x, g, b):
    mu = jnp.mean(x, axis=-1, keepdims=True)
    xc = x - mu
    var = jnp.mean(xc * xc, axis=-1, keepdims=True)
    return xc * lax.rsqrt(var + EPS) * g + b


def _rms_norm(x, g):
    ms = jnp.mean(x * x, axis=-1, keepdims=True)
    return x * lax.rsqrt(ms + EPS) * g


def _params(*semantics):
    return pltpu.CompilerParams(dimension_semantics=semantics, vmem_limit_bytes=VMEM_LIMIT)


def _pre_norm_kernel(x_ref, g_ref, h_ref):
    h_ref[...] = _rms_norm(x_ref[...], g_ref[...]).astype(h_ref.dtype)


def _pre_norm(x2, g, tm=1024):
    n_tok = x2.shape[0]
    return pl.pallas_call(
        _pre_norm_kernel,
        grid=(n_tok // tm,),
        in_specs=[pl.BlockSpec((tm, D_MODEL), lambda i: (i, 0)),
                  pl.BlockSpec((1, D_MODEL), lambda i: (0, 0))],
        out_specs=pl.BlockSpec((tm, D_MODEL), lambda i: (i, 0)),
        out_shape=jax.ShapeDtypeStruct((n_tok, D_MODEL), bf16),
        compiler_params=_params("arbitrary"),
        name="pre_norm",
    )(x2, g)


def _in_proj_kernel(h_ref, w_ref, z_ref, wb_scr):
    @pl.when(pl.program_id(1) == 0)
    def _():
        wb_scr[...] = w_ref[...].astype(bf16)

    z_ref[...] = jnp.dot(h_ref[...], wb_scr[...], preferred_element_type=f32).astype(z_ref.dtype)


def _in_proj(h, w_in, layer, tm=2048, tn=1024):
    n_tok = h.shape[0]
    return pl.pallas_call(
        _in_proj_kernel,
        grid=(D_IN // tn, n_tok // tm),
        in_specs=[
            pl.BlockSpec((tm, D_MODEL), lambda j, i: (i, 0)),
            pl.BlockSpec((None, D_MODEL, tn), lambda j, i: (layer, 0, j)),
        ],
        out_specs=pl.BlockSpec((tm, tn), lambda j, i: (i, j)),
        out_shape=jax.ShapeDtypeStruct((n_tok, D_IN), bf16),
        scratch_shapes=[pltpu.VMEM((D_MODEL, tn), bf16)],
        compiler_params=_params("arbitrary", "arbitrary"),
        name="in_proj",
    )(h, w_in)


def _attn_kernel(q_ref, k_ref, v_ref, g_ref, vec_ref, wo_ref, o_ref, wob_ref, bias_ref, *, rows, heads):
    wob_ref[...] = wo_ref[...].astype(bf16)
    log2e = math.log2(math.e)
    scale = HEAD_DIM ** -0.5 * log2e
    nq = Q_ROWS * GRID_W
    nk = K_ROWS * GRID_W
    w_idx = lax.broadcasted_iota(jnp.int32, (GRID_W, PAIR), 0)
    lane = lax.broadcasted_iota(jnp.int32, (GRID_W, PAIR), 1)
    left_half = lane < GRID_W
    kc = jnp.where(left_half, lane, lane - GRID_W)
    col_start = jnp.clip(w_idx - WIN_W // 2, 0, GRID_W - WIN_W)
    in_win = (kc >= col_start) & (kc < col_start + WIN_W)
    for hh in range(heads):
        for e in range(2 * WIN_H):
            vec = jnp.broadcast_to(vec_ref[hh, e:e + 1, :], (GRID_W, PAIR))
            bias = pltpu.roll(vec, 0, 1, stride=1, stride_axis=0) * log2e
            bias_ref[hh, e] = jnp.where(in_win, bias, MASKED)

    n_blk = rows // Q_ROWS
    first_key_row = lambda blk: min(max(blk * Q_ROWS - WIN_H // 2, 0), rows - K_ROWS)

    head_cols = lambda hh: slice(hh * HEAD_DIM, (hh + 1) * HEAD_DIM)
    items = [(hh, blk) for hh in range(heads) for blk in range(n_blk)]

    def scores(item):
        hh, blk = item
        q0, k0 = blk * nq, first_key_row(blk) * GRID_W
        q = (q_ref[0, q0:q0 + nq, head_cols(hh)].astype(f32) * scale).astype(bf16)
        kw = k_ref[0, k0:k0 + nk, head_cols(hh)]
        return lax.dot_general(q, kw, (((1,), (1,)), ((), ())), preferred_element_type=f32)

    s_next = scores(items[0])
    for pos, (hh, blk) in enumerate(items):
        ks = first_key_row(blk)
        q0, k0 = blk * nq, ks * GRID_W
        vw = v_ref[0, k0:k0 + nk, head_cols(hh)]
        s = s_next
        if pos + 1 < len(items):
            s_next = scores(items[pos + 1])

        p_rows = []
        for rq in range(Q_ROWS):
            r = blk * Q_ROWS + rq
            rs = min(max(r - WIN_H // 2, 0), rows - WIN_H)
            pieces = {}
            for ip in range(K_ROWS // 2):
                kr = ks + 2 * ip
                ok0, ok1 = rs <= kr < rs + WIN_H, rs <= kr + 1 < rs + WIN_H
                if not (ok0 or ok1):
                    continue
                dr0 = kr - r + (WIN_H - 1)
                sub = s[rq * GRID_W:(rq + 1) * GRID_W, ip * PAIR:(ip + 1) * PAIR]
                if ok0 and ok1:
                    bias = bias_ref[hh, dr0 + 1]
                elif ok0:
                    bias = jnp.where(left_half, bias_ref[hh, dr0 + 1], MASKED)
                else:
                    bias = jnp.where(left_half, MASKED, bias_ref[hh, dr0 + 1])
                pieces[ip] = sub + bias
            m = jnp.max(functools.reduce(jnp.maximum, pieces.values()), axis=-1, keepdims=True)
            pieces = {ip: jnp.exp2(sv - m) for ip, sv in pieces.items()}
            zero = jnp.zeros((GRID_W, PAIR), bf16)
            p_rows.append(jnp.concatenate(
                [pieces[ip].astype(bf16) if ip in pieces else zero for ip in range(K_ROWS // 2)], axis=1))
        p = jnp.concatenate(p_rows, axis=0)
        v_ext = jnp.concatenate([vw, jnp.ones((nk, HEAD_DIM), bf16)], axis=1)
        o = jnp.dot(p, v_ext, preferred_element_type=f32)
        for rq in range(Q_ROWS):
            rows_q = slice(q0 + rq * GRID_W, q0 + (rq + 1) * GRID_W)
            g = g_ref[0, rows_q, head_cols(hh)].astype(f32)
            o_rq = o[rq * GRID_W:(rq + 1) * GRID_W]
            o_rq = o_rq[:, :HEAD_DIM] * (1.0 / o_rq[:, HEAD_DIM:])
            o_ref[0, rows_q, head_cols(hh)] = (o_rq * _silu(g)).astype(o_ref.dtype)


def _attention(z3, vecs, w_out, layer, heads=2):
    bsz, t, _ = z3.shape
    rows = t // GRID_W
    assert rows % Q_ROWS == 0 and rows >= K_ROWS and N_HEADS % heads == 0
    groups = N_HEADS // heads
    slab = D_MIX // (bsz * groups)
    assert slab * bsz * groups == D_MIX and slab % (2 * SUBLANES) == 0
    blk = (1, t, heads * HEAD_DIM)
    return pl.pallas_call(
        functools.partial(_attn_kernel, rows=rows, heads=heads),
        grid=(bsz, groups),
        in_specs=[
            pl.BlockSpec(blk, lambda b, h: (b, 0, h)),
            pl.BlockSpec(blk, lambda b, h: (b, 0, groups + h)),
            pl.BlockSpec(blk, lambda b, h: (b, 0, 2 * groups + h)),
            pl.BlockSpec(blk, lambda b, h: (b, 0, 3 * groups + h)),
            pl.BlockSpec((heads, 2 * WIN_H, PAIR), lambda b, h: (h, 0, 0)),
            pl.BlockSpec((None, slab, D_MODEL), lambda b, h: (layer, b * groups + h, 0)),
        ],
        out_specs=[pl.BlockSpec(blk, lambda b, h: (b, 0, h)),
                   pl.BlockSpec((slab, D_MODEL), lambda b, h: (b * groups + h, 0))],
        out_shape=[jax.ShapeDtypeStruct((bsz, t, D_A), bf16),
                   jax.ShapeDtypeStruct((D_MIX, D_MODEL), bf16)],
        scratch_shapes=[pltpu.VMEM((heads, 2 * WIN_H, GRID_W, PAIR), f32)],
        compiler_params=_params("arbitrary", "arbitrary"),
        name="nbr_attention",
    )(z3, z3, z3, z3, vecs, w_out)


def _attention_bias_vectors(rpb):
    half = WIN_W - 1
    pad = jnp.full((N_HEADS, 1, 2 * WIN_W - 1), MASKED, f32)
    ext = jnp.concatenate([pad, rpb.astype(f32), pad], axis=1)
    lower, upper = ext[:, :-1], ext[:, 1:]
    gap = jnp.full((N_HEADS, 2 * WIN_H, GRID_W - 2 * half - 1), MASKED, f32)
    return jnp.concatenate([lower[..., half:], gap, upper, gap, lower[..., :half]], axis=-1)


def _conv_kernel(a_ref, b_ref, g_ref, cw_ref, cb_ref, lng_ref, lnb_ref, pw_ref, pb_ref, o_ref,
                 h_scr, c_scr, *, t, tt, tm):
    zeros = jnp.zeros((CONV_PAD, D_C), f32)
    h_scr[pl.ds(0, CONV_PAD), :] = zeros
    h_scr[pl.ds(CONV_PAD + t, CONV_PAD), :] = zeros
    h_scr[pl.ds(CONV_PAD, t), :] = a_ref[0].astype(f32) * _sigmoid(b_ref[0].astype(f32))

    shift = CONV_PAD - CONV_W // 2
    n_al = (CONV_W + shift + SUBLANES - 1) // SUBLANES
    win_rows = tt + n_al * SUBLANES

    def conv_tile(ti, carry):
        t0 = pl.multiple_of(ti * tt, tt)
        for cb in range(D_C // LANES):
            cols = slice(cb * LANES, (cb + 1) * LANES)
            win = h_scr[pl.ds(t0, win_rows), cols]
            acc = None
            for b in range(SUBLANES):
                part = None
                for a in range(n_al):
                    k = SUBLANES * a + b - shift
                    if 0 <= k < CONV_W:
                        term = cw_ref[k:k + 1, cols] * win[SUBLANES * a:SUBLANES * a + tt + SUBLANES]
                        part = term if part is None else part + term
                part = part[b:b + tt]
                acc = part if acc is None else acc + part
            c_scr[pl.ds(t0, tt), cols] = acc
        return carry

    lax.fori_loop(0, t // tt, conv_tile, 0)

    def post_tile(ti, carry):
        t0 = pl.multiple_of(ti * tm, tm)
        c = c_scr[pl.ds(t0, tm), :] + cb_ref[...]
        c = _silu(_layer_norm(c, lng_ref[...], lnb_ref[...])).astype(bf16)
        y = jnp.dot(c, pw_ref[...], preferred_element_type=f32) + pb_ref[...]
        g = g_ref[0, pl.ds(t0, tm), :].astype(f32)
        o_ref[0, pl.ds(t0, tm), :] = (y * _silu(g)).astype(o_ref.dtype)
        return carry

    lax.fori_loop(0, t // tm, post_tile, 0)


def _conformer_conv(z3, conv_w, conv_b, ln_g, ln_b, pw_w, pw_b, tt=128, tm=512):
    bsz, t, _ = z3.shape
    assert t % tt == 0 and t % tm == 0
    c0 = (4 * D_A + 3 * D_B) // D_C
    blk = (1, t, D_C)
    const2 = lambda b: (0, 0)
    return pl.pallas_call(
        functools.partial(_conv_kernel, t=t, tt=tt, tm=tm),
        grid=(bsz,),
        in_specs=[
            pl.BlockSpec(blk, lambda b: (b, 0, c0)),
            pl.BlockSpec(blk, lambda b: (b, 0, c0 + 1)),
            pl.BlockSpec(blk, lambda b: (b, 0, c0 + 2)),
            pl.BlockSpec((CONV_W, D_C), const2),
            pl.BlockSpec((1, D_C), const2),
            pl.BlockSpec((1, D_C), const2),
            pl.BlockSpec((1, D_C), const2),
            pl.BlockSpec((D_C, D_C), const2),
            pl.BlockSpec((1, D_C), const2),
        ],
        out_specs=pl.BlockSpec(blk, lambda b: (b, 0, 0)),
        out_shape=jax.ShapeDtypeStruct((bsz, t, D_C), bf16),
        scratch_shapes=[pltpu.VMEM((t + 2 * CONV_PAD, D_C), f32), pltpu.VMEM((t, D_C), f32)],
        compiler_params=_params("arbitrary"),
        name="conformer_conv",
    )(z3, z3, z3, conv_w, conv_b, ln_g, ln_b, pw_w, pw_b)


def _spatial_gating(u_ref, v_ref, g_ref, lng_ref, lnb_ref, ws_ref, bs_ref, o_ref, col0, r0, nr):
    gd = D_B // B_GROUPS
    for n in range(r0 // CHUNK, (r0 + nr) // CHUNK):
        rows = slice(n * CHUNK, (n + 1) * CHUNK)
        v = _layer_norm(_gelu(v_ref[rows, :].astype(f32)), lng_ref[...], lnb_ref[...]).astype(bf16)
        for grp in range(B_GROUPS):
            cols = slice(grp * gd, (grp + 1) * gd)
            s = jnp.dot(ws_ref[grp], v[:, cols], preferred_element_type=f32) + bs_ref[:, cols]
            u = _gelu(u_ref[rows, cols].astype(f32))
            g = g_ref[rows, cols].astype(f32)
            o_ref[rows, col0 + grp * gd:col0 + (grp + 1) * gd] = (u * s * _silu(g)).astype(o_ref.dtype)


def _out_sgu_kernel(ya_ref, yc_ref, w_ref, pg_ref, x_ref, gn_ref,
                    zb_ref, lng_ref, lnb_ref, ws_ref, bs_ref, o_ref, *rest, tm, n_sub, emit_h):
    cat_scr = rest[-1]
    sub = tm // n_sub
    u_ref, v_ref, g_ref = (zb_ref.at[:, k * D_B:(k + 1) * D_B] for k in range(3))

    def project(i):
        rows = slice(i * sub, (i + 1) * sub)
        cat_scr[rows, 0:D_A] = ya_ref[rows, :]
        _spatial_gating(u_ref, v_ref, g_ref, lng_ref, lnb_ref, ws_ref, bs_ref, cat_scr, D_A, i * sub, sub)
        cat_scr[rows, D_A + D_B:D_MIX] = yc_ref[rows, :]
        return jnp.dot(cat_scr[rows, :], w_ref[...], preferred_element_type=f32)

    def finish(i, y):
        rows = slice(i * sub, (i + 1) * sub)
        x_new = x_ref[rows, :] + _rms_norm(y, pg_ref[...])
        o_ref[rows, :] = x_new
        if emit_h:
            h_ref = rest[0]
            h_ref[rows, :] = _rms_norm(x_new, gn_ref[...]).astype(h_ref.dtype)

    y_prev = project(0)
    for i in range(1, n_sub):
        y_cur = project(i)
        finish(i - 1, y_prev)
        y_prev = y_cur
    finish(n_sub - 1, y_prev)


def _out_proj_sgu(ya, yc, z2, w, post_g, x2, next_g, ln_g, ln_b, w_s, b_map, *, emit_h, tm=512, n_sub=2):
    n_tok = x2.shape[0]
    assert tm % (n_sub * CHUNK) == 0
    zb_width = 4 * D_B
    assert (4 * D_A) % zb_width == 0 and 4 * D_A + 3 * D_B <= (4 * D_A // zb_width + 1) * zb_width
    const2 = lambda i: (0, 0)
    tile = lambda width, col=0: pl.BlockSpec((tm, width), lambda i: (i, col))
    out_shape = [jax.ShapeDtypeStruct((n_tok, D_MODEL), f32)]
    out_specs = [tile(D_MODEL)]
    if emit_h:
        out_shape.append(jax.ShapeDtypeStruct((n_tok, D_MODEL), bf16))
        out_specs.append(tile(D_MODEL))
    return pl.pallas_call(
        functools.partial(_out_sgu_kernel, tm=tm, n_sub=n_sub, emit_h=emit_h),
        grid=(n_tok // tm,),
        in_specs=[
            tile(D_A), tile(D_C),
            pl.BlockSpec((D_MIX, D_MODEL), const2),
            pl.BlockSpec((1, D_MODEL), const2),
            tile(D_MODEL),
            pl.BlockSpec((1, D_MODEL), const2),
            tile(zb_width, 4 * D_A // zb_width),
            pl.BlockSpec((1, D_B), const2),
            pl.BlockSpec((1, D_B), const2),
            pl.BlockSpec((B_GROUPS, CHUNK, CHUNK), lambda i: (0, 0, 0)),
            pl.BlockSpec((CHUNK, D_B), const2),
        ],
        out_specs=out_specs,
        out_shape=out_shape,
        scratch_shapes=[pltpu.VMEM((tm, D_MIX), bf16)],
        compiler_params=_params("arbitrary"),
        name="out_proj_sgu",
    )(ya, yc, w, post_g, x2, next_g, z2, ln_g, ln_b, w_s, b_map)


def kernel(x, pre_norm_g, w_in, attn_rpb, sgu_ln_g, sgu_ln_b, sgu_w, sgu_b, conv_w, conv_b,
           conv_ln_g, conv_ln_b, conv_pw_w, conv_pw_b, w_out, post_norm_g):
    bsz, t, d = x.shape
    depth = w_in.shape[0]
    assert d == D_MODEL and t % GRID_W == 0 and t % CHUNK == 0
    n_tok = bsz * t
    x2 = x.reshape(n_tok, d)
    row = lambda a: a.reshape(1, -1).astype(f32)
    h = _pre_norm(x2, row(pre_norm_g[0]))
    for l in range(depth):
        z2 = _in_proj(h, w_in, l)
        z3 = z2.reshape(bsz, t, D_IN)
        y_a, w_out_bf = _attention(z3, _attention_bias_vectors(attn_rpb[l]), w_out, l)
        b_map = jnp.repeat(sgu_b[l].astype(f32).T, D_B // B_GROUPS, axis=1)
        y_c = _conformer_conv(z3, conv_w[l].reshape(CONV_W, D_C).astype(f32), row(conv_b[l]),
                              row(conv_ln_g[l]), row(conv_ln_b[l]), conv_pw_w[l].astype(bf16),
                              row(conv_pw_b[l]))
        last = l == depth - 1
        next_g = post_norm_g[l] if last else pre_norm_g[l + 1]
        outs = _out_proj_sgu(
            y_a.reshape(n_tok, D_A), y_c.reshape(n_tok, D_C), z2, w_out_bf,
            row(post_norm_g[l]), x2, row(next_g), row(sgu_ln_g[l]), row(sgu_ln_b[l]),
            sgu_w[l].astype(bf16), b_map, emit_h=not last)
        x2 = outs[0]
        if not last:
            h = outs[1]
    return x2.reshape(bsz, t, d)
```

```python
import functools
import math

import jax
import jax.numpy as jnp
from jax import lax
from jax.experimental import pallas as pl
from jax.experimental.pallas import tpu as pltpu

D_MODEL = 2048
GRID_W = 64
WIN_H = 8
WIN_W = 16
HEAD_DIM = 128
D_A = 1024
N_HEADS = 8
CHUNK = 128
D_B = 512
B_GROUPS = 4
D_C = 512
CONV_W = 31
D_MIX = D_A + D_B + D_C
D_IN = 4 * D_A + 3 * D_B + 3 * D_C
EPS = 1e-6
MASKED = -1e30

LANES = 128
SUBLANES = 8
VMEM_LIMIT = 56 * 1024 * 1024

Q_ROWS = 4
K_ROWS = Q_ROWS + WIN_H
PAIR = 2 * GRID_W
assert PAIR == LANES

CONV_PAD = 16

f32 = jnp.float32
bf16 = jnp.bfloat16


def _sigmoid(x):
    return 1.0 / (1.0 + jnp.exp2(x * -math.log2(math.e)))


def _silu(x):
    return x * _sigmoid(x)


def _gelu(x):
    return 0.5 * x * (1.0 + lax.erf(x * math.sqrt(0.5)))


def _layer_norm(x, g, b):
    mu = jnp.mean(x, axis=-1, keepdims=True)
    xc = x - mu
    var = jnp.mean(xc * xc, axis=-1, keepdims=True)
    return xc * lax.rsqrt(var + EPS) * g + b


def _rms_norm(x, g):
    ms = jnp.mean(x * x, axis=-1, keepdims=True)
    return x * lax.rsqrt(ms + EPS) * g


def _params(*semantics):
    return pltpu.CompilerParams(dimension_semantics=semantics, vmem_limit_bytes=VMEM_LIMIT)


def _norm_proj_kernel(x_ref, g_ref, w_ref, h_ref, z_ref, wb_scr):
    @pl.when(pl.program_id(0) == 0)
    def _():
        wb_scr[...] = w_ref[...].astype(bf16)

    h = _rms_norm(x_ref[...], g_ref[...]).astype(bf16)
    h_ref[...] = h
    z_ref[...] = jnp.dot(h, wb_scr[...], preferred_element_type=f32).astype(z_ref.dtype)


def _norm_proj(x2, g, w_in, layer, tm=1024, tn=1024):
    n_tok = x2.shape[0]
    return pl.pallas_call(
        _norm_proj_kernel,
        grid=(n_tok // tm,),
        in_specs=[pl.BlockSpec((tm, D_MODEL), lambda i: (i, 0)),
                  pl.BlockSpec((1, D_MODEL), lambda i: (0, 0)),
                  pl.BlockSpec((None, D_MODEL, tn), lambda i: (layer, 0, 0))],
        out_specs=[pl.BlockSpec((tm, D_MODEL), lambda i: (i, 0)),
                   pl.BlockSpec((tm, tn), lambda i: (i, 0))],
        out_shape=[jax.ShapeDtypeStruct((n_tok, D_MODEL), bf16),
                   jax.ShapeDtypeStruct((n_tok, tn), bf16)],
        scratch_shapes=[pltpu.VMEM((D_MODEL, tn), bf16)],
        compiler_params=_params("arbitrary"),
        name="norm_proj",
    )(x2, g, w_in)


def _in_proj_kernel(h_ref, w_ref, z_ref, wb_scr):
    @pl.when(pl.program_id(1) == 0)
    def _():
        wb_scr[...] = w_ref[...].astype(bf16)

    z_ref[...] = jnp.dot(h_ref[...], wb_scr[...], preferred_element_type=f32).astype(z_ref.dtype)


def _in_proj(h, w_in, layer, col0=0, tm=2048, tn=1024):
    n_tok = h.shape[0]
    assert col0 % tn == 0 and (D_IN - col0) % tn == 0
    return pl.pallas_call(
        _in_proj_kernel,
        grid=((D_IN - col0) // tn, n_tok // tm),
        in_specs=[
            pl.BlockSpec((tm, D_MODEL), lambda j, i: (i, 0)),
            pl.BlockSpec((None, D_MODEL, tn), lambda j, i: (layer, 0, col0 // tn + j)),
        ],
        out_specs=pl.BlockSpec((tm, tn), lambda j, i: (i, j)),
        out_shape=jax.ShapeDtypeStruct((n_tok, D_IN - col0), bf16),
        scratch_shapes=[pltpu.VMEM((D_MODEL, tn), bf16)],
        compiler_params=_params("arbitrary", "arbitrary"),
        name="in_proj",
    )(h, w_in)


def _attn_kernel(q_ref, k_ref, v_ref, g_ref, vec_ref, wo_ref, o_ref, wob_ref, bias_ref, *, rows, heads):
    wob_ref[...] = wo_ref[...].astype(bf16)
    log2e = math.log2(math.e)
    scale = HEAD_DIM ** -0.5 * log2e
    nq = Q_ROWS * GRID_W
    nk = K_ROWS * GRID_W
    w_idx = lax.broadcasted_iota(jnp.int32, (GRID_W, PAIR), 0)
    lane = lax.broadcasted_iota(jnp.int32, (GRID_W, PAIR), 1)
    left_half = lane < GRID_W
    kc = jnp.where(left_half, lane, lane - GRID_W)
    col_start = jnp.clip(w_idx - WIN_W // 2, 0, GRID_W - WIN_W)
    in_win = (kc >= col_start) & (kc < col_start + WIN_W)
    for hh in range(heads):
        for e in range(2 * WIN_H):
            vec = jnp.broadcast_to(vec_ref[hh, e:e + 1, :], (GRID_W, PAIR))
            bias = pltpu.roll(vec, 0, 1, stride=1, stride_axis=0) * log2e
            bias_ref[hh, e] = jnp.where(in_win, bias, MASKED)

    n_blk = rows // Q_ROWS
    first_key_row = lambda blk: min(max(blk * Q_ROWS - WIN_H // 2, 0), rows - K_ROWS)

    head_cols = lambda hh: slice(hh * HEAD_DIM, (hh + 1) * HEAD_DIM)
    items = [(hh, blk) for hh in range(heads) for blk in range(n_blk)]

    def scores(item):
        hh, blk = item
        q0, k0 = blk * nq, first_key_row(blk) * GRID_W
        q = (q_ref[0, q0:q0 + nq, head_cols(hh)].astype(f32) * scale).astype(bf16)
        kw = k_ref[0, k0:k0 + nk, head_cols(hh)]
        return lax.dot_general(q, kw, (((1,), (1,)), ((), ())), preferred_element_type=f32)

    s_next = scores(items[0])
    for pos, (hh, blk) in enumerate(items):
        ks = first_key_row(blk)
        q0, k0 = blk * nq, ks * GRID_W
        vw = v_ref[0, k0:k0 + nk, head_cols(hh)]
        s = s_next
        if pos + 1 < len(items):
            s_next = scores(items[pos + 1])

        p_rows = []
        for rq in range(Q_ROWS):
            r = blk * Q_ROWS + rq
            rs = min(max(r - WIN_H // 2, 0), rows - WIN_H)
            pieces = {}
            for ip in range(K_ROWS // 2):
                kr = ks + 2 * ip
                ok0, ok1 = rs <= kr < rs + WIN_H, rs <= kr + 1 < rs + WIN_H
                if not (ok0 or ok1):
                    continue
                dr0 = kr - r + (WIN_H - 1)
                sub = s[rq * GRID_W:(rq + 1) * GRID_W, ip * PAIR:(ip + 1) * PAIR]
                if ok0 and ok1:
                    bias = bias_ref[hh, dr0 + 1]
                elif ok0:
                    bias = jnp.where(left_half, bias_ref[hh, dr0 + 1], MASKED)
                else:
                    bias = jnp.where(left_half, MASKED, bias_ref[hh, dr0 + 1])
                pieces[ip] = sub + bias
            m = jnp.max(functools.reduce(jnp.maximum, pieces.values()), axis=-1, keepdims=True)
            pieces = {ip: jnp.exp2(sv - m) for ip, sv in pieces.items()}
            zero = jnp.zeros((GRID_W, PAIR), bf16)
            p_rows.append(jnp.concatenate(
                [pieces[ip].astype(bf16) if ip in pieces else zero for ip in range(K_ROWS // 2)], axis=1))
        p = jnp.concatenate(p_rows, axis=0)
        v_ext = jnp.concatenate([vw, jnp.ones((nk, HEAD_DIM), bf16)], axis=1)
        o = jnp.dot(p, v_ext, preferred_element_type=f32)
        for rq in range(Q_ROWS):
            rows_q = slice(q0 + rq * GRID_W, q0 + (rq + 1) * GRID_W)
            g = g_ref[0, rows_q, head_cols(hh)].astype(f32)
            o_rq = o[rq * GRID_W:(rq + 1) * GRID_W]
            o_rq = o_rq[:, :HEAD_DIM] * (1.0 / o_rq[:, HEAD_DIM:])
            o_ref[0, rows_q, head_cols(hh)] = (o_rq * _silu(g)).astype(o_ref.dtype)


def _attention(zq3, zr3, col0, vecs, w_out, layer, heads=2):
    bsz, t, _ = zq3.shape
    rows = t // GRID_W
    assert rows % Q_ROWS == 0 and rows >= K_ROWS and N_HEADS % heads == 0
    groups = N_HEADS // heads
    slab = D_MIX // (bsz * groups)
    assert slab * bsz * groups == D_MIX and slab % (2 * SUBLANES) == 0
    blk = (1, t, heads * HEAD_DIM)
    assert (D_A - col0) % blk[2] == 0
    k0 = (D_A - col0) // blk[2]
    return pl.pallas_call(
        functools.partial(_attn_kernel, rows=rows, heads=heads),
        grid=(bsz, groups),
        in_specs=[
            pl.BlockSpec(blk, lambda b, h: (b, 0, h)),
            pl.BlockSpec(blk, lambda b, h: (b, 0, k0 + h)),
            pl.BlockSpec(blk, lambda b, h: (b, 0, k0 + groups + h)),
            pl.BlockSpec(blk, lambda b, h: (b, 0, k0 + 2 * groups + h)),
            pl.BlockSpec((heads, 2 * WIN_H, PAIR), lambda b, h: (h, 0, 0)),
            pl.BlockSpec((None, slab, D_MODEL), lambda b, h: (layer, b * groups + h, 0)),
        ],
        out_specs=[pl.BlockSpec(blk, lambda b, h: (b, 0, h)),
                   pl.BlockSpec((slab, D_MODEL), lambda b, h: (b * groups + h, 0))],
        out_shape=[jax.ShapeDtypeStruct((bsz, t, D_A), bf16),
                   jax.ShapeDtypeStruct((D_MIX, D_MODEL), bf16)],
        scratch_shapes=[pltpu.VMEM((heads, 2 * WIN_H, GRID_W, PAIR), f32)],
        compiler_params=_params("arbitrary", "arbitrary"),
        name="nbr_attention",
    )(zq3, zr3, zr3, zr3, vecs, w_out)


def _attention_bias_vectors(rpb):
    half = WIN_W - 1
    pad = jnp.full((N_HEADS, 1, 2 * WIN_W - 1), MASKED, f32)
    ext = jnp.concatenate([pad, rpb.astype(f32), pad], axis=1)
    lower, upper = ext[:, :-1], ext[:, 1:]
    gap = jnp.full((N_HEADS, 2 * WIN_H, GRID_W - 2 * half - 1), MASKED, f32)
    return jnp.concatenate([lower[..., half:], gap, upper, gap, lower[..., :half]], axis=-1)


def _conv_kernel(a_ref, b_ref, g_ref, cw_ref, cb_ref, lng_ref, lnb_ref, pw_ref, pb_ref, o_ref,
                 h_scr, c_scr, *, t, tt, tm):
    zeros = jnp.zeros((CONV_PAD, D_C), f32)
    h_scr[pl.ds(0, CONV_PAD), :] = zeros
    h_scr[pl.ds(CONV_PAD + t, CONV_PAD), :] = zeros
    h_scr[pl.ds(CONV_PAD, t), :] = a_ref[0].astype(f32) * _sigmoid(b_ref[0].astype(f32))

    shift = CONV_PAD - CONV_W // 2
    n_al = (CONV_W + shift + SUBLANES - 1) // SUBLANES
    win_rows = tt + n_al * SUBLANES

    def conv_tile(ti, carry):
        t0 = pl.multiple_of(ti * tt, tt)
        for cb in range(D_C // LANES):
            cols = slice(cb * LANES, (cb + 1) * LANES)
            win = h_scr[pl.ds(t0, win_rows), cols]
            acc = None
            for b in range(SUBLANES):
                part = None
                for a in range(n_al):
                    k = SUBLANES * a + b - shift
                    if 0 <= k < CONV_W:
                        term = cw_ref[k:k + 1, cols] * win[SUBLANES * a:SUBLANES * a + tt + SUBLANES]
                        part = term if part is None else part + term
                part = part[b:b + tt]
                acc = part if acc is None else acc + part
            c_scr[pl.ds(t0, tt), cols] = acc
        return carry

    lax.fori_loop(0, t // tt, conv_tile, 0)

    def post_tile(ti, carry):
        t0 = pl.multiple_of(ti * tm, tm)
        c = c_scr[pl.ds(t0, tm), :] + cb_ref[...]
        c = _silu(_layer_norm(c, lng_ref[...], lnb_ref[...])).astype(bf16)
        y = jnp.dot(c, pw_ref[...], preferred_element_type=f32) + pb_ref[...]
        g = g_ref[0, pl.ds(t0, tm), :].astype(f32)
        o_ref[0, pl.ds(t0, tm), :] = (y * _silu(g)).astype(o_ref.dtype)
        return carry

    lax.fori_loop(0, t // tm, post_tile, 0)


def _conformer_conv(z3, col0, conv_w, conv_b, ln_g, ln_b, pw_w, pw_b, tt=128, tm=512):
    bsz, t, _ = z3.shape
    assert t % tt == 0 and t % tm == 0 and (4 * D_A + 3 * D_B - col0) % D_C == 0
    c0 = (4 * D_A + 3 * D_B - col0) // D_C
    blk = (1, t, D_C)
    const2 = lambda b: (0, 0)
    return pl.pallas_call(
        functools.partial(_conv_kernel, t=t, tt=tt, tm=tm),
        grid=(bsz,),
        in_specs=[
            pl.BlockSpec(blk, lambda b: (b, 0, c0)),
            pl.BlockSpec(blk, lambda b: (b, 0, c0 + 1)),
            pl.BlockSpec(blk, lambda b: (b, 0, c0 + 2)),
            pl.BlockSpec((CONV_W, D_C), const2),
            pl.BlockSpec((1, D_C), const2),
            pl.BlockSpec((1, D_C), const2),
            pl.BlockSpec((1, D_C), const2),
            pl.BlockSpec((D_C, D_C), const2),
            pl.BlockSpec((1, D_C), const2),
        ],
        out_specs=pl.BlockSpec(blk, lambda b: (b, 0, 0)),
        out_shape=jax.ShapeDtypeStruct((bsz, t, D_C), bf16),
        scratch_shapes=[pltpu.VMEM((t + 2 * CONV_PAD, D_C), f32), pltpu.VMEM((t, D_C), f32)],
        compiler_params=_params("arbitrary"),
        name="conformer_conv",
    )(z3, z3, z3, conv_w, conv_b, ln_g, ln_b, pw_w, pw_b)


def _spatial_gating(u_ref, v_ref, g_ref, lng_ref, lnb_ref, ws_ref, bs_ref, o_ref, col0, r0, nr):
    gd = D_B // B_GROUPS
    for n in range(r0 // CHUNK, (r0 + nr) // CHUNK):
        rows = slice(n * CHUNK, (n + 1) * CHUNK)
        v = _layer_norm(_gelu(v_ref[rows, :].astype(f32)), lng_ref[...], lnb_ref[...]).astype(bf16)
        for grp in range(B_GROUPS):
            cols = slice(grp * gd, (grp + 1) * gd)
            s = jnp.dot(ws_ref[grp], v[:, cols], preferred_element_type=f32) + bs_ref[:, cols]
            u = _gelu(u_ref[rows, cols].astype(f32))
            g = g_ref[rows, cols].astype(f32)
            o_ref[rows, col0 + grp * gd:col0 + (grp + 1) * gd] = (u * s * _silu(g)).astype(o_ref.dtype)


def _out_sgu_kernel(ya_ref, yc_ref, w_ref, pg_ref, x_ref, gn_ref,
                    u_ref, v_ref, g_ref, lng_ref, lnb_ref, ws_ref, bs_ref, o_ref, *rest, tm, n_sub, emit_h):
    cat_scr = rest[-1]
    sub = tm // n_sub

    def project(i):
        rows = slice(i * sub, (i + 1) * sub)
        cat_scr[rows, 0:D_A] = ya_ref[rows, :]
        _spatial_gating(u_ref, v_ref, g_ref, lng_ref, lnb_ref, ws_ref, bs_ref, cat_scr, D_A, i * sub, sub)
        cat_scr[rows, D_A + D_B:D_MIX] = yc_ref[rows, :]
        return jnp.dot(cat_scr[rows, :], w_ref[...], preferred_element_type=f32)

    def finish(i, y):
        rows = slice(i * sub, (i + 1) * sub)
        x_new = x_ref[rows, :] + _rms_norm(y, pg_ref[...])
        o_ref[rows, :] = x_new
        if emit_h:
            h_ref = rest[0]
            h_ref[rows, :] = _rms_norm(x_new, gn_ref[...]).astype(h_ref.dtype)

    y_prev = project(0)
    for i in range(1, n_sub):
        y_cur = project(i)
        finish(i - 1, y_prev)
        y_prev = y_cur
    finish(n_sub - 1, y_prev)


def _out_proj_sgu(ya, yc, z2, col0, w, post_g, x2, next_g, ln_g, ln_b, w_s, b_map, *, emit_h,
                  tm=512, n_sub=2):
    n_tok = x2.shape[0]
    assert tm % (n_sub * CHUNK) == 0 and (4 * D_A - col0) % D_B == 0
    c0 = (4 * D_A - col0) // D_B
    const2 = lambda i: (0, 0)
    tile = lambda width, col=0: pl.BlockSpec((tm, width), lambda i: (i, col))
    out_shape = [jax.ShapeDtypeStruct((n_tok, D_MODEL), f32)]
    out_specs = [tile(D_MODEL)]
    if emit_h:
        out_shape.append(jax.ShapeDtypeStruct((n_tok, D_MODEL), bf16))
        out_specs.append(tile(D_MODEL))
    return pl.pallas_call(
        functools.partial(_out_sgu_kernel, tm=tm, n_sub=n_sub, emit_h=emit_h),
        grid=(n_tok // tm,),
        in_specs=[
            tile(D_A), tile(D_C),
            pl.BlockSpec((D_MIX, D_MODEL), const2),
            pl.BlockSpec((1, D_MODEL), const2),
            tile(D_MODEL),
            pl.BlockSpec((1, D_MODEL), const2),
            tile(D_B, c0), tile(D_B, c0 + 1), tile(D_B, c0 + 2),
            pl.BlockSpec((1, D_B), const2),
            pl.BlockSpec((1, D_B), const2),
            pl.BlockSpec((B_GROUPS, CHUNK, CHUNK), lambda i: (0, 0, 0)),
            pl.BlockSpec((CHUNK, D_B), const2),
        ],
        out_specs=out_specs,
        out_shape=out_shape,
        scratch_shapes=[pltpu.VMEM((tm, D_MIX), bf16)],
        compiler_params=_params("arbitrary"),
        name="out_proj_sgu",
    )(ya, yc, w, post_g, x2, next_g, z2, z2, z2, ln_g, ln_b, w_s, b_map)


def kernel(x, pre_norm_g, w_in, attn_rpb, sgu_ln_g, sgu_ln_b, sgu_w, sgu_b, conv_w, conv_b,
           conv_ln_g, conv_ln_b, conv_pw_w, conv_pw_b, w_out, post_norm_g):
    bsz, t, d = x.shape
    depth = w_in.shape[0]
    assert d == D_MODEL and t % GRID_W == 0 and t % CHUNK == 0
    n_tok = bsz * t
    x2 = x.reshape(n_tok, d)
    row = lambda a: a.reshape(1, -1).astype(f32)
    h, zq = _norm_proj(x2, row(pre_norm_g[0]), w_in, 0, tn=D_A)
    for l in range(depth):
        col0 = D_A if l == 0 else 0
        zr = _in_proj(h, w_in, l, col0=col0)
        if l > 0:
            zq = zr
        as_seq = lambda z: z.reshape(bsz, t, z.shape[-1])
        y_a, w_out_bf = _attention(as_seq(zq), as_seq(zr), col0, _attention_bias_vectors(attn_rpb[l]),
                                   w_out, l)
        b_map = jnp.repeat(sgu_b[l].astype(f32).T, D_B // B_GROUPS, axis=1)
        y_c = _conformer_conv(as_seq(zr), col0, conv_w[l].reshape(CONV_W, D_C).astype(f32), row(conv_b[l]),
                              row(conv_ln_g[l]), row(conv_ln_b[l]), conv_pw_w[l].astype(bf16),
                              row(conv_pw_b[l]))
        last = l == depth - 1
        next_g = post_norm_g[l] if last else pre_norm_g[l + 1]
        outs = _out_proj_sgu(
            y_a.reshape(n_tok, D_A), y_c.reshape(n_tok, D_C), zr, col0, w_out_bf,
            row(post_norm_g[l]), x2, row(next_g), row(sgu_ln_g[l]), row(sgu_ln_b[l]),
            sgu_w[l].astype(bf16), b_map, emit_h=not last)
        x2 = outs[0]
        if not last:
            h = outs[1]
    return x2.reshape(bsz, t, d)
```

```python
import functools
import math

import jax
import jax.numpy as jnp
from jax import lax
from jax.experimental import pallas as pl
from jax.experimental.pallas import tpu as pltpu

D_MODEL = 2048
GRID_W = 64
WIN_H = 8
WIN_W = 16
HEAD_DIM = 128
D_A = 1024
N_HEADS = 8
CHUNK = 128
D_B = 512
B_GROUPS = 4
D_C = 512
CONV_W = 31
D_MIX = D_A + D_B + D_C
D_IN = 4 * D_A + 3 * D_B + 3 * D_C
EPS = 1e-6
MASKED = -1e30

LANES = 128
SUBLANES = 8
VMEM_LIMIT = 56 * 1024 * 1024

Q_ROWS = 4
K_ROWS = Q_ROWS + WIN_H
PAIR = 2 * GRID_W
assert PAIR == LANES

CONV_PAD = 16

f32 = jnp.float32
bf16 = jnp.bfloat16


def _sigmoid(x):
    return 1.0 / (1.0 + jnp.exp2(x * -math.log2(math.e)))


def _silu(x):
    return x * _sigmoid(x)


def _gelu(x):
    return 0.5 * x * (1.0 + lax.erf(x * math.sqrt(0.5)))


def _layer_norm(x, g, b):
    mu = jnp.mean(x, axis=-1, keepdims=True)
    xc = x - mu
    var = jnp.mean(xc * xc, axis=-1, keepdims=True)
    return xc * lax.rsqrt(var + EPS) * g + b


def _rms_norm(x, g):
    ms = jnp.mean(x * x, axis=-1, keepdims=True)
    return x * lax.rsqrt(ms + EPS) * g


def _params(*semantics):
    return pltpu.CompilerParams(dimension_semantics=semantics, vmem_limit_bytes=VMEM_LIMIT)


def _norm_proj_kernel(x_ref, g_ref, w_ref, h_ref, z_ref, wb_scr):
    @pl.when(pl.program_id(0) == 0)
    def _():
        wb_scr[...] = w_ref[...].astype(bf16)

    h = _rms_norm(x_ref[...], g_ref[...]).astype(bf16)
    h_ref[...] = h
    z_ref[...] = jnp.dot(h, wb_scr[...], preferred_element_type=f32).astype(z_ref.dtype)


def _norm_proj(x2, g, w_in, layer, tm=1024, tn=1024):
    n_tok = x2.shape[0]
    return pl.pallas_call(
        _norm_proj_kernel,
        grid=(n_tok // tm,),
        in_specs=[pl.BlockSpec((tm, D_MODEL), lambda i: (i, 0)),
                  pl.BlockSpec((1, D_MODEL), lambda i: (0, 0)),
                  pl.BlockSpec((None, D_MODEL, tn), lambda i: (layer, 0, 0))],
        out_specs=[pl.BlockSpec((tm, D_MODEL), lambda i: (i, 0)),
                   pl.BlockSpec((tm, tn), lambda i: (i, 0))],
        out_shape=[jax.ShapeDtypeStruct((n_tok, D_MODEL), bf16),
                   jax.ShapeDtypeStruct((n_tok, tn), bf16)],
        scratch_shapes=[pltpu.VMEM((D_MODEL, tn), bf16)],
        compiler_params=_params("arbitrary"),
        name="norm_proj",
    )(x2, g, w_in)


def _in_proj_kernel(h_ref, w_ref, z_ref, wb_scr):
    @pl.when(pl.program_id(1) == 0)
    def _():
        wb_scr[...] = w_ref[...].astype(bf16)

    z_ref[...] = jnp.dot(h_ref[...], wb_scr[...], preferred_element_type=f32).astype(z_ref.dtype)


def _in_proj(h, w_in, layer, col0=0, tm=2048, tn=1024):
    n_tok = h.shape[0]
    assert col0 % tn == 0 and (D_IN - col0) % tn == 0
    return pl.pallas_call(
        _in_proj_kernel,
        grid=((D_IN - col0) // tn, n_tok // tm),
        in_specs=[
            pl.BlockSpec((tm, D_MODEL), lambda j, i: (i, 0)),
            pl.BlockSpec((None, D_MODEL, tn), lambda j, i: (layer, 0, col0 // tn + j)),
        ],
        out_specs=pl.BlockSpec((tm, tn), lambda j, i: (i, j)),
        out_shape=jax.ShapeDtypeStruct((n_tok, D_IN - col0), bf16),
        scratch_shapes=[pltpu.VMEM((D_MODEL, tn), bf16)],
        compiler_params=_params("arbitrary", "arbitrary"),
        name="in_proj",
    )(h, w_in)


def _attn_kernel(q_ref, k_ref, v_ref, g_ref, vec_ref, wo_ref, o_ref, wob_ref, bias_ref, *, rows, heads):
    wob_ref[...] = wo_ref[...].astype(bf16)
    log2e = math.log2(math.e)
    scale = HEAD_DIM ** -0.5 * log2e
    nq = Q_ROWS * GRID_W
    nk = K_ROWS * GRID_W
    w_idx = lax.broadcasted_iota(jnp.int32, (GRID_W, PAIR), 0)
    lane = lax.broadcasted_iota(jnp.int32, (GRID_W, PAIR), 1)
    left_half = lane < GRID_W
    kc = jnp.where(left_half, lane, lane - GRID_W)
    col_start = jnp.clip(w_idx - WIN_W // 2, 0, GRID_W - WIN_W)
    in_win = (kc >= col_start) & (kc < col_start + WIN_W)
    for hh in range(heads):
        for e in range(2 * WIN_H):
            vec = jnp.broadcast_to(vec_ref[hh, e:e + 1, :], (GRID_W, PAIR))
            bias = pltpu.roll(vec, 0, 1, stride=1, stride_axis=0) * log2e
            bias_ref[hh, e] = jnp.where(in_win, bias, MASKED)

    n_blk = rows // Q_ROWS
    first_key_row = lambda blk: min(max(blk * Q_ROWS - WIN_H // 2, 0), rows - K_ROWS)

    head_cols = lambda hh: slice(hh * HEAD_DIM, (hh + 1) * HEAD_DIM)
    items = [(hh, blk) for hh in range(heads) for blk in range(n_blk)]

    def scores(item):
        hh, blk = item
        q0, k0 = blk * nq, first_key_row(blk) * GRID_W
        q = (q_ref[0, q0:q0 + nq, head_cols(hh)].astype(f32) * scale).astype(bf16)
        kw = k_ref[0, k0:k0 + nk, head_cols(hh)]
        return lax.dot_general(q, kw, (((1,), (1,)), ((), ())), preferred_element_type=f32)

    s_next = scores(items[0])
    for pos, (hh, blk) in enumerate(items):
        ks = first_key_row(blk)
        q0, k0 = blk * nq, ks * GRID_W
        vw = v_ref[0, k0:k0 + nk, head_cols(hh)]
        s = s_next
        if pos + 1 < len(items):
            s_next = scores(items[pos + 1])

        p_rows = []
        for rq in range(Q_ROWS):
            r = blk * Q_ROWS + rq
            rs = min(max(r - WIN_H // 2, 0), rows - WIN_H)
            pieces = {}
            for ip in range(K_ROWS // 2):
                kr = ks + 2 * ip
                ok0, ok1 = rs <= kr < rs + WIN_H, rs <= kr + 1 < rs + WIN_H
                if not (ok0 or ok1):
                    continue
                dr0 = kr - r + (WIN_H - 1)
                sub = s[rq * GRID_W:(rq + 1) * GRID_W, ip * PAIR:(ip + 1) * PAIR]
                if ok0 and ok1:
                    bias = bias_ref[hh, dr0 + 1]
                elif ok0:
                    bias = jnp.where(left_half, bias_ref[hh, dr0 + 1], MASKED)
                else:
                    bias = jnp.where(left_half, MASKED, bias_ref[hh, dr0 + 1])
                pieces[ip] = sub + bias
            m = jnp.max(functools.reduce(jnp.maximum, pieces.values()), axis=-1, keepdims=True)
            pieces = {ip: jnp.exp2(sv - m) for ip, sv in pieces.items()}
            zero = jnp.zeros((GRID_W, PAIR), bf16)
            p_rows.append(jnp.concatenate(
                [pieces[ip].astype(bf16) if ip in pieces else zero for ip in range(K_ROWS // 2)], axis=1))
        p = jnp.concatenate(p_rows, axis=0)
        v_ext = jnp.concatenate([vw, jnp.ones((nk, HEAD_DIM), bf16)], axis=1)
        o = jnp.dot(p, v_ext, preferred_element_type=f32)
        for rq in range(Q_ROWS):
            rows_q = slice(q0 + rq * GRID_W, q0 + (rq + 1) * GRID_W)
            g = g_ref[0, rows_q, head_cols(hh)].astype(f32)
            o_rq = o[rq * GRID_W:(rq + 1) * GRID_W]
            o_rq = o_rq[:, :HEAD_DIM] * (1.0 / o_rq[:, HEAD_DIM:])
            o_ref[0, rows_q, head_cols(hh)] = (o_rq * _silu(g)).astype(o_ref.dtype)


def _attention(zq3, zr3, col0, vecs, w_out, layer, heads=2):
    bsz, t, _ = zq3.shape
    rows = t // GRID_W
    assert rows % Q_ROWS == 0 and rows >= K_ROWS and N_HEADS % heads == 0
    groups = N_HEADS // heads
    slab = D_MIX // (bsz * groups)
    assert slab * bsz * groups == D_MIX and slab % (2 * SUBLANES) == 0
    blk = (1, t, heads * HEAD_DIM)
    assert (D_A - col0) % blk[2] == 0
    k0 = (D_A - col0) // blk[2]
    return pl.pallas_call(
        functools.partial(_attn_kernel, rows=rows, heads=heads),
        grid=(bsz, groups),
        in_specs=[
            pl.BlockSpec(blk, lambda b, h: (b, 0, h)),
            pl.BlockSpec(blk, lambda b, h: (b, 0, k0 + h)),
            pl.BlockSpec(blk, lambda b, h: (b, 0, k0 + groups + h)),
            pl.BlockSpec(blk, lambda b, h: (b, 0, k0 + 2 * groups + h)),
            pl.BlockSpec((heads, 2 * WIN_H, PAIR), lambda b, h: (h, 0, 0)),
            pl.BlockSpec((None, slab, D_MODEL), lambda b, h: (layer, b * groups + h, 0)),
        ],
        out_specs=[pl.BlockSpec(blk, lambda b, h: (b, 0, h)),
                   pl.BlockSpec((slab, D_MODEL), lambda b, h: (b * groups + h, 0))],
        out_shape=[jax.ShapeDtypeStruct((bsz, t, D_A), bf16),
                   jax.ShapeDtypeStruct((D_MIX, D_MODEL), bf16)],
        scratch_shapes=[pltpu.VMEM((heads, 2 * WIN_H, GRID_W, PAIR), f32)],
        compiler_params=_params("arbitrary", "arbitrary"),
        name="nbr_attention",
    )(zq3, zr3, zr3, zr3, vecs, w_out)


def _attention_bias_vectors(rpb):
    half = WIN_W - 1
    pad = jnp.full((N_HEADS, 1, 2 * WIN_W - 1), MASKED, f32)
    ext = jnp.concatenate([pad, rpb.astype(f32), pad], axis=1)
    lower, upper = ext[:, :-1], ext[:, 1:]
    gap = jnp.full((N_HEADS, 2 * WIN_H, GRID_W - 2 * half - 1), MASKED, f32)
    return jnp.concatenate([lower[..., half:], gap, upper, gap, lower[..., :half]], axis=-1)


def _conv_kernel(a_ref, b_ref, g_ref, cw_ref, cb_ref, lng_ref, lnb_ref, pw_ref, pb_ref, o_ref,
                 h_scr, c_scr, *, t, tt, tm):
    zeros = jnp.zeros((CONV_PAD, D_C), f32)
    h_scr[pl.ds(0, CONV_PAD), :] = zeros
    h_scr[pl.ds(CONV_PAD + t, CONV_PAD), :] = zeros
    h_scr[pl.ds(CONV_PAD, t), :] = a_ref[0].astype(f32) * _sigmoid(b_ref[0].astype(f32))

    shift = CONV_PAD - CONV_W // 2
    n_al = (CONV_W + shift + SUBLANES - 1) // SUBLANES
    win_rows = tt + n_al * SUBLANES

    def conv_tile(ti, carry):
        t0 = pl.multiple_of(ti * tt, tt)
        for cb in range(D_C // LANES):
            cols = slice(cb * LANES, (cb + 1) * LANES)
            win = h_scr[pl.ds(t0, win_rows), cols]
            acc = None
            for b in range(SUBLANES):
                part = None
                for a in range(n_al):
                    k = SUBLANES * a + b - shift
                    if 0 <= k < CONV_W:
                        term = cw_ref[k:k + 1, cols] * win[SUBLANES * a:SUBLANES * a + tt + SUBLANES]
                        part = term if part is None else part + term
                part = part[b:b + tt]
                acc = part if acc is None else acc + part
            c_scr[pl.ds(t0, tt), cols] = acc
        return carry

    lax.fori_loop(0, t // tt, conv_tile, 0)

    def post_tile(ti, carry):
        t0 = pl.multiple_of(ti * tm, tm)
        c = c_scr[pl.ds(t0, tm), :] + cb_ref[...]
        c = _silu(_layer_norm(c, lng_ref[...], lnb_ref[...])).astype(bf16)
        y = jnp.dot(c, pw_ref[...], preferred_element_type=f32) + pb_ref[...]
        g = g_ref[0, pl.ds(t0, tm), :].astype(f32)
        o_ref[0, pl.ds(t0, tm), :] = (y * _silu(g)).astype(o_ref.dtype)
        return carry

    lax.fori_loop(0, t // tm, post_tile, 0)


def _conformer_conv(z3, col0, conv_w, conv_b, ln_g, ln_b, pw_w, pw_b, tt=128, tm=1024):
    bsz, t, _ = z3.shape
    assert t % tt == 0 and t % tm == 0 and (4 * D_A + 3 * D_B - col0) % D_C == 0
    c0 = (4 * D_A + 3 * D_B - col0) // D_C
    blk = (1, t, D_C)
    const2 = lambda b: (0, 0)
    return pl.pallas_call(
        functools.partial(_conv_kernel, t=t, tt=tt, tm=tm),
        grid=(bsz,),
        in_specs=[
            pl.BlockSpec(blk, lambda b: (b, 0, c0)),
            pl.BlockSpec(blk, lambda b: (b, 0, c0 + 1)),
            pl.BlockSpec(blk, lambda b: (b, 0, c0 + 2)),
            pl.BlockSpec((CONV_W, D_C), const2),
            pl.BlockSpec((1, D_C), const2),
            pl.BlockSpec((1, D_C), const2),
            pl.BlockSpec((1, D_C), const2),
            pl.BlockSpec((D_C, D_C), const2),
            pl.BlockSpec((1, D_C), const2),
        ],
        out_specs=pl.BlockSpec(blk, lambda b: (b, 0, 0)),
        out_shape=jax.ShapeDtypeStruct((bsz, t, D_C), bf16),
        scratch_shapes=[pltpu.VMEM((t + 2 * CONV_PAD, D_C), f32), pltpu.VMEM((t, D_C), f32)],
        compiler_params=_params("arbitrary"),
        name="conformer_conv",
    )(z3, z3, z3, conv_w, conv_b, ln_g, ln_b, pw_w, pw_b)


def _spatial_gating(u_ref, v_ref, g_ref, lng_ref, lnb_ref, ws_ref, bs_ref, o_ref, col0, r0, nr):
    gd = D_B // B_GROUPS
    for n in range(r0 // CHUNK, (r0 + nr) // CHUNK):
        rows = slice(n * CHUNK, (n + 1) * CHUNK)
        v = _layer_norm(_gelu(v_ref[rows, :].astype(f32)), lng_ref[...], lnb_ref[...]).astype(bf16)
        for grp in range(B_GROUPS):
            cols = slice(grp * gd, (grp + 1) * gd)
            s = jnp.dot(ws_ref[grp], v[:, cols], preferred_element_type=f32) + bs_ref[:, cols]
            u = _gelu(u_ref[rows, cols].astype(f32))
            g = g_ref[rows, cols].astype(f32)
            o_ref[rows, col0 + grp * gd:col0 + (grp + 1) * gd] = (u * s * _silu(g)).astype(o_ref.dtype)


def _out_sgu_kernel(ya_ref, yc_ref, w_ref, pg_ref, x_ref, gn_ref,
                    u_ref, v_ref, g_ref, lng_ref, lnb_ref, ws_ref, bs_ref, o_ref, *rest, tm, n_sub, emit_h):
    cat_scr = rest[-1]
    sub = tm // n_sub

    def project(i):
        rows = slice(i * sub, (i + 1) * sub)
        cat_scr[rows, 0:D_A] = ya_ref[rows, :]
        _spatial_gating(u_ref, v_ref, g_ref, lng_ref, lnb_ref, ws_ref, bs_ref, cat_scr, D_A, i * sub, sub)
        cat_scr[rows, D_A + D_B:D_MIX] = yc_ref[rows, :]
        return jnp.dot(cat_scr[rows, :], w_ref[...], preferred_element_type=f32)

    def finish(i, y):
        rows = slice(i * sub, (i + 1) * sub)
        x_new = x_ref[rows, :] + _rms_norm(y, pg_ref[...])
        o_ref[rows, :] = x_new
        if emit_h:
            h_ref = rest[0]
            h_ref[rows, :] = _rms_norm(x_new, gn_ref[...]).astype(h_ref.dtype)

    y_prev = project(0)
    for i in range(1, n_sub):
        y_cur = project(i)
        finish(i - 1, y_prev)
        y_prev = y_cur
    finish(n_sub - 1, y_prev)


def _out_proj_sgu(ya, yc, z2, col0, w, post_g, x2, next_g, ln_g, ln_b, w_s, b_map, *, emit_h,
                  tm=512, n_sub=2):
    n_tok = x2.shape[0]
    assert tm % (n_sub * CHUNK) == 0 and (4 * D_A - col0) % D_B == 0
    c0 = (4 * D_A - col0) // D_B
    const2 = lambda i: (0, 0)
    tile = lambda width, col=0: pl.BlockSpec((tm, width), lambda i: (i, col))
    out_shape = [jax.ShapeDtypeStruct((n_tok, D_MODEL), f32)]
    out_specs = [tile(D_MODEL)]
    if emit_h:
        out_shape.append(jax.ShapeDtypeStruct((n_tok, D_MODEL), bf16))
        out_specs.append(tile(D_MODEL))
    return pl.pallas_call(
        functools.partial(_out_sgu_kernel, tm=tm, n_sub=n_sub, emit_h=emit_h),
        grid=(n_tok // tm,),
        in_specs=[
            tile(D_A), tile(D_C),
            pl.BlockSpec((D_MIX, D_MODEL), const2),
            pl.BlockSpec((1, D_MODEL), const2),
            tile(D_MODEL),
            pl.BlockSpec((1, D_MODEL), const2),
            tile(D_B, c0), tile(D_B, c0 + 1), tile(D_B, c0 + 2),
            pl.BlockSpec((1, D_B), const2),
            pl.BlockSpec((1, D_B), const2),
            pl.BlockSpec((B_GROUPS, CHUNK, CHUNK), lambda i: (0, 0, 0)),
            pl.BlockSpec((CHUNK, D_B), const2),
        ],
        out_specs=out_specs,
        out_shape=out_shape,
        scratch_shapes=[pltpu.VMEM((tm, D_MIX), bf16)],
        compiler_params=_params("arbitrary"),
        name="out_proj_sgu",
    )(ya, yc, w, post_g, x2, next_g, z2, z2, z2, ln_g, ln_b, w_s, b_map)


def kernel(x, pre_norm_g, w_in, attn_rpb, sgu_ln_g, sgu_ln_b, sgu_w, sgu_b, conv_w, conv_b,
           conv_ln_g, conv_ln_b, conv_pw_w, conv_pw_b, w_out, post_norm_g):
    bsz, t, d = x.shape
    depth = w_in.shape[0]
    assert d == D_MODEL and t % GRID_W == 0 and t % CHUNK == 0
    n_tok = bsz * t
    x2 = x.reshape(n_tok, d)
    row = lambda a: a.reshape(1, -1).astype(f32)
    h, zq = _norm_proj(x2, row(pre_norm_g[0]), w_in, 0, tn=D_A)
    for l in range(depth):
        col0 = D_A if l == 0 else 0
        zr = _in_proj(h, w_in, l, col0=col0)
        if l > 0:
            zq = zr
        as_seq = lambda z: z.reshape(bsz, t, z.shape[-1])
        y_a, w_out_bf = _attention(as_seq(zq), as_seq(zr), col0, _attention_bias_vectors(attn_rpb[l]),
                                   w_out, l)
        b_map = jnp.repeat(sgu_b[l].astype(f32).T, D_B // B_GROUPS, axis=1)
        y_c = _conformer_conv(as_seq(zr), col0, conv_w[l].reshape(CONV_W, D_C).astype(f32), row(conv_b[l]),
                              row(conv_ln_g[l]), row(conv_ln_b[l]), conv_pw_w[l].astype(bf16),
                              row(conv_pw_b[l]))
        last = l == depth - 1
        next_g = post_norm_g[l] if last else pre_norm_g[l + 1]
        outs = _out_proj_sgu(
            y_a.reshape(n_tok, D_A), y_c.reshape(n_tok, D_C), zr, col0, w_out_bf,
            row(post_norm_g[l]), x2, row(next_g), row(sgu_ln_g[l]), row(sgu_ln_b[l]),
            sgu_w[l].astype(bf16), b_map, emit_h=not last)
        x2 = outs[0]
        if not last:
            h = outs[1]
    return x2.reshape(bsz, t, d)
```

```python
import functools
import math

import jax
import jax.numpy as jnp
from jax import lax
from jax.experimental import pallas as pl
from jax.experimental.pallas import tpu as pltpu

D_MODEL = 2048
GRID_W = 64
WIN_H = 8
WIN_W = 16
HEAD_DIM = 128
D_A = 1024
N_HEADS = 8
CHUNK = 128
D_B = 512
B_GROUPS = 4
D_C = 512
CONV_W = 31
D_MIX = D_A + D_B + D_C
D_IN = 4 * D_A + 3 * D_B + 3 * D_C
EPS = 1e-6
MASKED = -1e30

LANES = 128
SUBLANES = 8
VMEM_LIMIT = 56 * 1024 * 1024

Q_ROWS = 4
K_ROWS = Q_ROWS + WIN_H
PAIR = 2 * GRID_W
assert PAIR == LANES

CONV_PAD = 16

f32 = jnp.float32
bf16 = jnp.bfloat16


def _sigmoid(x):
    return 1.0 / (1.0 + jnp.exp2(x * -math.log2(math.e)))


def _silu(x):
    return x * _sigmoid(x)


def _gelu(x):
    return 0.5 * x * (1.0 + lax.erf(x * math.sqrt(0.5)))


def _layer_norm(x, g, b):
    mu = jnp.mean(x, axis=-1, keepdims=True)
    xc = x - mu
    var = jnp.mean(xc * xc, axis=-1, keepdims=True)
    return xc * lax.rsqrt(var + EPS) * g + b


def _rms_norm(x, g):
    ms = jnp.mean(x * x, axis=-1, keepdims=True)
    return x * lax.rsqrt(ms + EPS) * g


def _params(*semantics):
    return pltpu.CompilerParams(dimension_semantics=semantics, vmem_limit_bytes=VMEM_LIMIT)


def _norm_proj_kernel(x_ref, g_ref, w_ref, h_ref, z_ref, wb_scr):
    @pl.when(pl.program_id(0) == 0)
    def _():
        wb_scr[...] = w_ref[...].astype(bf16)

    h = _rms_norm(x_ref[...], g_ref[...]).astype(bf16)
    h_ref[...] = h
    z_ref[...] = jnp.dot(h, wb_scr[...], preferred_element_type=f32).astype(z_ref.dtype)


def _norm_proj(x2, g, w_in, layer, tm=1024, tn=1024):
    n_tok = x2.shape[0]
    return pl.pallas_call(
        _norm_proj_kernel,
        grid=(n_tok // tm,),
        in_specs=[pl.BlockSpec((tm, D_MODEL), lambda i: (i, 0)),
                  pl.BlockSpec((None, 1, D_MODEL), lambda i: (layer, 0, 0)),
                  pl.BlockSpec((None, D_MODEL, tn), lambda i: (layer, 0, 0))],
        out_specs=[pl.BlockSpec((tm, D_MODEL), lambda i: (i, 0)),
                   pl.BlockSpec((tm, tn), lambda i: (i, 0))],
        out_shape=[jax.ShapeDtypeStruct((n_tok, D_MODEL), bf16),
                   jax.ShapeDtypeStruct((n_tok, tn), bf16)],
        scratch_shapes=[pltpu.VMEM((D_MODEL, tn), bf16)],
        compiler_params=_params("arbitrary"),
        name="norm_proj",
    )(x2, g, w_in)


def _in_proj_kernel(h_ref, w_ref, z_ref, wb_scr):
    @pl.when(pl.program_id(1) == 0)
    def _():
        wb_scr[...] = w_ref[...].astype(bf16)

    z_ref[...] = jnp.dot(h_ref[...], wb_scr[...], preferred_element_type=f32).astype(z_ref.dtype)


def _in_proj(h, w_in, layer, col0=0, tm=2048, tn=1024):
    n_tok = h.shape[0]
    assert col0 % tn == 0 and (D_IN - col0) % tn == 0
    return pl.pallas_call(
        _in_proj_kernel,
        grid=((D_IN - col0) // tn, n_tok // tm),
        in_specs=[
            pl.BlockSpec((tm, D_MODEL), lambda j, i: (i, 0)),
            pl.BlockSpec((None, D_MODEL, tn), lambda j, i: (layer, 0, col0 // tn + j)),
        ],
        out_specs=pl.BlockSpec((tm, tn), lambda j, i: (i, j)),
        out_shape=jax.ShapeDtypeStruct((n_tok, D_IN - col0), bf16),
        scratch_shapes=[pltpu.VMEM((D_MODEL, tn), bf16)],
        compiler_params=_params("arbitrary", "arbitrary"),
        name="in_proj",
    )(h, w_in)


def _attn_kernel(q_ref, k_ref, v_ref, g_ref, vec_ref, wo_ref, o_ref, wob_ref, bias_ref, *, rows, heads):
    wob_ref[...] = wo_ref[...].astype(bf16)
    log2e = math.log2(math.e)
    scale = HEAD_DIM ** -0.5 * log2e
    nq = Q_ROWS * GRID_W
    nk = K_ROWS * GRID_W
    w_idx = lax.broadcasted_iota(jnp.int32, (GRID_W, PAIR), 0)
    lane = lax.broadcasted_iota(jnp.int32, (GRID_W, PAIR), 1)
    left_half = lane < GRID_W
    kc = jnp.where(left_half, lane, lane - GRID_W)
    col_start = jnp.clip(w_idx - WIN_W // 2, 0, GRID_W - WIN_W)
    in_win = (kc >= col_start) & (kc < col_start + WIN_W)
    for hh in range(heads):
        for e in range(2 * WIN_H):
            vec = jnp.broadcast_to(vec_ref[hh, e:e + 1, :], (GRID_W, PAIR))
            bias = pltpu.roll(vec, 0, 1, stride=1, stride_axis=0) * log2e
            bias_ref[hh, e] = jnp.where(in_win, bias, MASKED)

    n_blk = rows // Q_ROWS
    first_key_row = lambda blk: min(max(blk * Q_ROWS - WIN_H // 2, 0), rows - K_ROWS)

    head_cols = lambda hh: slice(hh * HEAD_DIM, (hh + 1) * HEAD_DIM)
    items = [(hh, blk) for hh in range(heads) for blk in range(n_blk)]

    def scores(item):
        hh, blk = item
        q0, k0 = blk * nq, first_key_row(blk) * GRID_W
        q = (q_ref[0, q0:q0 + nq, head_cols(hh)].astype(f32) * scale).astype(bf16)
        kw = k_ref[0, k0:k0 + nk, head_cols(hh)]
        return lax.dot_general(q, kw, (((1,), (1,)), ((), ())), preferred_element_type=f32)

    s_next = scores(items[0])
    for pos, (hh, blk) in enumerate(items):
        ks = first_key_row(blk)
        q0, k0 = blk * nq, ks * GRID_W
        vw = v_ref[0, k0:k0 + nk, head_cols(hh)]
        s = s_next
        if pos + 1 < len(items):
            s_next = scores(items[pos + 1])

        p_rows = []
        for rq in range(Q_ROWS):
            r = blk * Q_ROWS + rq
            rs = min(max(r - WIN_H // 2, 0), rows - WIN_H)
            pieces = {}
            for ip in range(K_ROWS // 2):
                kr = ks + 2 * ip
                ok0, ok1 = rs <= kr < rs + WIN_H, rs <= kr + 1 < rs + WIN_H
                if not (ok0 or ok1):
                    continue
                dr0 = kr - r + (WIN_H - 1)
                sub = s[rq * GRID_W:(rq + 1) * GRID_W, ip * PAIR:(ip + 1) * PAIR]
                if ok0 and ok1:
                    bias = bias_ref[hh, dr0 + 1]
                elif ok0:
                    bias = jnp.where(left_half, bias_ref[hh, dr0 + 1], MASKED)
                else:
                    bias = jnp.where(left_half, MASKED, bias_ref[hh, dr0 + 1])
                pieces[ip] = sub + bias
            m = jnp.max(functools.reduce(jnp.maximum, pieces.values()), axis=-1, keepdims=True)
            pieces = {ip: jnp.exp2(sv - m) for ip, sv in pieces.items()}
            zero = jnp.zeros((GRID_W, PAIR), bf16)
            p_rows.append(jnp.concatenate(
                [pieces[ip].astype(bf16) if ip in pieces else zero for ip in range(K_ROWS // 2)], axis=1))
        p = jnp.concatenate(p_rows, axis=0)
        v_ext = jnp.concatenate([vw, jnp.ones((nk, HEAD_DIM), bf16)], axis=1)
        o = jnp.dot(p, v_ext, preferred_element_type=f32)
        for rq in range(Q_ROWS):
            rows_q = slice(q0 + rq * GRID_W, q0 + (rq + 1) * GRID_W)
            g = g_ref[0, rows_q, head_cols(hh)].astype(f32)
            o_rq = o[rq * GRID_W:(rq + 1) * GRID_W]
            o_rq = o_rq[:, :HEAD_DIM] * (1.0 / o_rq[:, HEAD_DIM:])
            o_ref[0, rows_q, head_cols(hh)] = (o_rq * _silu(g)).astype(o_ref.dtype)


def _attention(zq3, zr3, col0, vecs, w_out, layer, heads=2):
    bsz, t, _ = zq3.shape
    rows = t // GRID_W
    assert rows % Q_ROWS == 0 and rows >= K_ROWS and N_HEADS % heads == 0
    groups = N_HEADS // heads
    slab = D_MIX // (bsz * groups)
    assert slab * bsz * groups == D_MIX and slab % (2 * SUBLANES) == 0
    blk = (1, t, heads * HEAD_DIM)
    assert (D_A - col0) % blk[2] == 0
    k0 = (D_A - col0) // blk[2]
    return pl.pallas_call(
        functools.partial(_attn_kernel, rows=rows, heads=heads),
        grid=(bsz, groups),
        in_specs=[
            pl.BlockSpec(blk, lambda b, h: (b, 0, h)),
            pl.BlockSpec(blk, lambda b, h: (b, 0, k0 + h)),
            pl.BlockSpec(blk, lambda b, h: (b, 0, k0 + groups + h)),
            pl.BlockSpec(blk, lambda b, h: (b, 0, k0 + 2 * groups + h)),
            pl.BlockSpec((None, heads, 2 * WIN_H, PAIR), lambda b, h: (layer, h, 0, 0)),
            pl.BlockSpec((None, slab, D_MODEL), lambda b, h: (layer, b * groups + h, 0)),
        ],
        out_specs=[pl.BlockSpec(blk, lambda b, h: (b, 0, h)),
                   pl.BlockSpec((slab, D_MODEL), lambda b, h: (b * groups + h, 0))],
        out_shape=[jax.ShapeDtypeStruct((bsz, t, D_A), bf16),
                   jax.ShapeDtypeStruct((D_MIX, D_MODEL), bf16)],
        scratch_shapes=[pltpu.VMEM((heads, 2 * WIN_H, GRID_W, PAIR), f32)],
        compiler_params=_params("arbitrary", "arbitrary"),
        name="nbr_attention",
    )(zq3, zr3, zr3, zr3, vecs, w_out)


def _attention_bias_vectors(rpb):
    half = WIN_W - 1
    lead = rpb.shape[:-2]
    pad = jnp.full(lead + (1, 2 * WIN_W - 1), MASKED, f32)
    ext = jnp.concatenate([pad, rpb.astype(f32), pad], axis=-2)
    lower, upper = ext[..., :-1, :], ext[..., 1:, :]
    gap = jnp.full(lead + (2 * WIN_H, GRID_W - 2 * half - 1), MASKED, f32)
    return jnp.concatenate([lower[..., half:], gap, upper, gap, lower[..., :half]], axis=-1)


def _conv_kernel(a_ref, b_ref, g_ref, cw_ref, cb_ref, lng_ref, lnb_ref, pw_ref, pb_ref, o_ref,
                 h_scr, c_scr, *, t, tt, tm):
    zeros = jnp.zeros((CONV_PAD, D_C), f32)
    h_scr[pl.ds(0, CONV_PAD), :] = zeros
    h_scr[pl.ds(CONV_PAD + t, CONV_PAD), :] = zeros
    h_scr[pl.ds(CONV_PAD, t), :] = a_ref[0].astype(f32) * _sigmoid(b_ref[0].astype(f32))

    shift = CONV_PAD - CONV_W // 2
    n_al = (CONV_W + shift + SUBLANES - 1) // SUBLANES
    win_rows = tt + n_al * SUBLANES

    def conv_tile(ti, carry):
        t0 = pl.multiple_of(ti * tt, tt)
        for cb in range(D_C // LANES):
            cols = slice(cb * LANES, (cb + 1) * LANES)
            win = h_scr[pl.ds(t0, win_rows), cols]
            acc = None
            for b in range(SUBLANES):
                part = None
                for a in range(n_al):
                    k = SUBLANES * a + b - shift
                    if 0 <= k < CONV_W:
                        term = cw_ref[k:k + 1, cols] * win[SUBLANES * a:SUBLANES * a + tt + SUBLANES]
                        part = term if part is None else part + term
                part = part[b:b + tt]
                acc = part if acc is None else acc + part
            c_scr[pl.ds(t0, tt), cols] = acc
        return carry

    lax.fori_loop(0, t // tt, conv_tile, 0)

    def post_tile(ti, carry):
        t0 = pl.multiple_of(ti * tm, tm)
        c = c_scr[pl.ds(t0, tm), :] + cb_ref[...]
        c = _silu(_layer_norm(c, lng_ref[...], lnb_ref[...])).astype(bf16)
        y = jnp.dot(c, pw_ref[...].astype(bf16), preferred_element_type=f32) + pb_ref[...]
        g = g_ref[0, pl.ds(t0, tm), :].astype(f32)
        o_ref[0, pl.ds(t0, tm), :] = (y * _silu(g)).astype(o_ref.dtype)
        return carry

    lax.fori_loop(0, t // tm, post_tile, 0)


def _conformer_conv(z3, col0, layer, conv_w, conv_b, ln_g, ln_b, pw_w, pw_b, tt=128, tm=1024):
    bsz, t, _ = z3.shape
    assert t % tt == 0 and t % tm == 0 and (4 * D_A + 3 * D_B - col0) % D_C == 0
    c0 = (4 * D_A + 3 * D_B - col0) // D_C
    blk = (1, t, D_C)
    of_layer = lambda *dims: pl.BlockSpec((None,) + dims, lambda b: (layer,) + (0,) * len(dims))
    return pl.pallas_call(
        functools.partial(_conv_kernel, t=t, tt=tt, tm=tm),
        grid=(bsz,),
        in_specs=[
            pl.BlockSpec(blk, lambda b: (b, 0, c0)),
            pl.BlockSpec(blk, lambda b: (b, 0, c0 + 1)),
            pl.BlockSpec(blk, lambda b: (b, 0, c0 + 2)),
            of_layer(CONV_W, D_C), of_layer(1, D_C), of_layer(1, D_C), of_layer(1, D_C),
            of_layer(D_C, D_C), of_layer(1, D_C),
        ],
        out_specs=pl.BlockSpec(blk, lambda b: (b, 0, 0)),
        out_shape=jax.ShapeDtypeStruct((bsz, t, D_C), bf16),
        scratch_shapes=[pltpu.VMEM((t + 2 * CONV_PAD, D_C), f32), pltpu.VMEM((t, D_C), f32)],
        compiler_params=_params("arbitrary"),
        name="conformer_conv",
    )(z3, z3, z3, conv_w, conv_b, ln_g, ln_b, pw_w, pw_b)


def _spatial_gating(u_ref, v_ref, g_ref, lng_ref, lnb_ref, ws_ref, bs_ref, o_ref, col0, r0, nr):
    gd = D_B // B_GROUPS
    for n in range(r0 // CHUNK, (r0 + nr) // CHUNK):
        rows = slice(n * CHUNK, (n + 1) * CHUNK)
        v = _layer_norm(_gelu(v_ref[rows, :].astype(f32)), lng_ref[...], lnb_ref[...]).astype(bf16)
        for grp in range(B_GROUPS):
            cols = slice(grp * gd, (grp + 1) * gd)
            s = jnp.dot(ws_ref[grp].astype(bf16), v[:, cols], preferred_element_type=f32) + bs_ref[:, cols]
            u = _gelu(u_ref[rows, cols].astype(f32))
            g = g_ref[rows, cols].astype(f32)
            o_ref[rows, col0 + grp * gd:col0 + (grp + 1) * gd] = (u * s * _silu(g)).astype(o_ref.dtype)


def _out_sgu_kernel(ya_ref, yc_ref, w_ref, pg_ref, x_ref, gn_ref,
                    u_ref, v_ref, g_ref, lng_ref, lnb_ref, ws_ref, bs_ref, o_ref, *rest, tm, n_sub, emit_h):
    cat_scr = rest[-1]
    sub = tm // n_sub

    def project(i):
        rows = slice(i * sub, (i + 1) * sub)
        cat_scr[rows, 0:D_A] = ya_ref[rows, :]
        _spatial_gating(u_ref, v_ref, g_ref, lng_ref, lnb_ref, ws_ref, bs_ref, cat_scr, D_A, i * sub, sub)
        cat_scr[rows, D_A + D_B:D_MIX] = yc_ref[rows, :]
        return jnp.dot(cat_scr[rows, :], w_ref[...], preferred_element_type=f32)

    def finish(i, y):
        rows = slice(i * sub, (i + 1) * sub)
        x_new = x_ref[rows, :] + _rms_norm(y, pg_ref[...])
        o_ref[rows, :] = x_new
        if emit_h:
            h_ref = rest[0]
            h_ref[rows, :] = _rms_norm(x_new, gn_ref[...]).astype(h_ref.dtype)

    y_prev = project(0)
    for i in range(1, n_sub):
        y_cur = project(i)
        finish(i - 1, y_prev)
        y_prev = y_cur
    finish(n_sub - 1, y_prev)


def _out_proj_sgu(ya, yc, z2, col0, layer, w, post_g, x2, pre_g, ln_g, ln_b, w_s, b_map, *, emit_h,
                  tm=512, n_sub=2):
    n_tok = x2.shape[0]
    assert tm % (n_sub * CHUNK) == 0 and (4 * D_A - col0) % D_B == 0
    c0 = (4 * D_A - col0) // D_B
    const2 = lambda i: (0, 0)
    of_layer = lambda lyr, *dims: pl.BlockSpec((None,) + dims, lambda i: (lyr,) + (0,) * len(dims))
    next_layer = layer + 1 if emit_h else layer
    tile = lambda width, col=0: pl.BlockSpec((tm, width), lambda i: (i, col))
    out_shape = [jax.ShapeDtypeStruct((n_tok, D_MODEL), f32)]
    out_specs = [tile(D_MODEL)]
    if emit_h:
        out_shape.append(jax.ShapeDtypeStruct((n_tok, D_MODEL), bf16))
        out_specs.append(tile(D_MODEL))
    return pl.pallas_call(
        functools.partial(_out_sgu_kernel, tm=tm, n_sub=n_sub, emit_h=emit_h),
        grid=(n_tok // tm,),
        in_specs=[
            tile(D_A), tile(D_C),
            pl.BlockSpec((D_MIX, D_MODEL), const2),
            of_layer(layer, 1, D_MODEL),
            tile(D_MODEL),
            of_layer(next_layer, 1, D_MODEL),
            tile(D_B, c0), tile(D_B, c0 + 1), tile(D_B, c0 + 2),
            of_layer(layer, 1, D_B), of_layer(layer, 1, D_B),
            of_layer(layer, B_GROUPS, CHUNK, CHUNK),
            of_layer(layer, CHUNK, D_B),
        ],
        out_specs=out_specs,
        out_shape=out_shape,
        scratch_shapes=[pltpu.VMEM((tm, D_MIX), bf16)],
        compiler_params=_params("arbitrary"),
        name="out_proj_sgu",
    )(ya, yc, w, post_g, x2, pre_g, z2, z2, z2, ln_g, ln_b, w_s, b_map)


def kernel(x, pre_norm_g, w_in, attn_rpb, sgu_ln_g, sgu_ln_b, sgu_w, sgu_b, conv_w, conv_b,
           conv_ln_g, conv_ln_b, conv_pw_w, conv_pw_b, w_out, post_norm_g):
    bsz, t, d = x.shape
    depth = w_in.shape[0]
    assert d == D_MODEL and t % GRID_W == 0 and t % CHUNK == 0
    n_tok = bsz * t
    x2 = x.reshape(n_tok, d)
    rows3 = lambda a: a.reshape(depth, 1, -1).astype(f32)
    pre_g, post_g = rows3(pre_norm_g), rows3(post_norm_g)
    vecs = _attention_bias_vectors(attn_rpb)
    b_map = jnp.repeat(jnp.swapaxes(sgu_b.astype(f32), 1, 2), D_B // B_GROUPS, axis=2)
    conv_w3 = conv_w.reshape(depth, CONV_W, D_C).astype(f32)
    as_seq = lambda z: z.reshape(bsz, t, z.shape[-1])
    h, zq = _norm_proj(x2, pre_g, w_in, 0, tn=D_A)
    for l in range(depth):
        col0 = D_A if l == 0 else 0
        zr = _in_proj(h, w_in, l, col0=col0)
        if l > 0:
            zq = zr
        y_a, w_out_bf = _attention(as_seq(zq), as_seq(zr), col0, vecs, w_out, l)
        y_c = _conformer_conv(as_seq(zr), col0, l, conv_w3, rows3(conv_b), rows3(conv_ln_g), rows3(conv_ln_b),
                              conv_pw_w.astype(f32), rows3(conv_pw_b))
        last = l == depth - 1
        outs = _out_proj_sgu(
            y_a.reshape(n_tok, D_A), y_c.reshape(n_tok, D_C), zr, col0, l, w_out_bf, post_g, x2, pre_g,
            rows3(sgu_ln_g), rows3(sgu_ln_b), sgu_w.astype(f32), b_map, emit_h=not last)
        x2 = outs[0]
        if not last:
            h = outs[1]
    return x2.reshape(bsz, t, d)
```

```python
import functools
import math

import jax
import jax.numpy as jnp
from jax import lax
from jax.experimental import pallas as pl
from jax.experimental.pallas import tpu as pltpu

D_MODEL = 2048
GRID_W = 64
WIN_H = 8
WIN_W = 16
HEAD_DIM = 128
D_A = 1024
N_HEADS = 8
CHUNK = 128
D_B = 512
B_GROUPS = 4
D_C = 512
CONV_W = 31
D_MIX = D_A + D_B + D_C
D_IN = 4 * D_A + 3 * D_B + 3 * D_C
EPS = 1e-6
MASKED = -1e30

LANES = 128
SUBLANES = 8
VMEM_LIMIT = 56 * 1024 * 1024

Q_ROWS = 4
K_ROWS = Q_ROWS + WIN_H
PAIR = 2 * GRID_W
assert PAIR == LANES

CONV_PAD = 16

f32 = jnp.float32
bf16 = jnp.bfloat16


def _sigmoid(x):
    return 1.0 / (1.0 + jnp.exp2(x * -math.log2(math.e)))


def _silu(x):
    return x * _sigmoid(x)


def _gelu(x):
    return 0.5 * x * (1.0 + lax.erf(x * math.sqrt(0.5)))


def _layer_norm(x, g, b):
    mu = jnp.mean(x, axis=-1, keepdims=True)
    xc = x - mu
    var = jnp.mean(xc * xc, axis=-1, keepdims=True)
    return xc * lax.rsqrt(var + EPS) * g + b


def _rms_norm(x, g):
    ms = jnp.mean(x * x, axis=-1, keepdims=True)
    return x * lax.rsqrt(ms + EPS) * g


def _params(*semantics):
    return pltpu.CompilerParams(dimension_semantics=semantics, vmem_limit_bytes=VMEM_LIMIT)


_VEC_WIDTHS = (("pre_g", D_MODEL), ("post_g", D_MODEL), ("sgu_ln_g", D_B), ("sgu_ln_b", D_B),
               ("conv_b", D_C), ("conv_ln_g", D_C), ("conv_ln_b", D_C), ("conv_pw_b", D_C))


def _vec_spec(name, layer):
    offset = 0
    for other, width in _VEC_WIDTHS:
        if other == name:
            break
        offset += width
    assert offset % width == 0
    index = (layer, 0, offset // width)
    return pl.BlockSpec((None, 1, width), lambda *_: index)


def _norm_proj_kernel(x_ref, g_ref, w_ref, h_ref, z_ref, wb_scr):
    @pl.when(pl.program_id(0) == 0)
    def _():
        wb_scr[...] = w_ref[...].astype(bf16)

    h = _rms_norm(x_ref[...], g_ref[...]).astype(bf16)
    h_ref[...] = h
    z_ref[...] = jnp.dot(h, wb_scr[...], preferred_element_type=f32).astype(z_ref.dtype)


def _norm_proj(x2, g, w_in, layer, tm=1024, tn=1024):
    n_tok = x2.shape[0]
    return pl.pallas_call(
        _norm_proj_kernel,
        grid=(n_tok // tm,),
        in_specs=[pl.BlockSpec((tm, D_MODEL), lambda i: (i, 0)),
                  _vec_spec("pre_g", layer),
                  pl.BlockSpec((None, D_MODEL, tn), lambda i: (layer, 0, 0))],
        out_specs=[pl.BlockSpec((tm, D_MODEL), lambda i: (i, 0)),
                   pl.BlockSpec((tm, tn), lambda i: (i, 0))],
        out_shape=[jax.ShapeDtypeStruct((n_tok, D_MODEL), bf16),
                   jax.ShapeDtypeStruct((n_tok, tn), bf16)],
        scratch_shapes=[pltpu.VMEM((D_MODEL, tn), bf16)],
        compiler_params=_params("arbitrary"),
        name="norm_proj",
    )(x2, g, w_in)


def _in_proj_kernel(h_ref, w_ref, z_ref, wb_scr):
    @pl.when(pl.program_id(1) == 0)
    def _():
        wb_scr[...] = w_ref[...].astype(bf16)

    z_ref[...] = jnp.dot(h_ref[...], wb_scr[...], preferred_element_type=f32).astype(z_ref.dtype)


def _in_proj(h, w_in, layer, col0=0, tm=2048, tn=1024):
    n_tok = h.shape[0]
    assert col0 % tn == 0 and (D_IN - col0) % tn == 0
    return pl.pallas_call(
        _in_proj_kernel,
        grid=((D_IN - col0) // tn, n_tok // tm),
        in_specs=[
            pl.BlockSpec((tm, D_MODEL), lambda j, i: (i, 0)),
            pl.BlockSpec((None, D_MODEL, tn), lambda j, i: (layer, 0, col0 // tn + j)),
        ],
        out_specs=pl.BlockSpec((tm, tn), lambda j, i: (i, j)),
        out_shape=jax.ShapeDtypeStruct((n_tok, D_IN - col0), bf16),
        scratch_shapes=[pltpu.VMEM((D_MODEL, tn), bf16)],
        compiler_params=_params("arbitrary", "arbitrary"),
        name="in_proj",
    )(h, w_in)


def _attn_kernel(q_ref, k_ref, v_ref, g_ref, vec_ref, wo_ref, o_ref, wob_ref, bias_ref, *, rows, heads):
    wob_ref[...] = wo_ref[...].astype(bf16)
    log2e = math.log2(math.e)
    scale = HEAD_DIM ** -0.5 * log2e
    nq = Q_ROWS * GRID_W
    nk = K_ROWS * GRID_W
    w_idx = lax.broadcasted_iota(jnp.int32, (GRID_W, PAIR), 0)
    lane = lax.broadcasted_iota(jnp.int32, (GRID_W, PAIR), 1)
    left_half = lane < GRID_W
    kc = jnp.where(left_half, lane, lane - GRID_W)
    col_start = jnp.clip(w_idx - WIN_W // 2, 0, GRID_W - WIN_W)
    in_win = (kc >= col_start) & (kc < col_start + WIN_W)
    for hh in range(heads):
        for e in range(2 * WIN_H):
            vec = jnp.broadcast_to(vec_ref[hh, e:e + 1, :], (GRID_W, PAIR))
            bias = pltpu.roll(vec, 0, 1, stride=1, stride_axis=0) * log2e
            bias_ref[hh, e] = jnp.where(in_win, bias, MASKED)

    n_blk = rows // Q_ROWS
    first_key_row = lambda blk: min(max(blk * Q_ROWS - WIN_H // 2, 0), rows - K_ROWS)

    head_cols = lambda hh: slice(hh * HEAD_DIM, (hh + 1) * HEAD_DIM)
    items = [(hh, blk) for hh in range(heads) for blk in range(n_blk)]

    def scores(item):
        hh, blk = item
        q0, k0 = blk * nq, first_key_row(blk) * GRID_W
        q = (q_ref[0, q0:q0 + nq, head_cols(hh)].astype(f32) * scale).astype(bf16)
        kw = k_ref[0, k0:k0 + nk, head_cols(hh)]
        return lax.dot_general(q, kw, (((1,), (1,)), ((), ())), preferred_element_type=f32)

    s_next = scores(items[0])
    for pos, (hh, blk) in enumerate(items):
        ks = first_key_row(blk)
        q0, k0 = blk * nq, ks * GRID_W
        vw = v_ref[0, k0:k0 + nk, head_cols(hh)]
        s = s_next
        if pos + 1 < len(items):
            s_next = scores(items[pos + 1])

        p_rows = []
        for rq in range(Q_ROWS):
            r = blk * Q_ROWS + rq
            rs = min(max(r - WIN_H // 2, 0), rows - WIN_H)
            pieces = {}
            for ip in range(K_ROWS // 2):
                kr = ks + 2 * ip
                ok0, ok1 = rs <= kr < rs + WIN_H, rs <= kr + 1 < rs + WIN_H
                if not (ok0 or ok1):
                    continue
                dr0 = kr - r + (WIN_H - 1)
                sub = s[rq * GRID_W:(rq + 1) * GRID_W, ip * PAIR:(ip + 1) * PAIR]
                if ok0 and ok1:
                    bias = bias_ref[hh, dr0 + 1]
                elif ok0:
                    bias = jnp.where(left_half, bias_ref[hh, dr0 + 1], MASKED)
                else:
                    bias = jnp.where(left_half, MASKED, bias_ref[hh, dr0 + 1])
                pieces[ip] = sub + bias
            m = jnp.max(functools.reduce(jnp.maximum, pieces.values()), axis=-1, keepdims=True)
            pieces = {ip: jnp.exp2(sv - m) for ip, sv in pieces.items()}
            zero = jnp.zeros((GRID_W, PAIR), bf16)
            p_rows.append(jnp.concatenate(
                [pieces[ip].astype(bf16) if ip in pieces else zero for ip in range(K_ROWS // 2)], axis=1))
        p = jnp.concatenate(p_rows, axis=0)
        v_ext = jnp.concatenate([vw, jnp.ones((nk, HEAD_DIM), bf16)], axis=1)
        o = jnp.dot(p, v_ext, preferred_element_type=f32)
        for rq in range(Q_ROWS):
            rows_q = slice(q0 + rq * GRID_W, q0 + (rq + 1) * GRID_W)
            g = g_ref[0, rows_q, head_cols(hh)].astype(f32)
            o_rq = o[rq * GRID_W:(rq + 1) * GRID_W]
            o_rq = o_rq[:, :HEAD_DIM] * (1.0 / o_rq[:, HEAD_DIM:])
            o_ref[0, rows_q, head_cols(hh)] = (o_rq * _silu(g)).astype(o_ref.dtype)


def _attention(zq3, zr3, col0, vecs, w_out, layer, heads=2):
    bsz, t, _ = zq3.shape
    rows = t // GRID_W
    assert rows % Q_ROWS == 0 and rows >= K_ROWS and N_HEADS % heads == 0
    groups = N_HEADS // heads
    slab = D_MIX // (bsz * groups)
    assert slab * bsz * groups == D_MIX and slab % (2 * SUBLANES) == 0
    blk = (1, t, heads * HEAD_DIM)
    assert (D_A - col0) % blk[2] == 0
    k0 = (D_A - col0) // blk[2]
    return pl.pallas_call(
        functools.partial(_attn_kernel, rows=rows, heads=heads),
        grid=(bsz, groups),
        in_specs=[
            pl.BlockSpec(blk, lambda b, h: (b, 0, h)),
            pl.BlockSpec(blk, lambda b, h: (b, 0, k0 + h)),
            pl.BlockSpec(blk, lambda b, h: (b, 0, k0 + groups + h)),
            pl.BlockSpec(blk, lambda b, h: (b, 0, k0 + 2 * groups + h)),
            pl.BlockSpec((None, heads, 2 * WIN_H, PAIR), lambda b, h: (layer, h, 0, 0)),
            pl.BlockSpec((None, slab, D_MODEL), lambda b, h: (layer, b * groups + h, 0)),
        ],
        out_specs=[pl.BlockSpec(blk, lambda b, h: (b, 0, h)),
                   pl.BlockSpec((slab, D_MODEL), lambda b, h: (b * groups + h, 0))],
        out_shape=[jax.ShapeDtypeStruct((bsz, t, D_A), bf16),
                   jax.ShapeDtypeStruct((D_MIX, D_MODEL), bf16)],
        scratch_shapes=[pltpu.VMEM((heads, 2 * WIN_H, GRID_W, PAIR), f32)],
        compiler_params=_params("arbitrary", "arbitrary"),
        name="nbr_attention",
    )(zq3, zr3, zr3, zr3, vecs, w_out)


def _attention_bias_vectors(rpb):
    half = WIN_W - 1
    lead = rpb.shape[:-2]
    pad = jnp.full(lead + (1, 2 * WIN_W - 1), MASKED, f32)
    ext = jnp.concatenate([pad, rpb.astype(f32), pad], axis=-2)
    lower, upper = ext[..., :-1, :], ext[..., 1:, :]
    gap = jnp.full(lead + (2 * WIN_H, GRID_W - 2 * half - 1), MASKED, f32)
    return jnp.concatenate([lower[..., half:], gap, upper, gap, lower[..., :half]], axis=-1)


def _conv_kernel(a_ref, b_ref, g_ref, cw_ref, cb_ref, lng_ref, lnb_ref, pw_ref, pb_ref, o_ref,
                 h_scr, c_scr, *, t, tt, tm):
    zeros = jnp.zeros((CONV_PAD, D_C), f32)
    h_scr[pl.ds(0, CONV_PAD), :] = zeros
    h_scr[pl.ds(CONV_PAD + t, CONV_PAD), :] = zeros
    h_scr[pl.ds(CONV_PAD, t), :] = a_ref[0].astype(f32) * _sigmoid(b_ref[0].astype(f32))

    shift = CONV_PAD - CONV_W // 2
    n_al = (CONV_W + shift + SUBLANES - 1) // SUBLANES
    win_rows = tt + n_al * SUBLANES

    def conv_tile(ti, carry):
        t0 = pl.multiple_of(ti * tt, tt)
        for cb in range(D_C // LANES):
            cols = slice(cb * LANES, (cb + 1) * LANES)
            win = h_scr[pl.ds(t0, win_rows), cols]
            acc = None
            for b in range(SUBLANES):
                part = None
                for a in range(n_al):
                    k = SUBLANES * a + b - shift
                    if 0 <= k < CONV_W:
                        term = cw_ref[k:k + 1, cols] * win[SUBLANES * a:SUBLANES * a + tt + SUBLANES]
                        part = term if part is None else part + term
                part = part[b:b + tt]
                acc = part if acc is None else acc + part
            c_scr[pl.ds(t0, tt), cols] = acc
        return carry

    lax.fori_loop(0, t // tt, conv_tile, 0)

    def post_tile(ti, carry):
        t0 = pl.multiple_of(ti * tm, tm)
        c = c_scr[pl.ds(t0, tm), :] + cb_ref[...]
        c = _silu(_layer_norm(c, lng_ref[...], lnb_ref[...])).astype(bf16)
        y = jnp.dot(c, pw_ref[...].astype(bf16), preferred_element_type=f32) + pb_ref[...]
        g = g_ref[0, pl.ds(t0, tm), :].astype(f32)
        o_ref[0, pl.ds(t0, tm), :] = (y * _silu(g)).astype(o_ref.dtype)
        return carry

    lax.fori_loop(0, t // tm, post_tile, 0)


def _conformer_conv(z3, col0, layer, conv_w, vec_pack, pw_w, tt=128, tm=1024):
    bsz, t, _ = z3.shape
    assert t % tt == 0 and t % tm == 0 and (4 * D_A + 3 * D_B - col0) % D_C == 0
    c0 = (4 * D_A + 3 * D_B - col0) // D_C
    blk = (1, t, D_C)
    of_layer = lambda *dims: pl.BlockSpec((None,) + dims, lambda b: (layer,) + (0,) * len(dims))
    return pl.pallas_call(
        functools.partial(_conv_kernel, t=t, tt=tt, tm=tm),
        grid=(bsz,),
        in_specs=[
            pl.BlockSpec(blk, lambda b: (b, 0, c0)),
            pl.BlockSpec(blk, lambda b: (b, 0, c0 + 1)),
            pl.BlockSpec(blk, lambda b: (b, 0, c0 + 2)),
            of_layer(CONV_W, D_C), _vec_spec("conv_b", layer), _vec_spec("conv_ln_g", layer),
            _vec_spec("conv_ln_b", layer), of_layer(D_C, D_C), _vec_spec("conv_pw_b", layer),
        ],
        out_specs=pl.BlockSpec(blk, lambda b: (b, 0, 0)),
        out_shape=jax.ShapeDtypeStruct((bsz, t, D_C), bf16),
        scratch_shapes=[pltpu.VMEM((t + 2 * CONV_PAD, D_C), f32), pltpu.VMEM((t, D_C), f32)],
        compiler_params=_params("arbitrary"),
        name="conformer_conv",
    )(z3, z3, z3, conv_w, vec_pack, vec_pack, vec_pack, pw_w, vec_pack)


def _spatial_gating(u_ref, v_ref, g_ref, lng_ref, lnb_ref, ws_ref, bs_ref, o_ref, col0, r0, nr):
    gd = D_B // B_GROUPS
    for n in range(r0 // CHUNK, (r0 + nr) // CHUNK):
        rows = slice(n * CHUNK, (n + 1) * CHUNK)
        v = _layer_norm(_gelu(v_ref[rows, :].astype(f32)), lng_ref[...], lnb_ref[...]).astype(bf16)
        for grp in range(B_GROUPS):
            cols = slice(grp * gd, (grp + 1) * gd)
            s = jnp.dot(ws_ref[grp].astype(bf16), v[:, cols], preferred_element_type=f32) + bs_ref[:, cols]
            u = _gelu(u_ref[rows, cols].astype(f32))
            g = g_ref[rows, cols].astype(f32)
            o_ref[rows, col0 + grp * gd:col0 + (grp + 1) * gd] = (u * s * _silu(g)).astype(o_ref.dtype)


def _out_sgu_kernel(ya_ref, yc_ref, w_ref, pg_ref, x_ref, gn_ref,
                    u_ref, v_ref, g_ref, lng_ref, lnb_ref, ws_ref, bs_ref, o_ref, *rest, tm, n_sub, emit_h):
    cat_scr = rest[-1]
    sub = tm // n_sub

    def project(i):
        rows = slice(i * sub, (i + 1) * sub)
        cat_scr[rows, 0:D_A] = ya_ref[rows, :]
        _spatial_gating(u_ref, v_ref, g_ref, lng_ref, lnb_ref, ws_ref, bs_ref, cat_scr, D_A, i * sub, sub)
        cat_scr[rows, D_A + D_B:D_MIX] = yc_ref[rows, :]
        return jnp.dot(cat_scr[rows, :], w_ref[...], preferred_element_type=f32)

    def finish(i, y):
        rows = slice(i * sub, (i + 1) * sub)
        x_new = x_ref[rows, :] + _rms_norm(y, pg_ref[...])
        o_ref[rows, :] = x_new
        if emit_h:
            h_ref = rest[0]
            h_ref[rows, :] = _rms_norm(x_new, gn_ref[...]).astype(h_ref.dtype)

    y_prev = project(0)
    for i in range(1, n_sub):
        y_cur = project(i)
        finish(i - 1, y_prev)
        y_prev = y_cur
    finish(n_sub - 1, y_prev)


def _out_proj_sgu(ya, yc, z2, col0, layer, w, x2, vec_pack, w_s, b_map, *, emit_h, tm=512, n_sub=2):
    n_tok = x2.shape[0]
    assert tm % (n_sub * CHUNK) == 0 and (4 * D_A - col0) % D_B == 0
    c0 = (4 * D_A - col0) // D_B
    const2 = lambda i: (0, 0)
    of_layer = lambda lyr, *dims: pl.BlockSpec((None,) + dims, lambda i: (lyr,) + (0,) * len(dims))
    next_layer = layer + 1 if emit_h else layer
    tile = lambda width, col=0: pl.BlockSpec((tm, width), lambda i: (i, col))
    out_shape = [jax.ShapeDtypeStruct((n_tok, D_MODEL), f32)]
    out_specs = [tile(D_MODEL)]
    if emit_h:
        out_shape.append(jax.ShapeDtypeStruct((n_tok, D_MODEL), bf16))
        out_specs.append(tile(D_MODEL))
    return pl.pallas_call(
        functools.partial(_out_sgu_kernel, tm=tm, n_sub=n_sub, emit_h=emit_h),
        grid=(n_tok // tm,),
        in_specs=[
            tile(D_A), tile(D_C),
            pl.BlockSpec((D_MIX, D_MODEL), const2),
            _vec_spec("post_g", layer),
            tile(D_MODEL),
            _vec_spec("pre_g", next_layer),
            tile(D_B, c0), tile(D_B, c0 + 1), tile(D_B, c0 + 2),
            _vec_spec("sgu_ln_g", layer), _vec_spec("sgu_ln_b", layer),
            of_layer(layer, B_GROUPS, CHUNK, CHUNK),
            of_layer(layer, CHUNK, D_B),
        ],
        out_specs=out_specs,
        out_shape=out_shape,
        scratch_shapes=[pltpu.VMEM((tm, D_MIX), bf16)],
        compiler_params=_params("arbitrary"),
        name="out_proj_sgu",
    )(ya, yc, w, vec_pack, x2, vec_pack, z2, z2, z2, vec_pack, vec_pack, w_s, b_map)


def kernel(x, pre_norm_g, w_in, attn_rpb, sgu_ln_g, sgu_ln_b, sgu_w, sgu_b, conv_w, conv_b,
           conv_ln_g, conv_ln_b, conv_pw_w, conv_pw_b, w_out, post_norm_g):
    bsz, t, d = x.shape
    depth = w_in.shape[0]
    assert d == D_MODEL and t % GRID_W == 0 and t % CHUNK == 0
    n_tok = bsz * t
    x2 = x.reshape(n_tok, d)
    named = dict(pre_g=pre_norm_g, post_g=post_norm_g, sgu_ln_g=sgu_ln_g, sgu_ln_b=sgu_ln_b,
                 conv_b=conv_b, conv_ln_g=conv_ln_g, conv_ln_b=conv_ln_b, conv_pw_b=conv_pw_b)
    vec_pack = jnp.concatenate([named[name].astype(f32) for name, _ in _VEC_WIDTHS], axis=1)[:, None, :]
    vecs = _attention_bias_vectors(attn_rpb)
    b_map = jnp.repeat(jnp.swapaxes(sgu_b.astype(f32), 1, 2), D_B // B_GROUPS, axis=2)
    conv_w3 = conv_w.reshape(depth, CONV_W, D_C).astype(f32)
    as_seq = lambda z: z.reshape(bsz, t, z.shape[-1])
    h, zq = _norm_proj(x2, vec_pack, w_in, 0, tn=D_A)
    for l in range(depth):
        col0 = D_A if l == 0 else 0
        zr = _in_proj(h, w_in, l, col0=col0)
        if l > 0:
            zq = zr
        y_a, w_out_bf = _attention(as_seq(zq), as_seq(zr), col0, vecs, w_out, l)
        y_c = _conformer_conv(as_seq(zr), col0, l, conv_w3, vec_pack, conv_pw_w.astype(f32))
        last = l == depth - 1
        outs = _out_proj_sgu(y_a.reshape(n_tok, D_A), y_c.reshape(n_tok, D_C), zr, col0, l, w_out_bf, x2,
                             vec_pack, sgu_w.astype(f32), b_map, emit_h=not last)
        x2 = outs[0]
        if not last:
            h = outs[1]
    return x2.reshape(bsz, t, d)
```

```python
import functools
import math

import jax
import jax.numpy as jnp
from jax import lax
from jax.experimental import pallas as pl
from jax.experimental.pallas import tpu as pltpu

D_MODEL = 2048
GRID_W = 64
WIN_H = 8
WIN_W = 16
HEAD_DIM = 128
D_A = 1024
N_HEADS = 8
CHUNK = 128
D_B = 512
B_GROUPS = 4
D_C = 512
CONV_W = 31
D_MIX = D_A + D_B + D_C
D_IN = 4 * D_A + 3 * D_B + 3 * D_C
EPS = 1e-6
MASKED = -1e30

LANES = 128
SUBLANES = 8
VMEM_LIMIT = 56 * 1024 * 1024

Q_ROWS = 4
K_ROWS = Q_ROWS + WIN_H
PAIR = 2 * GRID_W
assert PAIR == LANES

CONV_PAD = 16
X_SLOTS = 3

f32 = jnp.float32
bf16 = jnp.bfloat16


def _sigmoid(x):
    return 1.0 / (1.0 + jnp.exp2(x * -math.log2(math.e)))


def _silu(x):
    return x * _sigmoid(x)


def _gelu(x):
    return 0.5 * x * (1.0 + lax.erf(x * math.sqrt(0.5)))


def _layer_norm(x, g, b):
    mu = jnp.mean(x, axis=-1, keepdims=True)
    xc = x - mu
    var = jnp.mean(xc * xc, axis=-1, keepdims=True)
    return xc * lax.rsqrt(var + EPS) * g + b


def _rms_norm(x, g):
    ms = jnp.mean(x * x, axis=-1, keepdims=True)
    return x * lax.rsqrt(ms + EPS) * g


def _params(*semantics):
    return pltpu.CompilerParams(dimension_semantics=semantics, vmem_limit_bytes=VMEM_LIMIT)


def _whole(arr):
    return pl.BlockSpec(arr.shape, lambda *_: (0,) * arr.ndim)


def _layer_row(ref, layer):
    return ref.at[layer:layer + 1, :]


def _norm_proj_kernel(x_ref, g_ref, w_ref, h_ref, z_ref, wb_scr, *, layer):
    @pl.when(pl.program_id(0) == 0)
    def _():
        wb_scr[...] = w_ref[...].astype(bf16)

    h = _rms_norm(x_ref[...], _layer_row(g_ref, layer)[...]).astype(bf16)
    h_ref[...] = h
    z_ref[...] = jnp.dot(h, wb_scr[...], preferred_element_type=f32).astype(z_ref.dtype)


def _norm_proj(x2, g, w_in, layer, tm=1024, tn=1024):
    n_tok = x2.shape[0]
    return pl.pallas_call(
        functools.partial(_norm_proj_kernel, layer=layer),
        grid=(n_tok // tm,),
        in_specs=[pl.BlockSpec((tm, D_MODEL), lambda i: (i, 0)),
                  _whole(g),
                  pl.BlockSpec((None, D_MODEL, tn), lambda i: (layer, 0, 0))],
        out_specs=[pl.BlockSpec((tm, D_MODEL), lambda i: (i, 0)),
                   pl.BlockSpec((tm, tn), lambda i: (i, 0))],
        out_shape=[jax.ShapeDtypeStruct((n_tok, D_MODEL), bf16),
                   jax.ShapeDtypeStruct((n_tok, tn), bf16)],
        scratch_shapes=[pltpu.VMEM((D_MODEL, tn), bf16)],
        compiler_params=_params("arbitrary"),
        name="norm_proj",
    )(x2, g, w_in)


def _in_proj_kernel(h_ref, w_ref, z_ref, wb_scr):
    @pl.when(pl.program_id(1) == 0)
    def _():
        wb_scr[...] = w_ref[...].astype(bf16)

    z_ref[...] = jnp.dot(h_ref[...], wb_scr[...], preferred_element_type=f32).astype(z_ref.dtype)


def _in_proj(h, w_in, layer, col0=0, tm=2048, tn=1024):
    n_tok = h.shape[0]
    assert col0 % tn == 0 and (D_IN - col0) % tn == 0
    return pl.pallas_call(
        _in_proj_kernel,
        grid=((D_IN - col0) // tn, n_tok // tm),
        in_specs=[
            pl.BlockSpec((tm, D_MODEL), lambda j, i: (i, 0)),
            pl.BlockSpec((None, D_MODEL, tn), lambda j, i: (layer, 0, col0 // tn + j)),
        ],
        out_specs=pl.BlockSpec((tm, tn), lambda j, i: (i, j)),
        out_shape=jax.ShapeDtypeStruct((n_tok, D_IN - col0), bf16),
        scratch_shapes=[pltpu.VMEM((D_MODEL, tn), bf16)],
        compiler_params=_params("arbitrary", "arbitrary"),
        name="in_proj",
    )(h, w_in)


def _attn_kernel(q_ref, k_ref, v_ref, g_ref, vec_ref, wo_ref, o_ref, wob_ref, bias_ref, *, rows, heads):
    wob_ref[...] = wo_ref[...].astype(bf16)
    log2e = math.log2(math.e)
    scale = HEAD_DIM ** -0.5 * log2e
    nq = Q_ROWS * GRID_W
    nk = K_ROWS * GRID_W
    w_idx = lax.broadcasted_iota(jnp.int32, (GRID_W, PAIR), 0)
    lane = lax.broadcasted_iota(jnp.int32, (GRID_W, PAIR), 1)
    left_half = lane < GRID_W
    kc = jnp.where(left_half, lane, lane - GRID_W)
    col_start = jnp.clip(w_idx - WIN_W // 2, 0, GRID_W - WIN_W)
    in_win = (kc >= col_start) & (kc < col_start + WIN_W)
    for hh in range(heads):
        for e in range(2 * WIN_H):
            vec = jnp.broadcast_to(vec_ref[hh, e:e + 1, :], (GRID_W, PAIR))
            bias = pltpu.roll(vec, 0, 1, stride=1, stride_axis=0) * log2e
            bias_ref[hh, e] = jnp.where(in_win, bias, MASKED)

    n_blk = rows // Q_ROWS
    first_key_row = lambda blk: min(max(blk * Q_ROWS - WIN_H // 2, 0), rows - K_ROWS)

    head_cols = lambda hh: slice(hh * HEAD_DIM, (hh + 1) * HEAD_DIM)
    items = [(hh, blk) for hh in range(heads) for blk in range(n_blk)]

    def scores(item):
        hh, blk = item
        q0, k0 = blk * nq, first_key_row(blk) * GRID_W
        q = (q_ref[0, q0:q0 + nq, head_cols(hh)].astype(f32) * scale).astype(bf16)
        kw = k_ref[0, k0:k0 + nk, head_cols(hh)]
        return lax.dot_general(q, kw, (((1,), (1,)), ((), ())), preferred_element_type=f32)

    s_next = scores(items[0])
    for pos, (hh, blk) in enumerate(items):
        ks = first_key_row(blk)
        q0, k0 = blk * nq, ks * GRID_W
        vw = v_ref[0, k0:k0 + nk, head_cols(hh)]
        s = s_next
        if pos + 1 < len(items):
            s_next = scores(items[pos + 1])

        p_rows = []
        for rq in range(Q_ROWS):
            r = blk * Q_ROWS + rq
            rs = min(max(r - WIN_H // 2, 0), rows - WIN_H)
            pieces = {}
            for ip in range(K_ROWS // 2):
                kr = ks + 2 * ip
                ok0, ok1 = rs <= kr < rs + WIN_H, rs <= kr + 1 < rs + WIN_H
                if not (ok0 or ok1):
                    continue
                dr0 = kr - r + (WIN_H - 1)
                sub = s[rq * GRID_W:(rq + 1) * GRID_W, ip * PAIR:(ip + 1) * PAIR]
                if ok0 and ok1:
                    bias = bias_ref[hh, dr0 + 1]
                elif ok0:
                    bias = jnp.where(left_half, bias_ref[hh, dr0 + 1], MASKED)
                else:
                    bias = jnp.where(left_half, MASKED, bias_ref[hh, dr0 + 1])
                pieces[ip] = sub + bias
            m = jnp.max(functools.reduce(jnp.maximum, pieces.values()), axis=-1, keepdims=True)
            pieces = {ip: jnp.exp2(sv - m) for ip, sv in pieces.items()}
            zero = jnp.zeros((GRID_W, PAIR), bf16)
            p_rows.append(jnp.concatenate(
                [pieces[ip].astype(bf16) if ip in pieces else zero for ip in range(K_ROWS // 2)], axis=1))
        p = jnp.concatenate(p_rows, axis=0)
        v_ext = jnp.concatenate([vw, jnp.ones((nk, HEAD_DIM), bf16)], axis=1)
        o = jnp.dot(p, v_ext, preferred_element_type=f32)
        for rq in range(Q_ROWS):
            rows_q = slice(q0 + rq * GRID_W, q0 + (rq + 1) * GRID_W)
            g = g_ref[0, rows_q, head_cols(hh)].astype(f32)
            o_rq = o[rq * GRID_W:(rq + 1) * GRID_W]
            o_rq = o_rq[:, :HEAD_DIM] * (1.0 / o_rq[:, HEAD_DIM:])
            o_ref[0, rows_q, head_cols(hh)] = (o_rq * _silu(g)).astype(o_ref.dtype)


def _attention(zq3, zr3, col0, vecs, w_out, layer, heads=2):
    bsz, t, _ = zq3.shape
    rows = t // GRID_W
    assert rows % Q_ROWS == 0 and rows >= K_ROWS and N_HEADS % heads == 0
    groups = N_HEADS // heads
    slab = D_MIX // (bsz * groups)
    assert slab * bsz * groups == D_MIX and slab % (2 * SUBLANES) == 0
    blk = (1, t, heads * HEAD_DIM)
    assert (D_A - col0) % blk[2] == 0
    k0 = (D_A - col0) // blk[2]
    return pl.pallas_call(
        functools.partial(_attn_kernel, rows=rows, heads=heads),
        grid=(bsz, groups),
        in_specs=[
            pl.BlockSpec(blk, lambda b, h: (b, 0, h)),
            pl.BlockSpec(blk, lambda b, h: (b, 0, k0 + h)),
            pl.BlockSpec(blk, lambda b, h: (b, 0, k0 + groups + h)),
            pl.BlockSpec(blk, lambda b, h: (b, 0, k0 + 2 * groups + h)),
            pl.BlockSpec((None, heads, 2 * WIN_H, PAIR), lambda b, h: (layer, h, 0, 0)),
            pl.BlockSpec((None, slab, D_MODEL), lambda b, h: (layer, b * groups + h, 0)),
        ],
        out_specs=[pl.BlockSpec(blk, lambda b, h: (b, 0, h)),
                   pl.BlockSpec((slab, D_MODEL), lambda b, h: (b * groups + h, 0))],
        out_shape=[jax.ShapeDtypeStruct((bsz, t, D_A), bf16),
                   jax.ShapeDtypeStruct((D_MIX, D_MODEL), bf16)],
        scratch_shapes=[pltpu.VMEM((heads, 2 * WIN_H, GRID_W, PAIR), f32)],
        compiler_params=_params("arbitrary", "arbitrary"),
        name="nbr_attention",
    )(zq3, zr3, zr3, zr3, vecs, w_out)


def _attention_bias_vectors(rpb):
    half = WIN_W - 1
    lead = rpb.shape[:-2]
    pad = jnp.full(lead + (1, 2 * WIN_W - 1), MASKED, f32)
    ext = jnp.concatenate([pad, rpb.astype(f32), pad], axis=-2)
    lower, upper = ext[..., :-1, :], ext[..., 1:, :]
    gap = jnp.full(lead + (2 * WIN_H, GRID_W - 2 * half - 1), MASKED, f32)
    return jnp.concatenate([lower[..., half:], gap, upper, gap, lower[..., :half]], axis=-1)


def _conv_kernel(a_ref, b_ref, g_ref, cw_ref, cb_ref, lng_ref, lnb_ref, pw_ref, pb_ref, o_ref,
                 h_scr, c_scr, *, t, tt, tm, layer):
    cb_ref, lng_ref, lnb_ref, pb_ref = (_layer_row(r, layer) for r in (cb_ref, lng_ref, lnb_ref, pb_ref))
    zeros = jnp.zeros((CONV_PAD, D_C), f32)
    h_scr[pl.ds(0, CONV_PAD), :] = zeros
    h_scr[pl.ds(CONV_PAD + t, CONV_PAD), :] = zeros
    h_scr[pl.ds(CONV_PAD, t), :] = a_ref[0].astype(f32) * _sigmoid(b_ref[0].astype(f32))

    shift = CONV_PAD - CONV_W // 2
    n_al = (CONV_W + shift + SUBLANES - 1) // SUBLANES
    win_rows = tt + n_al * SUBLANES

    def conv_tile(ti, carry):
        t0 = pl.multiple_of(ti * tt, tt)
        for cb in range(D_C // LANES):
            cols = slice(cb * LANES, (cb + 1) * LANES)
            win = h_scr[pl.ds(t0, win_rows), cols]
            acc = None
            for b in range(SUBLANES):
                part = None
                for a in range(n_al):
                    k = SUBLANES * a + b - shift
                    if 0 <= k < CONV_W:
                        term = cw_ref[k:k + 1, cols] * win[SUBLANES * a:SUBLANES * a + tt + SUBLANES]
                        part = term if part is None else part + term
                part = part[b:b + tt]
                acc = part if acc is None else acc + part
            c_scr[pl.ds(t0, tt), cols] = acc
        return carry

    lax.fori_loop(0, t // tt, conv_tile, 0)

    def post_tile(ti, carry):
        t0 = pl.multiple_of(ti * tm, tm)
        c = c_scr[pl.ds(t0, tm), :] + cb_ref[...]
        c = _silu(_layer_norm(c, lng_ref[...], lnb_ref[...])).astype(bf16)
        y = jnp.dot(c, pw_ref[...].astype(bf16), preferred_element_type=f32) + pb_ref[...]
        g = g_ref[0, pl.ds(t0, tm), :].astype(f32)
        o_ref[0, pl.ds(t0, tm), :] = (y * _silu(g)).astype(o_ref.dtype)
        return carry

    lax.fori_loop(0, t // tm, post_tile, 0)


def _conformer_conv(z3, col0, layer, conv_w, conv_b, ln_g, ln_b, pw_w, pw_b, tt=128, tm=1024):
    bsz, t, _ = z3.shape
    assert t % tt == 0 and t % tm == 0 and (4 * D_A + 3 * D_B - col0) % D_C == 0
    c0 = (4 * D_A + 3 * D_B - col0) // D_C
    blk = (1, t, D_C)
    of_layer = lambda *dims: pl.BlockSpec((None,) + dims, lambda b: (layer,) + (0,) * len(dims))
    return pl.pallas_call(
        functools.partial(_conv_kernel, t=t, tt=tt, tm=tm, layer=layer),
        grid=(bsz,),
        in_specs=[
            pl.BlockSpec(blk, lambda b: (b, 0, c0)),
            pl.BlockSpec(blk, lambda b: (b, 0, c0 + 1)),
            pl.BlockSpec(blk, lambda b: (b, 0, c0 + 2)),
            of_layer(CONV_W, D_C), _whole(conv_b), _whole(ln_g), _whole(ln_b), of_layer(D_C, D_C), _whole(pw_b),
        ],
        out_specs=pl.BlockSpec(blk, lambda b: (b, 0, 0)),
        out_shape=jax.ShapeDtypeStruct((bsz, t, D_C), bf16),
        scratch_shapes=[pltpu.VMEM((t + 2 * CONV_PAD, D_C), f32), pltpu.VMEM((t, D_C), f32)],
        compiler_params=_params("arbitrary"),
        name="conformer_conv",
    )(z3, z3, z3, conv_w, conv_b, ln_g, ln_b, pw_w, pw_b)


def _spatial_gating(u_ref, v_ref, g_ref, lng_ref, lnb_ref, ws_ref, bs_ref, o_ref, col0, r0, nr):
    gd = D_B // B_GROUPS
    for n in range(r0 // CHUNK, (r0 + nr) // CHUNK):
        rows = slice(n * CHUNK, (n + 1) * CHUNK)
        v = _layer_norm(_gelu(v_ref[rows, :].astype(f32)), lng_ref[...], lnb_ref[...]).astype(bf16)
        for grp in range(B_GROUPS):
            cols = slice(grp * gd, (grp + 1) * gd)
            s = jnp.dot(ws_ref[grp].astype(bf16), v[:, cols], preferred_element_type=f32) + bs_ref[:, cols]
            u = _gelu(u_ref[rows, cols].astype(f32))
            g = g_ref[rows, cols].astype(f32)
            o_ref[rows, col0 + grp * gd:col0 + (grp + 1) * gd] = (u * s * _silu(g)).astype(o_ref.dtype)


def _out_sgu_kernel(ya_ref, yc_ref, w_ref, pg_ref, x_hbm, gn_ref,
                    u_ref, v_ref, g_ref, lng_ref, lnb_ref, ws_ref, bs_ref, o_ref, *rest,
                    tm, n_sub, emit_h, layer):
    pg_ref, lng_ref, lnb_ref = (_layer_row(r, layer) for r in (pg_ref, lng_ref, lnb_ref))
    gn_ref = _layer_row(gn_ref, layer + 1 if emit_h else layer)
    x_ring, x_sem, cat_scr = rest[-3:]
    sub = tm // n_sub

    step, n_steps = pl.program_id(0), pl.num_programs(0)

    def x_copy(tile):
        slot = tile % X_SLOTS
        return pltpu.make_async_copy(x_hbm.at[pl.ds(tile * tm, tm), :], x_ring.at[slot], x_sem.at[slot])

    @pl.when(step == 0)
    def _():
        x_copy(0).start()
        x_copy(1).start()

    @pl.when(step + 2 < n_steps)
    def _():
        x_copy(step + 2).start()

    x_copy(step).wait()
    x_ref = x_ring.at[step % X_SLOTS]

    def project(i):
        rows = slice(i * sub, (i + 1) * sub)
        cat_scr[rows, 0:D_A] = ya_ref[rows, :]
        _spatial_gating(u_ref, v_ref, g_ref, lng_ref, lnb_ref, ws_ref, bs_ref, cat_scr, D_A, i * sub, sub)
        cat_scr[rows, D_A + D_B:D_MIX] = yc_ref[rows, :]
        return jnp.dot(cat_scr[rows, :], w_ref[...], preferred_element_type=f32)

    def finish(i, y):
        rows = slice(i * sub, (i + 1) * sub)
        x_new = x_ref[rows, :] + _rms_norm(y, pg_ref[...])
        o_ref[rows, :] = x_new
        if emit_h:
            h_ref = rest[0]
            h_ref[rows, :] = _rms_norm(x_new, gn_ref[...]).astype(h_ref.dtype)

    y_prev = project(0)
    for i in range(1, n_sub):
        y_cur = project(i)
        finish(i - 1, y_prev)
        y_prev = y_cur
    finish(n_sub - 1, y_prev)


def _out_proj_sgu(ya, yc, z2, col0, layer, w, post_g, x2, pre_g, ln_g, ln_b, w_s, b_map, *, emit_h,
                  tm=512, n_sub=2):
    n_tok = x2.shape[0]
    assert tm % (n_sub * CHUNK) == 0 and (4 * D_A - col0) % D_B == 0 and n_tok // tm >= 2
    c0 = (4 * D_A - col0) // D_B
    const2 = lambda i: (0, 0)
    of_layer = lambda *dims: pl.BlockSpec((None,) + dims, lambda i: (layer,) + (0,) * len(dims))
    tile = lambda width, col=0: pl.BlockSpec((tm, width), lambda i: (i, col))
    out_shape = [jax.ShapeDtypeStruct((n_tok, D_MODEL), f32)]
    out_specs = [tile(D_MODEL)]
    if emit_h:
        out_shape.append(jax.ShapeDtypeStruct((n_tok, D_MODEL), bf16))
        out_specs.append(tile(D_MODEL))
    return pl.pallas_call(
        functools.partial(_out_sgu_kernel, tm=tm, n_sub=n_sub, emit_h=emit_h, layer=layer),
        grid=(n_tok // tm,),
        in_specs=[
            tile(D_A), tile(D_C),
            pl.BlockSpec((D_MIX, D_MODEL), const2),
            _whole(post_g),
            pl.BlockSpec(memory_space=pl.ANY),
            _whole(pre_g),
            tile(D_B, c0), tile(D_B, c0 + 1), tile(D_B, c0 + 2),
            _whole(ln_g), _whole(ln_b),
            of_layer(B_GROUPS, CHUNK, CHUNK),
            of_layer(CHUNK, D_B),
        ],
        out_specs=out_specs,
        out_shape=out_shape,
        scratch_shapes=[pltpu.VMEM((X_SLOTS, tm, D_MODEL), f32), pltpu.SemaphoreType.DMA((X_SLOTS,)),
                        pltpu.VMEM((tm, D_MIX), bf16)],
        compiler_params=_params("arbitrary"),
        name="out_proj_sgu",
    )(ya, yc, w, post_g, x2, pre_g, z2, z2, z2, ln_g, ln_b, w_s, b_map)


def kernel(x, pre_norm_g, w_in, attn_rpb, sgu_ln_g, sgu_ln_b, sgu_w, sgu_b, conv_w, conv_b,
           conv_ln_g, conv_ln_b, conv_pw_w, conv_pw_b, w_out, post_norm_g):
    bsz, t, d = x.shape
    depth = w_in.shape[0]
    assert d == D_MODEL and t % GRID_W == 0 and t % CHUNK == 0
    n_tok = bsz * t
    x2 = x.reshape(n_tok, d)
    vecs = _attention_bias_vectors(attn_rpb)
    b_map = jnp.repeat(jnp.swapaxes(sgu_b.astype(f32), 1, 2), D_B // B_GROUPS, axis=2)
    conv_w3 = conv_w.reshape(depth, CONV_W, D_C).astype(f32)
    as_seq = lambda z: z.reshape(bsz, t, z.shape[-1])
    h, zq = _norm_proj(x2, pre_norm_g.astype(f32), w_in, 0, tn=D_A)
    for l in range(depth):
        col0 = D_A if l == 0 else 0
        zr = _in_proj(h, w_in, l, col0=col0)
        if l > 0:
            zq = zr
        y_a, w_out_bf = _attention(as_seq(zq), as_seq(zr), col0, vecs, w_out, l)
        y_c = _conformer_conv(as_seq(zr), col0, l, conv_w3, conv_b.astype(f32), conv_ln_g.astype(f32),
                              conv_ln_b.astype(f32), conv_pw_w.astype(f32), conv_pw_b.astype(f32))
        last = l == depth - 1
        outs = _out_proj_sgu(
            y_a.reshape(n_tok, D_A), y_c.reshape(n_tok, D_C), zr, col0, l, w_out_bf,
            post_norm_g.astype(f32), x2, pre_norm_g.astype(f32), sgu_ln_g.astype(f32), sgu_ln_b.astype(f32),
            sgu_w.astype(f32), b_map, emit_h=not last)
        x2 = outs[0]
        if not last:
            h = outs[1]
    return x2.reshape(bsz, t, d)
```

```python
import functools
import math

import jax
import jax.numpy as jnp
from jax import lax
from jax.experimental import pallas as pl
from jax.experimental.pallas import tpu as pltpu

D_MODEL = 2048
GRID_W = 64
WIN_H = 8
WIN_W = 16
HEAD_DIM = 128
D_A = 1024
N_HEADS = 8
CHUNK = 128
D_B = 512
B_GROUPS = 4
D_C = 512
CONV_W = 31
D_MIX = D_A + D_B + D_C
D_IN = 4 * D_A + 3 * D_B + 3 * D_C
EPS = 1e-6
MASKED = -1e30

LANES = 128
SUBLANES = 8
VMEM_LIMIT = 56 * 1024 * 1024

Q_ROWS = 4
K_ROWS = Q_ROWS + WIN_H
PAIR = 2 * GRID_W
assert PAIR == LANES

CONV_PAD = 16

f32 = jnp.float32
bf16 = jnp.bfloat16


def _sigmoid(x):
    return 1.0 / (1.0 + jnp.exp2(x * -math.log2(math.e)))


def _silu(x):
    return x * _sigmoid(x)


def _gelu(x):
    return 0.5 * x * (1.0 + lax.erf(x * math.sqrt(0.5)))


def _layer_norm(x, g, b):
    mu = jnp.mean(x, axis=-1, keepdims=True)
    xc = x - mu
    var = jnp.mean(xc * xc, axis=-1, keepdims=True)
    return xc * lax.rsqrt(var + EPS) * g + b


def _rms_norm(x, g):
    ms = jnp.mean(x * x, axis=-1, keepdims=True)
    return x * lax.rsqrt(ms + EPS) * g


def _params(*semantics):
    return pltpu.CompilerParams(dimension_semantics=semantics, vmem_limit_bytes=VMEM_LIMIT)


def _whole(arr):
    return pl.BlockSpec(arr.shape, lambda *_: (0,) * arr.ndim)


def _layer_row(ref, layer):
    return ref.at[layer:layer + 1, :]


def _norm_proj_kernel(x_ref, g_ref, w_ref, h_ref, z_ref, wb_scr, *, layer):
    @pl.when(pl.program_id(0) == 0)
    def _():
        wb_scr[...] = w_ref[...].astype(bf16)

    h = _rms_norm(x_ref[...], _layer_row(g_ref, layer)[...]).astype(bf16)
    h_ref[...] = h
    z_ref[...] = jnp.dot(h, wb_scr[...], preferred_element_type=f32).astype(z_ref.dtype)


def _norm_proj(x2, g, w_in, layer, tm=1024, tn=1024):
    n_tok = x2.shape[0]
    return pl.pallas_call(
        functools.partial(_norm_proj_kernel, layer=layer),
        grid=(n_tok // tm,),
        in_specs=[pl.BlockSpec((tm, D_MODEL), lambda i: (i, 0)),
                  _whole(g),
                  pl.BlockSpec((None, D_MODEL, tn), lambda i: (layer, 0, 0))],
        out_specs=[pl.BlockSpec((tm, D_MODEL), lambda i: (i, 0)),
                   pl.BlockSpec((tm, tn), lambda i: (i, 0))],
        out_shape=[jax.ShapeDtypeStruct((n_tok, D_MODEL), bf16),
                   jax.ShapeDtypeStruct((n_tok, tn), bf16)],
        scratch_shapes=[pltpu.VMEM((D_MODEL, tn), bf16)],
        compiler_params=_params("arbitrary"),
        name="norm_proj",
    )(x2, g, w_in)


def _in_proj_kernel(h_ref, w_ref, z_ref, wb_scr):
    @pl.when(pl.program_id(1) == 0)
    def _():
        wb_scr[...] = w_ref[...].astype(bf16)

    z_ref[...] = jnp.dot(h_ref[...], wb_scr[...], preferred_element_type=f32).astype(z_ref.dtype)


def _in_proj(h, w_in, layer, col0=0, tm=2048, tn=1024):
    n_tok = h.shape[0]
    assert col0 % tn == 0 and (D_IN - col0) % tn == 0
    return pl.pallas_call(
        _in_proj_kernel,
        grid=((D_IN - col0) // tn, n_tok // tm),
        in_specs=[
            pl.BlockSpec((tm, D_MODEL), lambda j, i: (i, 0)),
            pl.BlockSpec((None, D_MODEL, tn), lambda j, i: (layer, 0, col0 // tn + j)),
        ],
        out_specs=pl.BlockSpec((tm, tn), lambda j, i: (i, j)),
        out_shape=jax.ShapeDtypeStruct((n_tok, D_IN - col0), bf16),
        scratch_shapes=[pltpu.VMEM((D_MODEL, tn), bf16)],
        compiler_params=_params("arbitrary", "arbitrary"),
        name="in_proj",
    )(h, w_in)


def _attn_kernel(q_ref, k_ref, v_ref, g_ref, vec_ref, wo_ref, o_ref, wob_ref, bias_ref, *, rows, heads):
    wob_ref[...] = wo_ref[...].astype(bf16)
    log2e = math.log2(math.e)
    scale = HEAD_DIM ** -0.5 * log2e
    nq = Q_ROWS * GRID_W
    nk = K_ROWS * GRID_W
    w_idx = lax.broadcasted_iota(jnp.int32, (GRID_W, PAIR), 0)
    lane = lax.broadcasted_iota(jnp.int32, (GRID_W, PAIR), 1)
    left_half = lane < GRID_W
    kc = jnp.where(left_half, lane, lane - GRID_W)
    col_start = jnp.clip(w_idx - WIN_W // 2, 0, GRID_W - WIN_W)
    in_win = (kc >= col_start) & (kc < col_start + WIN_W)
    for hh in range(heads):
        for e in range(2 * WIN_H):
            vec = jnp.broadcast_to(vec_ref[hh, e:e + 1, :], (GRID_W, PAIR))
            bias = pltpu.roll(vec, 0, 1, stride=1, stride_axis=0) * log2e
            bias_ref[hh, e] = jnp.where(in_win, bias, MASKED)

    n_blk = rows // Q_ROWS
    first_key_row = lambda blk: min(max(blk * Q_ROWS - WIN_H // 2, 0), rows - K_ROWS)

    head_cols = lambda hh: slice(hh * HEAD_DIM, (hh + 1) * HEAD_DIM)
    items = [(hh, blk) for hh in range(heads) for blk in range(n_blk)]

    def scores(item):
        hh, blk = item
        q0, k0 = blk * nq, first_key_row(blk) * GRID_W
        q = (q_ref[0, q0:q0 + nq, head_cols(hh)].astype(f32) * scale).astype(bf16)
        kw = k_ref[0, k0:k0 + nk, head_cols(hh)]
        return lax.dot_general(q, kw, (((1,), (1,)), ((), ())), preferred_element_type=f32)

    s_next = scores(items[0])
    for pos, (hh, blk) in enumerate(items):
        ks = first_key_row(blk)
        q0, k0 = blk * nq, ks * GRID_W
        vw = v_ref[0, k0:k0 + nk, head_cols(hh)]
        s = s_next
        if pos + 1 < len(items):
            s_next = scores(items[pos + 1])

        p_rows = []
        for rq in range(Q_ROWS):
            r = blk * Q_ROWS + rq
            rs = min(max(r - WIN_H // 2, 0), rows - WIN_H)
            pieces = {}
            for ip in range(K_ROWS // 2):
                kr = ks + 2 * ip
                ok0, ok1 = rs <= kr < rs + WIN_H, rs <= kr + 1 < rs + WIN_H
                if not (ok0 or ok1):
                    continue
                dr0 = kr - r + (WIN_H - 1)
                sub = s[rq * GRID_W:(rq + 1) * GRID_W, ip * PAIR:(ip + 1) * PAIR]
                if ok0 and ok1:
                    bias = bias_ref[hh, dr0 + 1]
                elif ok0:
                    bias = jnp.where(left_half, bias_ref[hh, dr0 + 1], MASKED)
                else:
                    bias = jnp.where(left_half, MASKED, bias_ref[hh, dr0 + 1])
                pieces[ip] = sub + bias
            m = jnp.max(functools.reduce(jnp.maximum, pieces.values()), axis=-1, keepdims=True)
            pieces = {ip: jnp.exp2(sv - m) for ip, sv in pieces.items()}
            zero = jnp.zeros((GRID_W, PAIR), bf16)
            p_rows.append(jnp.concatenate(
                [pieces[ip].astype(bf16) if ip in pieces else zero for ip in range(K_ROWS // 2)], axis=1))
        p = jnp.concatenate(p_rows, axis=0)
        v_ext = jnp.concatenate([vw, jnp.ones((nk, HEAD_DIM), bf16)], axis=1)
        o = jnp.dot(p, v_ext, preferred_element_type=f32)
        for rq in range(Q_ROWS):
            rows_q = slice(q0 + rq * GRID_W, q0 + (rq + 1) * GRID_W)
            g = g_ref[0, rows_q, head_cols(hh)].astype(f32)
            o_rq = o[rq * GRID_W:(rq + 1) * GRID_W]
            o_rq = o_rq[:, :HEAD_DIM] * (1.0 / o_rq[:, HEAD_DIM:])
            o_ref[0, rows_q, head_cols(hh)] = (o_rq * _silu(g)).astype(o_ref.dtype)


def _attention(zq3, zr3, col0, vecs, w_out, layer, heads=2):
    bsz, t, _ = zq3.shape
    rows = t // GRID_W
    assert rows % Q_ROWS == 0 and rows >= K_ROWS and N_HEADS % heads == 0
    groups = N_HEADS // heads
    slab = D_MIX // (bsz * groups)
    assert slab * bsz * groups == D_MIX and slab % (2 * SUBLANES) == 0
    blk = (1, t, heads * HEAD_DIM)
    assert (D_A - col0) % blk[2] == 0
    k0 = (D_A - col0) // blk[2]
    return pl.pallas_call(
        functools.partial(_attn_kernel, rows=rows, heads=heads),
        grid=(bsz, groups),
        in_specs=[
            pl.BlockSpec(blk, lambda b, h: (b, 0, h)),
            pl.BlockSpec(blk, lambda b, h: (b, 0, k0 + h)),
            pl.BlockSpec(blk, lambda b, h: (b, 0, k0 + groups + h)),
            pl.BlockSpec(blk, lambda b, h: (b, 0, k0 + 2 * groups + h)),
            pl.BlockSpec((None, heads, 2 * WIN_H, PAIR), lambda b, h: (layer, h, 0, 0)),
            pl.BlockSpec((None, slab, D_MODEL), lambda b, h: (layer, b * groups + h, 0)),
        ],
        out_specs=[pl.BlockSpec(blk, lambda b, h: (b, 0, h)),
                   pl.BlockSpec((slab, D_MODEL), lambda b, h: (b * groups + h, 0))],
        out_shape=[jax.ShapeDtypeStruct((bsz, t, D_A), bf16),
                   jax.ShapeDtypeStruct((D_MIX, D_MODEL), bf16)],
        scratch_shapes=[pltpu.VMEM((heads, 2 * WIN_H, GRID_W, PAIR), f32)],
        compiler_params=_params("arbitrary", "arbitrary"),
        name="nbr_attention",
    )(zq3, zr3, zr3, zr3, vecs, w_out)


def _attention_bias_vectors(rpb):
    half = WIN_W - 1
    lead = rpb.shape[:-2]
    pad = jnp.full(lead + (1, 2 * WIN_W - 1), MASKED, f32)
    ext = jnp.concatenate([pad, rpb.astype(f32), pad], axis=-2)
    lower, upper = ext[..., :-1, :], ext[..., 1:, :]
    gap = jnp.full(lead + (2 * WIN_H, GRID_W - 2 * half - 1), MASKED, f32)
    return jnp.concatenate([lower[..., half:], gap, upper, gap, lower[..., :half]], axis=-1)


def _conv_kernel(a_ref, b_ref, g_ref, cw_ref, cb_ref, lng_ref, lnb_ref, pw_ref, pb_ref, o_ref,
                 h_scr, c_scr, *, t, tt, tm, layer):
    cb_ref, lng_ref, lnb_ref, pb_ref = (_layer_row(r, layer) for r in (cb_ref, lng_ref, lnb_ref, pb_ref))
    zeros = jnp.zeros((CONV_PAD, D_C), f32)
    h_scr[pl.ds(0, CONV_PAD), :] = zeros
    h_scr[pl.ds(CONV_PAD + t, CONV_PAD), :] = zeros
    h_scr[pl.ds(CONV_PAD, t), :] = a_ref[0].astype(f32) * _sigmoid(b_ref[0].astype(f32))

    shift = CONV_PAD - CONV_W // 2
    n_al = (CONV_W + shift + SUBLANES - 1) // SUBLANES
    win_rows = tt + n_al * SUBLANES

    def conv_tile(ti, carry):
        t0 = pl.multiple_of(ti * tt, tt)
        for cb in range(D_C // LANES):
            cols = slice(cb * LANES, (cb + 1) * LANES)
            win = h_scr[pl.ds(t0, win_rows), cols]
            acc = None
            for b in range(SUBLANES):
                part = None
                for a in range(n_al):
                    k = SUBLANES * a + b - shift
                    if 0 <= k < CONV_W:
                        term = cw_ref[k, :, cols] * win[SUBLANES * a:SUBLANES * a + tt + SUBLANES]
                        part = term if part is None else part + term
                part = part[b:b + tt]
                acc = part if acc is None else acc + part
            c_scr[pl.ds(t0, tt), cols] = acc
        return carry

    lax.fori_loop(0, t // tt, conv_tile, 0)

    def post_tile(ti, carry):
        t0 = pl.multiple_of(ti * tm, tm)
        c = c_scr[pl.ds(t0, tm), :] + cb_ref[...]
        c = _silu(_layer_norm(c, lng_ref[...], lnb_ref[...])).astype(bf16)
        y = jnp.dot(c, pw_ref[...].astype(bf16), preferred_element_type=f32) + pb_ref[...]
        g = g_ref[0, pl.ds(t0, tm), :].astype(f32)
        o_ref[0, pl.ds(t0, tm), :] = (y * _silu(g)).astype(o_ref.dtype)
        return carry

    lax.fori_loop(0, t // tm, post_tile, 0)


def _conformer_conv(z3, col0, layer, conv_w, conv_b, ln_g, ln_b, pw_w, pw_b, tt=128, tm=1024):
    bsz, t, _ = z3.shape
    assert t % tt == 0 and t % tm == 0 and (4 * D_A + 3 * D_B - col0) % D_C == 0
    c0 = (4 * D_A + 3 * D_B - col0) // D_C
    blk = (1, t, D_C)
    of_layer = lambda *dims: pl.BlockSpec((None,) + dims, lambda b: (layer,) + (0,) * len(dims))
    return pl.pallas_call(
        functools.partial(_conv_kernel, t=t, tt=tt, tm=tm, layer=layer),
        grid=(bsz,),
        in_specs=[
            pl.BlockSpec(blk, lambda b: (b, 0, c0)),
            pl.BlockSpec(blk, lambda b: (b, 0, c0 + 1)),
            pl.BlockSpec(blk, lambda b: (b, 0, c0 + 2)),
            of_layer(CONV_W, 1, D_C), _whole(conv_b), _whole(ln_g), _whole(ln_b), of_layer(D_C, D_C), _whole(pw_b),
        ],
        out_specs=pl.BlockSpec(blk, lambda b: (b, 0, 0)),
        out_shape=jax.ShapeDtypeStruct((bsz, t, D_C), bf16),
        scratch_shapes=[pltpu.VMEM((t + 2 * CONV_PAD, D_C), f32), pltpu.VMEM((t, D_C), f32)],
        compiler_params=_params("arbitrary"),
        name="conformer_conv",
    )(z3, z3, z3, conv_w, conv_b, ln_g, ln_b, pw_w, pw_b)


def _spatial_gating(u_ref, v_ref, g_ref, lng_ref, lnb_ref, ws_ref, bs_ref, o_ref, col0, r0, nr):
    gd = D_B // B_GROUPS
    for n in range(r0 // CHUNK, (r0 + nr) // CHUNK):
        rows = slice(n * CHUNK, (n + 1) * CHUNK)
        v = _layer_norm(_gelu(v_ref[rows, :].astype(f32)), lng_ref[...], lnb_ref[...]).astype(bf16)
        for grp in range(B_GROUPS):
            cols = slice(grp * gd, (grp + 1) * gd)
            s = jnp.dot(ws_ref[grp].astype(bf16), v[:, cols], preferred_element_type=f32) + bs_ref[:, cols]
            u = _gelu(u_ref[rows, cols].astype(f32))
            g = g_ref[rows, cols].astype(f32)
            o_ref[rows, col0 + grp * gd:col0 + (grp + 1) * gd] = (u * s * _silu(g)).astype(o_ref.dtype)


def _out_sgu_kernel(ya_ref, yc_ref, w_ref, pg_ref, x_ref, gn_ref,
                    u_ref, v_ref, g_ref, lng_ref, lnb_ref, ws_ref, bs_ref, o_ref, *rest,
                    tm, n_sub, emit_h, layer):
    pg_ref, lng_ref, lnb_ref = (_layer_row(r, layer) for r in (pg_ref, lng_ref, lnb_ref))
    gn_ref = _layer_row(gn_ref, layer + 1 if emit_h else layer)
    cat_scr = rest[-1]
    sub = tm // n_sub

    def project(i):
        rows = slice(i * sub, (i + 1) * sub)
        cat_scr[rows, 0:D_A] = ya_ref[rows, :]
        _spatial_gating(u_ref, v_ref, g_ref, lng_ref, lnb_ref, ws_ref, bs_ref, cat_scr, D_A, i * sub, sub)
        cat_scr[rows, D_A + D_B:D_MIX] = yc_ref[rows, :]
        return jnp.dot(cat_scr[rows, :], w_ref[...], preferred_element_type=f32)

    def finish(i, y):
        rows = slice(i * sub, (i + 1) * sub)
        x_new = x_ref[rows, :] + _rms_norm(y, pg_ref[...])
        o_ref[rows, :] = x_new
        if emit_h:
            h_ref = rest[0]
            h_ref[rows, :] = _rms_norm(x_new, gn_ref[...]).astype(h_ref.dtype)

    y_prev = project(0)
    for i in range(1, n_sub):
        y_cur = project(i)
        finish(i - 1, y_prev)
        y_prev = y_cur
    finish(n_sub - 1, y_prev)


def _out_proj_sgu(ya, yc, z2, col0, layer, w, post_g, x2, pre_g, ln_g, ln_b, w_s, b_map, *, emit_h,
                  tm=512, n_sub=2):
    n_tok = x2.shape[0]
    assert tm % (n_sub * CHUNK) == 0 and (4 * D_A - col0) % D_B == 0
    c0 = (4 * D_A - col0) // D_B
    const2 = lambda i: (0, 0)
    of_layer = lambda *dims: pl.BlockSpec((None,) + dims, lambda i: (layer,) + (0,) * len(dims))
    tile = lambda width, col=0: pl.BlockSpec((tm, width), lambda i: (i, col))
    out_shape = [jax.ShapeDtypeStruct((n_tok, D_MODEL), f32)]
    out_specs = [tile(D_MODEL)]
    if emit_h:
        out_shape.append(jax.ShapeDtypeStruct((n_tok, D_MODEL), bf16))
        out_specs.append(tile(D_MODEL))
    return pl.pallas_call(
        functools.partial(_out_sgu_kernel, tm=tm, n_sub=n_sub, emit_h=emit_h, layer=layer),
        grid=(n_tok // tm,),
        in_specs=[
            tile(D_A), tile(D_C),
            pl.BlockSpec((D_MIX, D_MODEL), const2),
            _whole(post_g),
            tile(D_MODEL),
            _whole(pre_g),
            tile(D_B, c0), tile(D_B, c0 + 1), tile(D_B, c0 + 2),
            _whole(ln_g), _whole(ln_b),
            of_layer(B_GROUPS, CHUNK, CHUNK),
            of_layer(CHUNK, D_B),
        ],
        out_specs=out_specs,
        out_shape=out_shape,
        scratch_shapes=[pltpu.VMEM((tm, D_MIX), bf16)],
        compiler_params=_params("arbitrary"),
        name="out_proj_sgu",
    )(ya, yc, w, post_g, x2, pre_g, z2, z2, z2, ln_g, ln_b, w_s, b_map)


def kernel(x, pre_norm_g, w_in, attn_rpb, sgu_ln_g, sgu_ln_b, sgu_w, sgu_b, conv_w, conv_b,
           conv_ln_g, conv_ln_b, conv_pw_w, conv_pw_b, w_out, post_norm_g):
    bsz, t, d = x.shape
    depth = w_in.shape[0]
    assert d == D_MODEL and t % GRID_W == 0 and t % CHUNK == 0
    n_tok = bsz * t
    x2 = x.reshape(n_tok, d)
    vecs = _attention_bias_vectors(attn_rpb)
    b_map = jnp.repeat(jnp.swapaxes(sgu_b.astype(f32), 1, 2), D_B // B_GROUPS, axis=2)
    as_seq = lambda z: z.reshape(bsz, t, z.shape[-1])
    h, zq = _norm_proj(x2, pre_norm_g.astype(f32), w_in, 0, tn=D_A)
    for l in range(depth):
        col0 = D_A if l == 0 else 0
        zr = _in_proj(h, w_in, l, col0=col0)
        if l > 0:
            zq = zr
        y_a, w_out_bf = _attention(as_seq(zq), as_seq(zr), col0, vecs, w_out, l)
        y_c = _conformer_conv(as_seq(zr), col0, l, conv_w.astype(f32), conv_b.astype(f32), conv_ln_g.astype(f32),
                              conv_ln_b.astype(f32), conv_pw_w.astype(f32), conv_pw_b.astype(f32))
        last = l == depth - 1
        outs = _out_proj_sgu(
            y_a.reshape(n_tok, D_A), y_c.reshape(n_tok, D_C), zr, col0, l, w_out_bf,
            post_norm_g.astype(f32), x2, pre_norm_g.astype(f32), sgu_ln_g.astype(f32), sgu_ln_b.astype(f32),
            sgu_w.astype(f32), b_map, emit_h=not last)
        x2 = outs[0]
        if not last:
            h = outs[1]
    return x2.reshape(bsz, t, d)
```

```python
import functools
import math

import jax
import jax.numpy as jnp
import numpy as np
from jax import lax
from jax.experimental import pallas as pl
from jax.experimental.pallas import tpu as pltpu

D_MODEL = 2048
GRID_W = 64
WIN_H = 8
WIN_W = 16
HEAD_DIM = 128
D_A = 1024
N_HEADS = 8
CHUNK = 128
D_B = 512
B_GROUPS = 4
D_C = 512
CONV_W = 31
D_MIX = D_A + D_B + D_C
D_IN = 4 * D_A + 3 * D_B + 3 * D_C
EPS = 1e-6
MASKED = -1e30

LANES = 128
SUBLANES = 8
VMEM_LIMIT = 56 * 1024 * 1024

Q_ROWS = 4
K_ROWS = Q_ROWS + WIN_H
PAIR = 2 * GRID_W
assert PAIR == LANES

CONV_PAD = 16
CONV_TAIL = 24

f32 = jnp.float32
bf16 = jnp.bfloat16


def _sigmoid(x):
    return 1.0 / (1.0 + jnp.exp2(x * -math.log2(math.e)))


def _silu(x):
    return x * _sigmoid(x)


def _gelu(x):
    return 0.5 * x * (1.0 + lax.erf(x * math.sqrt(0.5)))


def _layer_norm(x, g, b):
    mu = jnp.mean(x, axis=-1, keepdims=True)
    xc = x - mu
    var = jnp.mean(xc * xc, axis=-1, keepdims=True)
    return xc * lax.rsqrt(var + EPS) * g + b


def _rms_norm(x, g):
    ms = jnp.mean(x * x, axis=-1, keepdims=True)
    return x * lax.rsqrt(ms + EPS) * g


def _params(*semantics):
    return pltpu.CompilerParams(dimension_semantics=semantics, vmem_limit_bytes=VMEM_LIMIT)


def _whole(arr):
    return pl.BlockSpec(arr.shape, lambda *_: (0,) * arr.ndim)


def _layer_row(ref, layer):
    return ref.at[layer:layer + 1, :]


def _norm_proj_kernel(x_ref, g_ref, w_ref, h_ref, z_ref, wb_scr, *, layer):
    @pl.when(pl.program_id(0) == 0)
    def _():
        wb_scr[...] = w_ref[...].astype(bf16)

    h = _rms_norm(x_ref[...], _layer_row(g_ref, layer)[...]).astype(bf16)
    h_ref[...] = h
    z_ref[...] = jnp.dot(h, wb_scr[...], preferred_element_type=f32).astype(z_ref.dtype)


def _norm_proj(x2, g, w_in, layer, tm=1024, tn=1024):
    n_tok = x2.shape[0]
    return pl.pallas_call(
        functools.partial(_norm_proj_kernel, layer=layer),
        grid=(n_tok // tm,),
        in_specs=[pl.BlockSpec((tm, D_MODEL), lambda i: (i, 0)),
                  _whole(g),
                  pl.BlockSpec((None, D_MODEL, tn), lambda i: (layer, 0, 0))],
        out_specs=[pl.BlockSpec((tm, D_MODEL), lambda i: (i, 0)),
                   pl.BlockSpec((tm, tn), lambda i: (i, 0))],
        out_shape=[jax.ShapeDtypeStruct((n_tok, D_MODEL), bf16),
                   jax.ShapeDtypeStruct((n_tok, tn), bf16)],
        scratch_shapes=[pltpu.VMEM((D_MODEL, tn), bf16)],
        compiler_params=_params("arbitrary"),
        name="norm_proj",
    )(x2, g, w_in)


def _in_proj_kernel(h_ref, w_ref, z_ref, wb_scr):
    @pl.when(pl.program_id(1) == 0)
    def _():
        wb_scr[...] = w_ref[...].astype(bf16)

    z_ref[...] = jnp.dot(h_ref[...], wb_scr[...], preferred_element_type=f32).astype(z_ref.dtype)


def _in_proj(h, w_in, layer, col0=0, tm=2048, tn=1024):
    n_tok = h.shape[0]
    assert col0 % tn == 0 and (D_IN - col0) % tn == 0
    return pl.pallas_call(
        _in_proj_kernel,
        grid=((D_IN - col0) // tn, n_tok // tm),
        in_specs=[
            pl.BlockSpec((tm, D_MODEL), lambda j, i: (i, 0)),
            pl.BlockSpec((None, D_MODEL, tn), lambda j, i: (layer, 0, col0 // tn + j)),
        ],
        out_specs=pl.BlockSpec((tm, tn), lambda j, i: (i, j)),
        out_shape=jax.ShapeDtypeStruct((n_tok, D_IN - col0), bf16),
        scratch_shapes=[pltpu.VMEM((D_MODEL, tn), bf16)],
        compiler_params=_params("arbitrary", "arbitrary"),
        name="in_proj",
    )(h, w_in)


def _attn_kernel(q_ref, k_ref, v_ref, g_ref, vec_ref, wo_ref, o_ref, wob_ref, bias_ref, *, rows, heads):
    wob_ref[...] = wo_ref[...].astype(bf16)
    log2e = math.log2(math.e)
    scale = HEAD_DIM ** -0.5 * log2e
    nq = Q_ROWS * GRID_W
    nk = K_ROWS * GRID_W
    w_idx = lax.broadcasted_iota(jnp.int32, (GRID_W, PAIR), 0)
    lane = lax.broadcasted_iota(jnp.int32, (GRID_W, PAIR), 1)
    left_half = lane < GRID_W
    kc = jnp.where(left_half, lane, lane - GRID_W)
    col_start = jnp.clip(w_idx - WIN_W // 2, 0, GRID_W - WIN_W)
    in_win = (kc >= col_start) & (kc < col_start + WIN_W)
    for hh in range(heads):
        for e in range(2 * WIN_H):
            vec = jnp.broadcast_to(vec_ref[hh, e:e + 1, :], (GRID_W, PAIR))
            bias = pltpu.roll(vec, 0, 1, stride=1, stride_axis=0) * log2e
            bias_ref[hh, e] = jnp.where(in_win, bias, MASKED)

    n_blk = rows // Q_ROWS
    first_key_row = lambda blk: min(max(blk * Q_ROWS - WIN_H // 2, 0), rows - K_ROWS)

    head_cols = lambda hh: slice(hh * HEAD_DIM, (hh + 1) * HEAD_DIM)
    items = [(hh, blk) for hh in range(heads) for blk in range(n_blk)]

    def scores(item):
        hh, blk = item
        q0, k0 = blk * nq, first_key_row(blk) * GRID_W
        q = (q_ref[0, q0:q0 + nq, head_cols(hh)].astype(f32) * scale).astype(bf16)
        kw = k_ref[0, k0:k0 + nk, head_cols(hh)]
        return lax.dot_general(q, kw, (((1,), (1,)), ((), ())), preferred_element_type=f32)

    s_next = scores(items[0])
    for pos, (hh, blk) in enumerate(items):
        ks = first_key_row(blk)
        q0, k0 = blk * nq, ks * GRID_W
        vw = v_ref[0, k0:k0 + nk, head_cols(hh)]
        s = s_next
        if pos + 1 < len(items):
            s_next = scores(items[pos + 1])

        p_rows = []
        for rq in range(Q_ROWS):
            r = blk * Q_ROWS + rq
            rs = min(max(r - WIN_H // 2, 0), rows - WIN_H)
            pieces = {}
            for ip in range(K_ROWS // 2):
                kr = ks + 2 * ip
                ok0, ok1 = rs <= kr < rs + WIN_H, rs <= kr + 1 < rs + WIN_H
                if not (ok0 or ok1):
                    continue
                dr0 = kr - r + (WIN_H - 1)
                sub = s[rq * GRID_W:(rq + 1) * GRID_W, ip * PAIR:(ip + 1) * PAIR]
                if ok0 and ok1:
                    bias = bias_ref[hh, dr0 + 1]
                elif ok0:
                    bias = jnp.where(left_half, bias_ref[hh, dr0 + 1], MASKED)
                else:
                    bias = jnp.where(left_half, MASKED, bias_ref[hh, dr0 + 1])
                pieces[ip] = sub + bias
            m = jnp.max(functools.reduce(jnp.maximum, pieces.values()), axis=-1, keepdims=True)
            pieces = {ip: jnp.exp2(sv - m) for ip, sv in pieces.items()}
            zero = jnp.zeros((GRID_W, PAIR), bf16)
            p_rows.append(jnp.concatenate(
                [pieces[ip].astype(bf16) if ip in pieces else zero for ip in range(K_ROWS // 2)], axis=1))
        p = jnp.concatenate(p_rows, axis=0)
        v_ext = jnp.concatenate([vw, jnp.ones((nk, HEAD_DIM), bf16)], axis=1)
        o = jnp.dot(p, v_ext, preferred_element_type=f32)
        for rq in range(Q_ROWS):
            rows_q = slice(q0 + rq * GRID_W, q0 + (rq + 1) * GRID_W)
            g = g_ref[0, rows_q, head_cols(hh)].astype(f32)
            o_rq = o[rq * GRID_W:(rq + 1) * GRID_W]
            o_rq = o_rq[:, :HEAD_DIM] * (1.0 / o_rq[:, HEAD_DIM:])
            o_ref[0, rows_q, head_cols(hh)] = (o_rq * _silu(g)).astype(o_ref.dtype)


def _attention(zq3, zr3, col0, vecs, w_out, layer, heads=2):
    bsz, t, _ = zq3.shape
    rows = t // GRID_W
    assert rows % Q_ROWS == 0 and rows >= K_ROWS and N_HEADS % heads == 0
    groups = N_HEADS // heads
    slab = D_MIX // (bsz * groups)
    assert slab * bsz * groups == D_MIX and slab % (2 * SUBLANES) == 0
    blk = (1, t, heads * HEAD_DIM)
    assert (D_A - col0) % blk[2] == 0
    k0 = (D_A - col0) // blk[2]
    return pl.pallas_call(
        functools.partial(_attn_kernel, rows=rows, heads=heads),
        grid=(bsz, groups),
        in_specs=[
            pl.BlockSpec(blk, lambda b, h: (b, 0, h)),
            pl.BlockSpec(blk, lambda b, h: (b, 0, k0 + h)),
            pl.BlockSpec(blk, lambda b, h: (b, 0, k0 + groups + h)),
            pl.BlockSpec(blk, lambda b, h: (b, 0, k0 + 2 * groups + h)),
            pl.BlockSpec((None, heads, 2 * WIN_H, PAIR), lambda b, h: (layer, h, 0, 0)),
            pl.BlockSpec((None, slab, D_MODEL), lambda b, h: (layer, b * groups + h, 0)),
        ],
        out_specs=[pl.BlockSpec(blk, lambda b, h: (b, 0, h)),
                   pl.BlockSpec((slab, D_MODEL), lambda b, h: (b * groups + h, 0))],
        out_shape=[jax.ShapeDtypeStruct((bsz, t, D_A), bf16),
                   jax.ShapeDtypeStruct((D_MIX, D_MODEL), bf16)],
        scratch_shapes=[pltpu.VMEM((heads, 2 * WIN_H, GRID_W, PAIR), f32)],
        compiler_params=_params("arbitrary", "arbitrary"),
        name="nbr_attention",
    )(zq3, zr3, zr3, zr3, vecs, w_out)


def _attention_bias_vectors(rpb):
    half = WIN_W - 1
    lead = rpb.shape[:-2]
    pad = jnp.full(lead + (1, 2 * WIN_W - 1), MASKED, f32)
    ext = jnp.concatenate([pad, rpb.astype(f32), pad], axis=-2)
    lower, upper = ext[..., :-1, :], ext[..., 1:, :]
    gap = jnp.full(lead + (2 * WIN_H, GRID_W - 2 * half - 1), MASKED, f32)
    return jnp.concatenate([lower[..., half:], gap, upper, gap, lower[..., :half]], axis=-1)


def _conv_kernel(a_ref, b_ref, g_ref, cw_ref, cb_ref, lng_ref, lnb_ref, pw_ref, pb_ref, sel_ref, o_ref,
                 h_scr, c_scr, p_scr, q_scr, *, t, tt, tm, layer):
    cb_ref, lng_ref, lnb_ref, pb_ref = (_layer_row(r, layer) for r in (cb_ref, lng_ref, lnb_ref, pb_ref))
    h_scr[pl.ds(0, CONV_PAD), :] = jnp.zeros((CONV_PAD, D_C), f32)
    h_scr[pl.ds(CONV_PAD + t, CONV_TAIL), :] = jnp.zeros((CONV_TAIL, D_C), f32)
    h_scr[pl.ds(CONV_PAD, t), :] = a_ref[0].astype(f32) * _sigmoid(b_ref[0].astype(f32))

    shift = CONV_PAD - CONV_W // 2
    n_al = (CONV_W + shift + SUBLANES - 1) // SUBLANES
    part_rows = tt + 2 * SUBLANES
    win_rows = (n_al - 1) * SUBLANES + part_rows
    assert CONV_PAD + CONV_TAIL >= win_rows - tt
    n_tiles = t // tt
    assert n_tiles % 2 == 0
    p_scrs = (p_scr, q_scr)

    def shift_sum(slot, t0):
        c_scr[pl.ds(t0, tt), :] = jnp.dot(sel_ref[...], p_scrs[slot][...], preferred_element_type=f32)

    def tap_sums(slot, t0):
        for cb in range(D_C // LANES):
            cols = slice(cb * LANES, (cb + 1) * LANES)
            win = h_scr[pl.ds(t0, win_rows), cols]
            for b in range(SUBLANES):
                part = None
                for a in range(n_al):
                    k = SUBLANES * a + b - shift
                    if 0 <= k < CONV_W:
                        term = cw_ref[k, :, cols] * win[SUBLANES * a:SUBLANES * a + part_rows]
                        part = term if part is None else part + term
                p_scrs[slot][b * part_rows:(b + 1) * part_rows, cols] = part.astype(bf16)

    def tile_pair(i, carry):
        t0 = pl.multiple_of(2 * i * tt, 2 * tt)
        shift_sum(0, t0)
        tap_sums(1, t0 + tt)
        shift_sum(1, t0 + tt)
        tap_sums(0, t0 + 2 * tt)
        return carry

    for scr in p_scrs:
        scr[SUBLANES * part_rows:, :] = jnp.zeros((scr.shape[0] - SUBLANES * part_rows, D_C), bf16)
    def one_tile(i, carry):
        t0 = pl.multiple_of(i * tt, tt)
        tap_sums(0, t0)
        shift_sum(0, t0)
        return carry

    lax.fori_loop(0, n_tiles, one_tile, 0)

    def post_tile(ti, carry):
        t0 = pl.multiple_of(ti * tm, tm)
        c = c_scr[pl.ds(t0, tm), :] + cb_ref[...]
        c = _silu(_layer_norm(c, lng_ref[...], lnb_ref[...])).astype(bf16)
        y = jnp.dot(c, pw_ref[...].astype(bf16), preferred_element_type=f32) + pb_ref[...]
        g = g_ref[0, pl.ds(t0, tm), :].astype(f32)
        o_ref[0, pl.ds(t0, tm), :] = (y * _silu(g)).astype(o_ref.dtype)
        return carry

    lax.fori_loop(0, t // tm, post_tile, 0)


def _conformer_conv(z3, col0, layer, conv_w, conv_b, ln_g, ln_b, pw_w, pw_b, tt=128, tm=1024):
    bsz, t, _ = z3.shape
    assert t % tt == 0 and t % tm == 0 and (4 * D_A + 3 * D_B - col0) % D_C == 0
    c0 = (4 * D_A + 3 * D_B - col0) // D_C
    blk = (1, t, D_C)
    of_layer = lambda *dims: pl.BlockSpec((None,) + dims, lambda b: (layer,) + (0,) * len(dims))
    part_rows = tt + 2 * SUBLANES
    depth = -(-SUBLANES * part_rows // 256) * 256
    sel = np.zeros((tt, depth), np.float32)
    for b in range(SUBLANES):
        sel[np.arange(tt), b * part_rows + np.arange(tt) + b] = 1.0
    sel = jnp.asarray(sel, bf16)
    parts = pltpu.VMEM((depth, D_C), bf16)
    return pl.pallas_call(
        functools.partial(_conv_kernel, t=t, tt=tt, tm=tm, layer=layer),
        grid=(bsz,),
        in_specs=[
            pl.BlockSpec(blk, lambda b: (b, 0, c0)),
            pl.BlockSpec(blk, lambda b: (b, 0, c0 + 1)),
            pl.BlockSpec(blk, lambda b: (b, 0, c0 + 2)),
            of_layer(CONV_W, 1, D_C), _whole(conv_b), _whole(ln_g), _whole(ln_b), of_layer(D_C, D_C), _whole(pw_b),
            _whole(sel),
        ],
        out_specs=pl.BlockSpec(blk, lambda b: (b, 0, 0)),
        out_shape=jax.ShapeDtypeStruct((bsz, t, D_C), bf16),
        scratch_shapes=[pltpu.VMEM((t + CONV_PAD + CONV_TAIL, D_C), f32), pltpu.VMEM((t, D_C), f32),
                        parts, parts],
        compiler_params=_params("arbitrary"),
        name="conformer_conv",
    )(z3, z3, z3, conv_w, conv_b, ln_g, ln_b, pw_w, pw_b, sel)


def _spatial_gating(u_ref, v_ref, g_ref, lng_ref, lnb_ref, ws_ref, bs_ref, o_ref, col0, r0, nr):
    gd = D_B // B_GROUPS
    for n in range(r0 // CHUNK, (r0 + nr) // CHUNK):
        rows = slice(n * CHUNK, (n + 1) * CHUNK)
        v = _layer_norm(_gelu(v_ref[rows, :].astype(f32)), lng_ref[...], lnb_ref[...]).astype(bf16)
        for grp in range(B_GROUPS):
            cols = slice(grp * gd, (grp + 1) * gd)
            s = jnp.dot(ws_ref[grp].astype(bf16), v[:, cols], preferred_element_type=f32) + bs_ref[:, cols]
            u = _gelu(u_ref[rows, cols].astype(f32))
            g = g_ref[rows, cols].astype(f32)
            o_ref[rows, col0 + grp * gd:col0 + (grp + 1) * gd] = (u * s * _silu(g)).astype(o_ref.dtype)


def _out_sgu_kernel(ya_ref, yc_ref, w_ref, pg_ref, x_ref, gn_ref,
                    u_ref, v_ref, g_ref, lng_ref, lnb_ref, ws_ref, bs_ref, o_ref, *rest,
                    tm, n_sub, emit_h, layer):
    pg_ref, lng_ref, lnb_ref = (_layer_row(r, layer) for r in (pg_ref, lng_ref, lnb_ref))
    gn_ref = _layer_row(gn_ref, layer + 1 if emit_h else layer)
    cat_scr = rest[-1]
    sub = tm // n_sub

    def project(i):
        rows = slice(i * sub, (i + 1) * sub)
        cat_scr[rows, 0:D_A] = ya_ref[rows, :]
        _spatial_gating(u_ref, v_ref, g_ref, lng_ref, lnb_ref, ws_ref, bs_ref, cat_scr, D_A, i * sub, sub)
        cat_scr[rows, D_A + D_B:D_MIX] = yc_ref[rows, :]
        return jnp.dot(cat_scr[rows, :], w_ref[...], preferred_element_type=f32)

    def finish(i, y):
        rows = slice(i * sub, (i + 1) * sub)
        x_new = x_ref[rows, :] + _rms_norm(y, pg_ref[...])
        o_ref[rows, :] = x_new
        if emit_h:
            h_ref = rest[0]
            h_ref[rows, :] = _rms_norm(x_new, gn_ref[...]).astype(h_ref.dtype)

    y_prev = project(0)
    for i in range(1, n_sub):
        y_cur = project(i)
        finish(i - 1, y_prev)
        y_prev = y_cur
    finish(n_sub - 1, y_prev)


def _out_proj_sgu(ya, yc, z2, col0, layer, w, post_g, x2, pre_g, ln_g, ln_b, w_s, b_map, *, emit_h,
                  tm=512, n_sub=2):
    n_tok = x2.shape[0]
    assert tm % (n_sub * CHUNK) == 0 and (4 * D_A - col0) % D_B == 0
    c0 = (4 * D_A - col0) // D_B
    const2 = lambda i: (0, 0)
    of_layer = lambda *dims: pl.BlockSpec((None,) + dims, lambda i: (layer,) + (0,) * len(dims))
    tile = lambda width, col=0: pl.BlockSpec((tm, width), lambda i: (i, col))
    out_shape = [jax.ShapeDtypeStruct((n_tok, D_MODEL), f32)]
    out_specs = [tile(D_MODEL)]
    if emit_h:
        out_shape.append(jax.ShapeDtypeStruct((n_tok, D_MODEL), bf16))
        out_specs.append(tile(D_MODEL))
    return pl.pallas_call(
        functools.partial(_out_sgu_kernel, tm=tm, n_sub=n_sub, emit_h=emit_h, layer=layer),
        grid=(n_tok // tm,),
        in_specs=[
            tile(D_A), tile(D_C),
            pl.BlockSpec((D_MIX, D_MODEL), const2),
            _whole(post_g),
            tile(D_MODEL),
            _whole(pre_g),
            tile(D_B, c0), tile(D_B, c0 + 1), tile(D_B, c0 + 2),
            _whole(ln_g), _whole(ln_b),
            of_layer(B_GROUPS, CHUNK, CHUNK),
            of_layer(CHUNK, D_B),
        ],
        out_specs=out_specs,
        out_shape=out_shape,
        scratch_shapes=[pltpu.VMEM((tm, D_MIX), bf16)],
        compiler_params=_params("arbitrary"),
        name="out_proj_sgu",
    )(ya, yc, w, post_g, x2, pre_g, z2, z2, z2, ln_g, ln_b, w_s, b_map)


def kernel(x, pre_norm_g, w_in, attn_rpb, sgu_ln_g, sgu_ln_b, sgu_w, sgu_b, conv_w, conv_b,
           conv_ln_g, conv_ln_b, conv_pw_w, conv_pw_b, w_out, post_norm_g):
    bsz, t, d = x.shape
    depth = w_in.shape[0]
    assert d == D_MODEL and t % GRID_W == 0 and t % CHUNK == 0
    n_tok = bsz * t
    x2 = x.reshape(n_tok, d)
    vecs = _attention_bias_vectors(attn_rpb)
    b_map = jnp.repeat(jnp.swapaxes(sgu_b.astype(f32), 1, 2), D_B // B_GROUPS, axis=2)
    as_seq = lambda z: z.reshape(bsz, t, z.shape[-1])
    h, zq = _norm_proj(x2, pre_norm_g.astype(f32), w_in, 0, tn=D_A)
    for l in range(depth):
        col0 = D_A if l == 0 else 0
        zr = _in_proj(h, w_in, l, col0=col0)
        if l > 0:
            zq = zr
        y_a, w_out_bf = _attention(as_seq(zq), as_seq(zr), col0, vecs, w_out, l)
        y_c = _conformer_conv(as_seq(zr), col0, l, conv_w.astype(f32), conv_b.astype(f32), conv_ln_g.astype(f32),
                              conv_ln_b.astype(f32), conv_pw_w.astype(f32), conv_pw_b.astype(f32))
        last = l == depth - 1
        outs = _out_proj_sgu(
            y_a.reshape(n_tok, D_A), y_c.reshape(n_tok, D_C), zr, col0, l, w_out_bf,
            post_norm_g.astype(f32), x2, pre_norm_g.astype(f32), sgu_ln_g.astype(f32), sgu_ln_b.astype(f32),
            sgu_w.astype(f32), b_map, emit_h=not last)
        x2 = outs[0]
        if not last:
            h = outs[1]
    return x2.reshape(bsz, t, d)
```

```python
import functools
import math

import jax
import jax.numpy as jnp
from jax import lax
from jax.experimental import pallas as pl
from jax.experimental.pallas import tpu as pltpu

D_MODEL = 2048
GRID_W = 64
WIN_H = 8
WIN_W = 16
HEAD_DIM = 128
D_A = 1024
N_HEADS = 8
CHUNK = 128
D_B = 512
B_GROUPS = 4
D_C = 512
CONV_W = 31
D_MIX = D_A + D_B + D_C
D_IN = 4 * D_A + 3 * D_B + 3 * D_C
EPS = 1e-6
MASKED = -1e30

LANES = 128
SUBLANES = 8
VMEM_LIMIT = 56 * 1024 * 1024

Q_ROWS = 4
K_ROWS = Q_ROWS + WIN_H
PAIR = 2 * GRID_W
assert PAIR == LANES

CONV_PAD = 16
CONV_SPLIT = 2

f32 = jnp.float32
bf16 = jnp.bfloat16


def _sigmoid(x):
    return 1.0 / (1.0 + jnp.exp2(x * -math.log2(math.e)))


def _silu(x):
    return x * _sigmoid(x)


def _gelu(x):
    return 0.5 * x * (1.0 + lax.erf(x * math.sqrt(0.5)))


def _layer_norm(x, g, b):
    mu = jnp.mean(x, axis=-1, keepdims=True)
    xc = x - mu
    var = jnp.mean(xc * xc, axis=-1, keepdims=True)
    return xc * lax.rsqrt(var + EPS) * g + b


def _rms_norm(x, g):
    ms = jnp.mean(x * x, axis=-1, keepdims=True)
    return x * lax.rsqrt(ms + EPS) * g


def _params(*semantics):
    return pltpu.CompilerParams(dimension_semantics=semantics, vmem_limit_bytes=VMEM_LIMIT)


def _whole(arr):
    return pl.BlockSpec(arr.shape, lambda *_: (0,) * arr.ndim)


def _layer_row(ref, layer):
    return ref.at[layer:layer + 1, :]


def _norm_proj_kernel(x_ref, g_ref, w_ref, h_ref, z_ref, wb_scr, *, layer):
    @pl.when(pl.program_id(0) == 0)
    def _():
        wb_scr[...] = w_ref[...].astype(bf16)

    h = _rms_norm(x_ref[...], _layer_row(g_ref, layer)[...]).astype(bf16)
    h_ref[...] = h
    z_ref[...] = jnp.dot(h, wb_scr[...], preferred_element_type=f32).astype(z_ref.dtype)


def _norm_proj(x2, g, w_in, layer, tm=1024, tn=1024):
    n_tok = x2.shape[0]
    return pl.pallas_call(
        functools.partial(_norm_proj_kernel, layer=layer),
        grid=(n_tok // tm,),
        in_specs=[pl.BlockSpec((tm, D_MODEL), lambda i: (i, 0)),
                  _whole(g),
                  pl.BlockSpec((None, D_MODEL, tn), lambda i: (layer, 0, 0))],
        out_specs=[pl.BlockSpec((tm, D_MODEL), lambda i: (i, 0)),
                   pl.BlockSpec((tm, tn), lambda i: (i, 0))],
        out_shape=[jax.ShapeDtypeStruct((n_tok, D_MODEL), bf16),
                   jax.ShapeDtypeStruct((n_tok, tn), bf16)],
        scratch_shapes=[pltpu.VMEM((D_MODEL, tn), bf16)],
        compiler_params=_params("arbitrary"),
        name="norm_proj",
    )(x2, g, w_in)


def _in_proj_kernel(h_ref, w_ref, z_ref, wb_scr):
    @pl.when(pl.program_id(1) == 0)
    def _():
        wb_scr[...] = w_ref[...].astype(bf16)

    z_ref[...] = jnp.dot(h_ref[...], wb_scr[...], preferred_element_type=f32).astype(z_ref.dtype)


def _in_proj(h, w_in, layer, col0=0, tm=2048, tn=1024):
    n_tok = h.shape[0]
    assert col0 % tn == 0 and (D_IN - col0) % tn == 0
    return pl.pallas_call(
        _in_proj_kernel,
        grid=((D_IN - col0) // tn, n_tok // tm),
        in_specs=[
            pl.BlockSpec((tm, D_MODEL), lambda j, i: (i, 0)),
            pl.BlockSpec((None, D_MODEL, tn), lambda j, i: (layer, 0, col0 // tn + j)),
        ],
        out_specs=pl.BlockSpec((tm, tn), lambda j, i: (i, j)),
        out_shape=jax.ShapeDtypeStruct((n_tok, D_IN - col0), bf16),
        scratch_shapes=[pltpu.VMEM((D_MODEL, tn), bf16)],
        compiler_params=_params("arbitrary", "arbitrary"),
        name="in_proj",
    )(h, w_in)


def _attn_kernel(q_ref, k_ref, v_ref, g_ref, vec_ref, wo_ref, ca_ref, cb_ref, cw_ref, o_ref, wob_ref, c_ref,
                 bias_ref, h_scr, *, rows, heads):
    wob_ref[...] = wo_ref[...].astype(bf16)
    t = rows * GRID_W
    zeros = jnp.zeros((CONV_PAD, LANES), f32)
    h_scr[0:CONV_PAD, :] = zeros
    h_scr[CONV_PAD + t:CONV_PAD + t + CONV_PAD, :] = zeros
    h_scr[CONV_PAD:CONV_PAD + t, :] = ca_ref[0].astype(f32) * _sigmoid(cb_ref[0].astype(f32))
    shift = CONV_PAD - CONV_W // 2
    n_al = (CONV_W + shift + SUBLANES - 1) // SUBLANES

    def conv_tile(t0, tt):
        win = h_scr[t0:t0 + tt + n_al * SUBLANES, :]
        acc = None
        for b in range(SUBLANES):
            part = None
            for a in range(n_al):
                k = SUBLANES * a + b - shift
                if 0 <= k < CONV_W:
                    term = cw_ref[k] * win[SUBLANES * a:SUBLANES * a + tt + SUBLANES]
                    part = term if part is None else part + term
            part = part[b:b + tt]
            acc = part if acc is None else acc + part
        c_ref[0, t0:t0 + tt, :] = acc

    log2e = math.log2(math.e)
    scale = HEAD_DIM ** -0.5 * log2e
    nq = Q_ROWS * GRID_W
    nk = K_ROWS * GRID_W
    w_idx = lax.broadcasted_iota(jnp.int32, (GRID_W, PAIR), 0)
    lane = lax.broadcasted_iota(jnp.int32, (GRID_W, PAIR), 1)
    left_half = lane < GRID_W
    kc = jnp.where(left_half, lane, lane - GRID_W)
    col_start = jnp.clip(w_idx - WIN_W // 2, 0, GRID_W - WIN_W)
    in_win = (kc >= col_start) & (kc < col_start + WIN_W)
    for hh in range(heads):
        for e in range(2 * WIN_H):
            vec = jnp.broadcast_to(vec_ref[hh, e:e + 1, :], (GRID_W, PAIR))
            bias = pltpu.roll(vec, 0, 1, stride=1, stride_axis=0) * log2e
            bias_ref[hh, e] = jnp.where(in_win, bias, MASKED)

    n_blk = rows // Q_ROWS
    first_key_row = lambda blk: min(max(blk * Q_ROWS - WIN_H // 2, 0), rows - K_ROWS)

    head_cols = lambda hh: slice(hh * HEAD_DIM, (hh + 1) * HEAD_DIM)
    items = [(hh, blk) for hh in range(heads) for blk in range(n_blk)]

    def scores(item):
        hh, blk = item
        q0, k0 = blk * nq, first_key_row(blk) * GRID_W
        q = (q_ref[0, q0:q0 + nq, head_cols(hh)].astype(f32) * scale).astype(bf16)
        kw = k_ref[0, k0:k0 + nk, head_cols(hh)]
        return lax.dot_general(q, kw, (((1,), (1,)), ((), ())), preferred_element_type=f32)

    s_next = scores(items[0])
    for pos, (hh, blk) in enumerate(items):
        ks = first_key_row(blk)
        q0, k0 = blk * nq, ks * GRID_W
        vw = v_ref[0, k0:k0 + nk, head_cols(hh)]
        s = s_next
        if pos + 1 < len(items):
            s_next = scores(items[pos + 1])
        for piece in range(CONV_SPLIT):
            conv_tile(pos * (t // len(items)) + piece * (t // len(items) // CONV_SPLIT),
                      t // len(items) // CONV_SPLIT)

        p_rows = []
        for rq in range(Q_ROWS):
            r = blk * Q_ROWS + rq
            rs = min(max(r - WIN_H // 2, 0), rows - WIN_H)
            pieces = {}
            for ip in range(K_ROWS // 2):
                kr = ks + 2 * ip
                ok0, ok1 = rs <= kr < rs + WIN_H, rs <= kr + 1 < rs + WIN_H
                if not (ok0 or ok1):
                    continue
                dr0 = kr - r + (WIN_H - 1)
                sub = s[rq * GRID_W:(rq + 1) * GRID_W, ip * PAIR:(ip + 1) * PAIR]
                if ok0 and ok1:
                    bias = bias_ref[hh, dr0 + 1]
                elif ok0:
                    bias = jnp.where(left_half, bias_ref[hh, dr0 + 1], MASKED)
                else:
                    bias = jnp.where(left_half, MASKED, bias_ref[hh, dr0 + 1])
                pieces[ip] = sub + bias
            m = jnp.max(functools.reduce(jnp.maximum, pieces.values()), axis=-1, keepdims=True)
            pieces = {ip: jnp.exp2(sv - m) for ip, sv in pieces.items()}
            zero = jnp.zeros((GRID_W, PAIR), bf16)
            p_rows.append(jnp.concatenate(
                [pieces[ip].astype(bf16) if ip in pieces else zero for ip in range(K_ROWS // 2)], axis=1))
        p = jnp.concatenate(p_rows, axis=0)
        v_ext = jnp.concatenate([vw, jnp.ones((nk, HEAD_DIM), bf16)], axis=1)
        o = jnp.dot(p, v_ext, preferred_element_type=f32)
        for rq in range(Q_ROWS):
            rows_q = slice(q0 + rq * GRID_W, q0 + (rq + 1) * GRID_W)
            g = g_ref[0, rows_q, head_cols(hh)].astype(f32)
            o_rq = o[rq * GRID_W:(rq + 1) * GRID_W]
            o_rq = o_rq[:, :HEAD_DIM] * (1.0 / o_rq[:, HEAD_DIM:])
            o_ref[0, rows_q, head_cols(hh)] = (o_rq * _silu(g)).astype(o_ref.dtype)


def _attention(zq3, zr3, col0, vecs, w_out, conv_w, layer, heads=2):
    bsz, t, _ = zq3.shape
    rows = t // GRID_W
    assert rows % Q_ROWS == 0 and rows >= K_ROWS and N_HEADS % heads == 0
    groups = N_HEADS // heads
    n_items = heads * (rows // Q_ROWS)
    assert groups * LANES == D_C and t % (n_items * SUBLANES) == 0 and (4 * D_A + 3 * D_B - col0) % LANES == 0
    ca0 = (4 * D_A + 3 * D_B - col0) // LANES
    lane_blk = (1, t, LANES)
    slab = D_MIX // (bsz * groups)
    assert slab * bsz * groups == D_MIX and slab % (2 * SUBLANES) == 0
    blk = (1, t, heads * HEAD_DIM)
    assert (D_A - col0) % blk[2] == 0
    k0 = (D_A - col0) // blk[2]
    return pl.pallas_call(
        functools.partial(_attn_kernel, rows=rows, heads=heads),
        grid=(bsz, groups),
        in_specs=[
            pl.BlockSpec(blk, lambda b, h: (b, 0, h)),
            pl.BlockSpec(blk, lambda b, h: (b, 0, k0 + h)),
            pl.BlockSpec(blk, lambda b, h: (b, 0, k0 + groups + h)),
            pl.BlockSpec(blk, lambda b, h: (b, 0, k0 + 2 * groups + h)),
            pl.BlockSpec((None, heads, 2 * WIN_H, PAIR), lambda b, h: (layer, h, 0, 0)),
            pl.BlockSpec((None, slab, D_MODEL), lambda b, h: (layer, b * groups + h, 0)),
            pl.BlockSpec(lane_blk, lambda b, h: (b, 0, ca0 + h)),
            pl.BlockSpec(lane_blk, lambda b, h: (b, 0, ca0 + groups + h)),
            pl.BlockSpec((None, CONV_W, 1, LANES), lambda b, h: (layer, 0, 0, h)),
        ],
        out_specs=[pl.BlockSpec(blk, lambda b, h: (b, 0, h)),
                   pl.BlockSpec((slab, D_MODEL), lambda b, h: (b * groups + h, 0)),
                   pl.BlockSpec(lane_blk, lambda b, h: (b, 0, h))],
        out_shape=[jax.ShapeDtypeStruct((bsz, t, D_A), bf16),
                   jax.ShapeDtypeStruct((D_MIX, D_MODEL), bf16),
                   jax.ShapeDtypeStruct((bsz, t, D_C), f32)],
        scratch_shapes=[pltpu.VMEM((heads, 2 * WIN_H, GRID_W, PAIR), f32),
                        pltpu.VMEM((t + 2 * CONV_PAD, LANES), f32)],
        compiler_params=_params("arbitrary", "arbitrary"),
        name="nbr_attention",
    )(zq3, zr3, zr3, zr3, vecs, w_out, zr3, zr3, conv_w)


def _attention_bias_vectors(rpb):
    half = WIN_W - 1
    lead = rpb.shape[:-2]
    pad = jnp.full(lead + (1, 2 * WIN_W - 1), MASKED, f32)
    ext = jnp.concatenate([pad, rpb.astype(f32), pad], axis=-2)
    lower, upper = ext[..., :-1, :], ext[..., 1:, :]
    gap = jnp.full(lead + (2 * WIN_H, GRID_W - 2 * half - 1), MASKED, f32)
    return jnp.concatenate([lower[..., half:], gap, upper, gap, lower[..., :half]], axis=-1)


def _conv_kernel(c_ref, g_ref, cb_ref, lng_ref, lnb_ref, pw_ref, pb_ref, o_ref, *, layer):
    cb_ref, lng_ref, lnb_ref, pb_ref = (_layer_row(r, layer) for r in (cb_ref, lng_ref, lnb_ref, pb_ref))
    c = c_ref[...] + cb_ref[...]
    c = _silu(_layer_norm(c, lng_ref[...], lnb_ref[...])).astype(bf16)
    y = jnp.dot(c, pw_ref[...].astype(bf16), preferred_element_type=f32) + pb_ref[...]
    o_ref[...] = (y * _silu(g_ref[...].astype(f32))).astype(o_ref.dtype)


def _conformer_conv(c2, z2, col0, layer, conv_b, ln_g, ln_b, pw_w, pw_b, tm=1024):
    n_tok = c2.shape[0]
    assert n_tok % tm == 0 and (4 * D_A + 3 * D_B - col0) % D_C == 0
    c0 = (4 * D_A + 3 * D_B - col0) // D_C
    return pl.pallas_call(
        functools.partial(_conv_kernel, layer=layer),
        grid=(n_tok // tm,),
        in_specs=[
            pl.BlockSpec((tm, D_C), lambda i: (i, 0)),
            pl.BlockSpec((tm, D_C), lambda i: (i, c0 + 2)),
            _whole(conv_b), _whole(ln_g), _whole(ln_b),
            pl.BlockSpec((None, D_C, D_C), lambda i: (layer, 0, 0)), _whole(pw_b),
        ],
        out_specs=pl.BlockSpec((tm, D_C), lambda i: (i, 0)),
        out_shape=jax.ShapeDtypeStruct((n_tok, D_C), bf16),
        compiler_params=_params("arbitrary"),
        name="conformer_conv",
    )(c2, z2, conv_b, ln_g, ln_b, pw_w, pw_b)


def _spatial_gating(u_ref, v_ref, g_ref, lng_ref, lnb_ref, ws_ref, bs_ref, o_ref, col0, r0, nr):
    gd = D_B // B_GROUPS
    for n in range(r0 // CHUNK, (r0 + nr) // CHUNK):
        rows = slice(n * CHUNK, (n + 1) * CHUNK)
        v = _layer_norm(_gelu(v_ref[rows, :].astype(f32)), lng_ref[...], lnb_ref[...]).astype(bf16)
        for grp in range(B_GROUPS):
            cols = slice(grp * gd, (grp + 1) * gd)
            s = jnp.dot(ws_ref[grp].astype(bf16), v[:, cols], preferred_element_type=f32) + bs_ref[:, cols]
            u = _gelu(u_ref[rows, cols].astype(f32))
            g = g_ref[rows, cols].astype(f32)
            o_ref[rows, col0 + grp * gd:col0 + (grp + 1) * gd] = (u * s * _silu(g)).astype(o_ref.dtype)


def _out_sgu_kernel(ya_ref, yc_ref, w_ref, pg_ref, x_ref, gn_ref,
                    u_ref, v_ref, g_ref, lng_ref, lnb_ref, ws_ref, bs_ref, o_ref, *rest,
                    tm, n_sub, emit_h, layer):
    pg_ref, lng_ref, lnb_ref = (_layer_row(r, layer) for r in (pg_ref, lng_ref, lnb_ref))
    gn_ref = _layer_row(gn_ref, layer + 1 if emit_h else layer)
    cat_scr = rest[-1]
    sub = tm // n_sub

    def project(i):
        rows = slice(i * sub, (i + 1) * sub)
        cat_scr[rows, 0:D_A] = ya_ref[rows, :]
        _spatial_gating(u_ref, v_ref, g_ref, lng_ref, lnb_ref, ws_ref, bs_ref, cat_scr, D_A, i * sub, sub)
        cat_scr[rows, D_A + D_B:D_MIX] = yc_ref[rows, :]
        return jnp.dot(cat_scr[rows, :], w_ref[...], preferred_element_type=f32)

    def finish(i, y):
        rows = slice(i * sub, (i + 1) * sub)
        x_new = x_ref[rows, :] + _rms_norm(y, pg_ref[...])
        o_ref[rows, :] = x_new
        if emit_h:
            h_ref = rest[0]
            h_ref[rows, :] = _rms_norm(x_new, gn_ref[...]).astype(h_ref.dtype)

    y_prev = project(0)
    for i in range(1, n_sub):
        y_cur = project(i)
        finish(i - 1, y_prev)
        y_prev = y_cur
    finish(n_sub - 1, y_prev)


def _out_proj_sgu(ya, yc, z2, col0, layer, w, post_g, x2, pre_g, ln_g, ln_b, w_s, b_map, *, emit_h,
                  tm=512, n_sub=2):
    n_tok = x2.shape[0]
    assert tm % (n_sub * CHUNK) == 0 and (4 * D_A - col0) % D_B == 0
    c0 = (4 * D_A - col0) // D_B
    const2 = lambda i: (0, 0)
    of_layer = lambda *dims: pl.BlockSpec((None,) + dims, lambda i: (layer,) + (0,) * len(dims))
    tile = lambda width, col=0: pl.BlockSpec((tm, width), lambda i: (i, col))
    out_shape = [jax.ShapeDtypeStruct((n_tok, D_MODEL), f32)]
    out_specs = [tile(D_MODEL)]
    if emit_h:
        out_shape.append(jax.ShapeDtypeStruct((n_tok, D_MODEL), bf16))
        out_specs.append(tile(D_MODEL))
    return pl.pallas_call(
        functools.partial(_out_sgu_kernel, tm=tm, n_sub=n_sub, emit_h=emit_h, layer=layer),
        grid=(n_tok // tm,),
        in_specs=[
            tile(D_A), tile(D_C),
            pl.BlockSpec((D_MIX, D_MODEL), const2),
            _whole(post_g),
            tile(D_MODEL),
            _whole(pre_g),
            tile(D_B, c0), tile(D_B, c0 + 1), tile(D_B, c0 + 2),
            _whole(ln_g), _whole(ln_b),
            of_layer(B_GROUPS, CHUNK, CHUNK),
            of_layer(CHUNK, D_B),
        ],
        out_specs=out_specs,
        out_shape=out_shape,
        scratch_shapes=[pltpu.VMEM((tm, D_MIX), bf16)],
        compiler_params=_params("arbitrary"),
        name="out_proj_sgu",
    )(ya, yc, w, post_g, x2, pre_g, z2, z2, z2, ln_g, ln_b, w_s, b_map)


def kernel(x, pre_norm_g, w_in, attn_rpb, sgu_ln_g, sgu_ln_b, sgu_w, sgu_b, conv_w, conv_b,
           conv_ln_g, conv_ln_b, conv_pw_w, conv_pw_b, w_out, post_norm_g):
    bsz, t, d = x.shape
    depth = w_in.shape[0]
    assert d == D_MODEL and t % GRID_W == 0 and t % CHUNK == 0
    n_tok = bsz * t
    x2 = x.reshape(n_tok, d)
    vecs = _attention_bias_vectors(attn_rpb)
    b_map = jnp.repeat(jnp.swapaxes(sgu_b.astype(f32), 1, 2), D_B // B_GROUPS, axis=2)
    as_seq = lambda z: z.reshape(bsz, t, z.shape[-1])
    h, zq = _norm_proj(x2, pre_norm_g.astype(f32), w_in, 0, tn=D_A)
    for l in range(depth):
        col0 = D_A if l == 0 else 0
        zr = _in_proj(h, w_in, l, col0=col0)
        if l > 0:
            zq = zr
        y_a, w_out_bf, c_dw = _attention(as_seq(zq), as_seq(zr), col0, vecs, w_out, conv_w.astype(f32), l)
        y_c = _conformer_conv(c_dw.reshape(n_tok, D_C), zr, col0, l, conv_b.astype(f32), conv_ln_g.astype(f32),
                              conv_ln_b.astype(f32), conv_pw_w.astype(f32), conv_pw_b.astype(f32))
        last = l == depth - 1
        outs = _out_proj_sgu(
            y_a.reshape(n_tok, D_A), y_c, zr, col0, l, w_out_bf,
            post_norm_g.astype(f32), x2, pre_norm_g.astype(f32), sgu_ln_g.astype(f32), sgu_ln_b.astype(f32),
            sgu_w.astype(f32), b_map, emit_h=not last)
        x2 = outs[0]
        if not last:
            h = outs[1]
    return x2.reshape(bsz, t, d)
```

```python
import functools
import math

import jax
import jax.numpy as jnp
import numpy as np
from jax import lax
from jax.experimental import pallas as pl
from jax.experimental.pallas import tpu as pltpu

D_MODEL = 2048
GRID_W = 64
WIN_H = 8
WIN_W = 16
HEAD_DIM = 128
D_A = 1024
N_HEADS = 8
CHUNK = 128
D_B = 512
B_GROUPS = 4
D_C = 512
CONV_W = 31
D_MIX = D_A + D_B + D_C
D_IN = 4 * D_A + 3 * D_B + 3 * D_C
EPS = 1e-6
MASKED = -1e30

LANES = 128
SUBLANES = 8
VMEM_LIMIT = 56 * 1024 * 1024

Q_ROWS = 4
K_ROWS = Q_ROWS + WIN_H
PAIR = 2 * GRID_W
assert PAIR == LANES

CONV_PAD = 16
CONV_TAIL = 24

f32 = jnp.float32
bf16 = jnp.bfloat16


def _sigmoid(x):
    return 1.0 / (1.0 + jnp.exp2(x * -math.log2(math.e)))


def _silu(x):
    return x * _sigmoid(x)


def _gelu(x):
    return 0.5 * x * (1.0 + lax.erf(x * math.sqrt(0.5)))


def _layer_norm(x, g, b):
    mu = jnp.mean(x, axis=-1, keepdims=True)
    xc = x - mu
    var = jnp.mean(xc * xc, axis=-1, keepdims=True)
    return xc * lax.rsqrt(var + EPS) * g + b


def _rms_norm(x, g):
    ms = jnp.mean(x * x, axis=-1, keepdims=True)
    return x * lax.rsqrt(ms + EPS) * g


def _params(*semantics):
    return pltpu.CompilerParams(dimension_semantics=semantics, vmem_limit_bytes=VMEM_LIMIT)


def _whole(arr):
    return pl.BlockSpec(arr.shape, lambda *_: (0,) * arr.ndim)


def _layer_row(ref, layer):
    return ref.at[layer:layer + 1, :]


def _norm_proj_kernel(x_ref, g_ref, w_ref, h_ref, z_ref, wb_scr, *, layer):
    @pl.when(pl.program_id(0) == 0)
    def _():
        wb_scr[...] = w_ref[...].astype(bf16)

    h = _rms_norm(x_ref[...], _layer_row(g_ref, layer)[...]).astype(bf16)
    h_ref[...] = h
    z_ref[...] = jnp.dot(h, wb_scr[...], preferred_element_type=f32).astype(z_ref.dtype)


def _norm_proj(x2, g, w_in, layer, tm=1024, tn=1024):
    n_tok = x2.shape[0]
    return pl.pallas_call(
        functools.partial(_norm_proj_kernel, layer=layer),
        grid=(n_tok // tm,),
        in_specs=[pl.BlockSpec((tm, D_MODEL), lambda i: (i, 0)),
                  _whole(g),
                  pl.BlockSpec((None, D_MODEL, tn), lambda i: (layer, 0, 0))],
        out_specs=[pl.BlockSpec((tm, D_MODEL), lambda i: (i, 0)),
                   pl.BlockSpec((tm, tn), lambda i: (i, 0))],
        out_shape=[jax.ShapeDtypeStruct((n_tok, D_MODEL), bf16),
                   jax.ShapeDtypeStruct((n_tok, tn), bf16)],
        scratch_shapes=[pltpu.VMEM((D_MODEL, tn), bf16)],
        compiler_params=_params("arbitrary"),
        name="norm_proj",
    )(x2, g, w_in)


def _in_proj_kernel(h_ref, w_ref, z_ref, wb_scr):
    @pl.when(pl.program_id(1) == 0)
    def _():
        wb_scr[...] = w_ref[...].astype(bf16)

    z_ref[...] = jnp.dot(h_ref[...], wb_scr[...], preferred_element_type=f32).astype(z_ref.dtype)


def _in_proj(h, w_in, layer, col0=0, tm=2048, tn=1024):
    n_tok = h.shape[0]
    assert col0 % tn == 0 and (D_IN - col0) % tn == 0
    return pl.pallas_call(
        _in_proj_kernel,
        grid=((D_IN - col0) // tn, n_tok // tm),
        in_specs=[
            pl.BlockSpec((tm, D_MODEL), lambda j, i: (i, 0)),
            pl.BlockSpec((None, D_MODEL, tn), lambda j, i: (layer, 0, col0 // tn + j)),
        ],
        out_specs=pl.BlockSpec((tm, tn), lambda j, i: (i, j)),
        out_shape=jax.ShapeDtypeStruct((n_tok, D_IN - col0), bf16),
        scratch_shapes=[pltpu.VMEM((D_MODEL, tn), bf16)],
        compiler_params=_params("arbitrary", "arbitrary"),
        name="in_proj",
    )(h, w_in)


def _attn_kernel(q_ref, k_ref, v_ref, g_ref, vec_ref, wo_ref, ca_ref, cb_ref, cw_ref, sel_ref, o_ref, wob_ref,
                 c_ref, bias_ref, h_scr, p_scr, *, rows, heads):
    wob_ref[...] = wo_ref[...].astype(bf16)
    t = rows * GRID_W
    h_scr[0:CONV_PAD, :] = jnp.zeros((CONV_PAD, LANES), f32)
    h_scr[CONV_PAD + t:CONV_PAD + t + CONV_TAIL, :] = jnp.zeros((CONV_TAIL, LANES), f32)
    h_scr[CONV_PAD:CONV_PAD + t, :] = ca_ref[0].astype(f32) * _sigmoid(cb_ref[0].astype(f32))
    shift = CONV_PAD - CONV_W // 2
    n_al = (CONV_W + shift + SUBLANES - 1) // SUBLANES
    tt = sel_ref.shape[0]
    part_rows = tt + 2 * SUBLANES
    used = (SUBLANES - 1) * part_rows
    assert CONV_PAD + CONV_TAIL >= (n_al - 1) * SUBLANES + part_rows - tt
    p_scr[:, used:, :] = jnp.zeros((p_scr.shape[0], p_scr.shape[1] - used, LANES), bf16)

    def tap_sums(ti):
        t0 = ti * tt
        win = h_scr[t0:t0 + (n_al - 1) * SUBLANES + part_rows, :]
        for b in range(SUBLANES):
            part = None
            for a in range(n_al):
                k = SUBLANES * a + b - shift
                if 0 <= k < CONV_W:
                    term = cw_ref[k] * win[SUBLANES * a:SUBLANES * a + part_rows]
                    part = term if part is None else part + term
            if b == 0:
                c_ref[0, t0:t0 + tt, :] = part[:tt]
            else:
                p_scr[ti, (b - 1) * part_rows:b * part_rows, :] = part.astype(bf16)

    def shift_sum(ti):
        t0 = ti * tt
        c_ref[0, t0:t0 + tt, :] += jnp.dot(sel_ref[...], p_scr[ti], preferred_element_type=f32)

    log2e = math.log2(math.e)
    scale = HEAD_DIM ** -0.5 * log2e
    nq = Q_ROWS * GRID_W
    nk = K_ROWS * GRID_W
    w_idx = lax.broadcasted_iota(jnp.int32, (GRID_W, PAIR), 0)
    lane = lax.broadcasted_iota(jnp.int32, (GRID_W, PAIR), 1)
    left_half = lane < GRID_W
    kc = jnp.where(left_half, lane, lane - GRID_W)
    col_start = jnp.clip(w_idx - WIN_W // 2, 0, GRID_W - WIN_W)
    in_win = (kc >= col_start) & (kc < col_start + WIN_W)
    for hh in range(heads):
        for e in range(2 * WIN_H):
            vec = jnp.broadcast_to(vec_ref[hh, e:e + 1, :], (GRID_W, PAIR))
            bias = pltpu.roll(vec, 0, 1, stride=1, stride_axis=0) * log2e
            bias_ref[hh, e] = jnp.where(in_win, bias, MASKED)

    n_blk = rows // Q_ROWS
    first_key_row = lambda blk: min(max(blk * Q_ROWS - WIN_H // 2, 0), rows - K_ROWS)

    head_cols = lambda hh: slice(hh * HEAD_DIM, (hh + 1) * HEAD_DIM)
    items = [(hh, blk) for hh in range(heads) for blk in range(n_blk)]

    def scores(item):
        hh, blk = item
        q0, k0 = blk * nq, first_key_row(blk) * GRID_W
        q = (q_ref[0, q0:q0 + nq, head_cols(hh)].astype(f32) * scale).astype(bf16)
        kw = k_ref[0, k0:k0 + nk, head_cols(hh)]
        return lax.dot_general(q, kw, (((1,), (1,)), ((), ())), preferred_element_type=f32)

    s_next = scores(items[0])
    for pos, (hh, blk) in enumerate(items):
        ks = first_key_row(blk)
        q0, k0 = blk * nq, ks * GRID_W
        vw = v_ref[0, k0:k0 + nk, head_cols(hh)]
        s = s_next
        if pos + 1 < len(items):
            s_next = scores(items[pos + 1])
        if pos > 0:
            shift_sum(pos - 1)
        tap_sums(pos)

        p_rows = []
        for rq in range(Q_ROWS):
            r = blk * Q_ROWS + rq
            rs = min(max(r - WIN_H // 2, 0), rows - WIN_H)
            pieces = {}
            for ip in range(K_ROWS // 2):
                kr = ks + 2 * ip
                ok0, ok1 = rs <= kr < rs + WIN_H, rs <= kr + 1 < rs + WIN_H
                if not (ok0 or ok1):
                    continue
                dr0 = kr - r + (WIN_H - 1)
                sub = s[rq * GRID_W:(rq + 1) * GRID_W, ip * PAIR:(ip + 1) * PAIR]
                if ok0 and ok1:
                    bias = bias_ref[hh, dr0 + 1]
                elif ok0:
                    bias = jnp.where(left_half, bias_ref[hh, dr0 + 1], MASKED)
                else:
                    bias = jnp.where(left_half, MASKED, bias_ref[hh, dr0 + 1])
                pieces[ip] = sub + bias
            m = jnp.max(functools.reduce(jnp.maximum, pieces.values()), axis=-1, keepdims=True)
            pieces = {ip: jnp.exp2(sv - m) for ip, sv in pieces.items()}
            zero = jnp.zeros((GRID_W, PAIR), bf16)
            p_rows.append(jnp.concatenate(
                [pieces[ip].astype(bf16) if ip in pieces else zero for ip in range(K_ROWS // 2)], axis=1))
        p = jnp.concatenate(p_rows, axis=0)
        v_ext = jnp.concatenate([vw, jnp.ones((nk, HEAD_DIM), bf16)], axis=1)
        o = jnp.dot(p, v_ext, preferred_element_type=f32)
        for rq in range(Q_ROWS):
            rows_q = slice(q0 + rq * GRID_W, q0 + (rq + 1) * GRID_W)
            g = g_ref[0, rows_q, head_cols(hh)].astype(f32)
            o_rq = o[rq * GRID_W:(rq + 1) * GRID_W]
            o_rq = o_rq[:, :HEAD_DIM] * (1.0 / o_rq[:, HEAD_DIM:])
            o_ref[0, rows_q, head_cols(hh)] = (o_rq * _silu(g)).astype(o_ref.dtype)
    shift_sum(len(items) - 1)


def _attention(zq3, zr3, col0, vecs, w_out, conv_w, layer, heads=2):
    bsz, t, _ = zq3.shape
    rows = t // GRID_W
    assert rows % Q_ROWS == 0 and rows >= K_ROWS and N_HEADS % heads == 0
    groups = N_HEADS // heads
    n_items = heads * (rows // Q_ROWS)
    assert groups * LANES == D_C and t % (n_items * SUBLANES) == 0 and (4 * D_A + 3 * D_B - col0) % LANES == 0
    ca0 = (4 * D_A + 3 * D_B - col0) // LANES
    lane_blk = (1, t, LANES)
    tt = t // n_items
    part_rows = tt + 2 * SUBLANES
    depth = -(-(SUBLANES - 1) * part_rows // 256) * 256
    sel = np.zeros((tt, depth), np.float32)
    for b in range(1, SUBLANES):
        sel[np.arange(tt), (b - 1) * part_rows + np.arange(tt) + b] = 1.0
    sel = jnp.asarray(sel, bf16)
    slab = D_MIX // (bsz * groups)
    assert slab * bsz * groups == D_MIX and slab % (2 * SUBLANES) == 0
    blk = (1, t, heads * HEAD_DIM)
    assert (D_A - col0) % blk[2] == 0
    k0 = (D_A - col0) // blk[2]
    return pl.pallas_call(
        functools.partial(_attn_kernel, rows=rows, heads=heads),
        grid=(bsz, groups),
        in_specs=[
            pl.BlockSpec(blk, lambda b, h: (b, 0, h)),
            pl.BlockSpec(blk, lambda b, h: (b, 0, k0 + h)),
            pl.BlockSpec(blk, lambda b, h: (b, 0, k0 + groups + h)),
            pl.BlockSpec(blk, lambda b, h: (b, 0, k0 + 2 * groups + h)),
            pl.BlockSpec((None, heads, 2 * WIN_H, PAIR), lambda b, h: (layer, h, 0, 0)),
            pl.BlockSpec((None, slab, D_MODEL), lambda b, h: (layer, b * groups + h, 0)),
            pl.BlockSpec(lane_blk, lambda b, h: (b, 0, ca0 + h)),
            pl.BlockSpec(lane_blk, lambda b, h: (b, 0, ca0 + groups + h)),
            pl.BlockSpec((None, CONV_W, 1, LANES), lambda b, h: (layer, 0, 0, h)),
            _whole(sel),
        ],
        out_specs=[pl.BlockSpec(blk, lambda b, h: (b, 0, h)),
                   pl.BlockSpec((slab, D_MODEL), lambda b, h: (b * groups + h, 0)),
                   pl.BlockSpec(lane_blk, lambda b, h: (b, 0, h))],
        out_shape=[jax.ShapeDtypeStruct((bsz, t, D_A), bf16),
                   jax.ShapeDtypeStruct((D_MIX, D_MODEL), bf16),
                   jax.ShapeDtypeStruct((bsz, t, D_C), f32)],
        scratch_shapes=[pltpu.VMEM((heads, 2 * WIN_H, GRID_W, PAIR), f32),
                        pltpu.VMEM((t + CONV_PAD + CONV_TAIL, LANES), f32),
                        pltpu.VMEM((n_items, depth, LANES), bf16)],
        compiler_params=_params("arbitrary", "arbitrary"),
        name="nbr_attention",
    )(zq3, zr3, zr3, zr3, vecs, w_out, zr3, zr3, conv_w, sel)


def _attention_bias_vectors(rpb):
    half = WIN_W - 1
    lead = rpb.shape[:-2]
    pad = jnp.full(lead + (1, 2 * WIN_W - 1), MASKED, f32)
    ext = jnp.concatenate([pad, rpb.astype(f32), pad], axis=-2)
    lower, upper = ext[..., :-1, :], ext[..., 1:, :]
    gap = jnp.full(lead + (2 * WIN_H, GRID_W - 2 * half - 1), MASKED, f32)
    return jnp.concatenate([lower[..., half:], gap, upper, gap, lower[..., :half]], axis=-1)


def _conv_kernel(c_ref, g_ref, cb_ref, lng_ref, lnb_ref, pw_ref, pb_ref, o_ref, *, layer):
    cb_ref, lng_ref, lnb_ref, pb_ref = (_layer_row(r, layer) for r in (cb_ref, lng_ref, lnb_ref, pb_ref))
    c = c_ref[...] + cb_ref[...]
    c = _silu(_layer_norm(c, lng_ref[...], lnb_ref[...])).astype(bf16)
    y = jnp.dot(c, pw_ref[...].astype(bf16), preferred_element_type=f32) + pb_ref[...]
    o_ref[...] = (y * _silu(g_ref[...].astype(f32))).astype(o_ref.dtype)


def _conformer_conv(c2, z2, col0, layer, conv_b, ln_g, ln_b, pw_w, pw_b, tm=1024):
    n_tok = c2.shape[0]
    assert n_tok % tm == 0 and (4 * D_A + 3 * D_B - col0) % D_C == 0
    c0 = (4 * D_A + 3 * D_B - col0) // D_C
    return pl.pallas_call(
        functools.partial(_conv_kernel, layer=layer),
        grid=(n_tok // tm,),
        in_specs=[
            pl.BlockSpec((tm, D_C), lambda i: (i, 0)),
            pl.BlockSpec((tm, D_C), lambda i: (i, c0 + 2)),
            _whole(conv_b), _whole(ln_g), _whole(ln_b),
            pl.BlockSpec((None, D_C, D_C), lambda i: (layer, 0, 0)), _whole(pw_b),
        ],
        out_specs=pl.BlockSpec((tm, D_C), lambda i: (i, 0)),
        out_shape=jax.ShapeDtypeStruct((n_tok, D_C), bf16),
        compiler_params=_params("arbitrary"),
        name="conformer_conv",
    )(c2, z2, conv_b, ln_g, ln_b, pw_w, pw_b)


def _spatial_gating(u_ref, v_ref, g_ref, lng_ref, lnb_ref, ws_ref, bs_ref, o_ref, col0, r0, nr):
    gd = D_B // B_GROUPS
    for n in range(r0 // CHUNK, (r0 + nr) // CHUNK):
        rows = slice(n * CHUNK, (n + 1) * CHUNK)
        v = _layer_norm(_gelu(v_ref[rows, :].astype(f32)), lng_ref[...], lnb_ref[...]).astype(bf16)
        for grp in range(B_GROUPS):
            cols = slice(grp * gd, (grp + 1) * gd)
            s = jnp.dot(ws_ref[grp].astype(bf16), v[:, cols], preferred_element_type=f32) + bs_ref[:, cols]
            u = _gelu(u_ref[rows, cols].astype(f32))
            g = g_ref[rows, cols].astype(f32)
            o_ref[rows, col0 + grp * gd:col0 + (grp + 1) * gd] = (u * s * _silu(g)).astype(o_ref.dtype)


def _out_sgu_kernel(ya_ref, yc_ref, w_ref, pg_ref, x_ref, gn_ref,
                    u_ref, v_ref, g_ref, lng_ref, lnb_ref, ws_ref, bs_ref, o_ref, *rest,
                    tm, n_sub, emit_h, layer):
    pg_ref, lng_ref, lnb_ref = (_layer_row(r, layer) for r in (pg_ref, lng_ref, lnb_ref))
    gn_ref = _layer_row(gn_ref, layer + 1 if emit_h else layer)
    cat_scr = rest[-1]
    sub = tm // n_sub

    def project(i):
        rows = slice(i * sub, (i + 1) * sub)
        cat_scr[rows, 0:D_A] = ya_ref[rows, :]
        _spatial_gating(u_ref, v_ref, g_ref, lng_ref, lnb_ref, ws_ref, bs_ref, cat_scr, D_A, i * sub, sub)
        cat_scr[rows, D_A + D_B:D_MIX] = yc_ref[rows, :]
        return jnp.dot(cat_scr[rows, :], w_ref[...], preferred_element_type=f32)

    def finish(i, y):
        rows = slice(i * sub, (i + 1) * sub)
        x_new = x_ref[rows, :] + _rms_norm(y, pg_ref[...])
        o_ref[rows, :] = x_new
        if emit_h:
            h_ref = rest[0]
            h_ref[rows, :] = _rms_norm(x_new, gn_ref[...]).astype(h_ref.dtype)

    y_prev = project(0)
    for i in range(1, n_sub):
        y_cur = project(i)
        finish(i - 1, y_prev)
        y_prev = y_cur
    finish(n_sub - 1, y_prev)


def _out_proj_sgu(ya, yc, z2, col0, layer, w, post_g, x2, pre_g, ln_g, ln_b, w_s, b_map, *, emit_h,
                  tm=512, n_sub=2):
    n_tok = x2.shape[0]
    assert tm % (n_sub * CHUNK) == 0 and (4 * D_A - col0) % D_B == 0
    c0 = (4 * D_A - col0) // D_B
    const2 = lambda i: (0, 0)
    of_layer = lambda *dims: pl.BlockSpec((None,) + dims, lambda i: (layer,) + (0,) * len(dims))
    tile = lambda width, col=0: pl.BlockSpec((tm, width), lambda i: (i, col))
    out_shape = [jax.ShapeDtypeStruct((n_tok, D_MODEL), f32)]
    out_specs = [tile(D_MODEL)]
    if emit_h:
        out_shape.append(jax.ShapeDtypeStruct((n_tok, D_MODEL), bf16))
        out_specs.append(tile(D_MODEL))
    return pl.pallas_call(
        functools.partial(_out_sgu_kernel, tm=tm, n_sub=n_sub, emit_h=emit_h, layer=layer),
        grid=(n_tok // tm,),
        in_specs=[
            tile(D_A), tile(D_C),
            pl.BlockSpec((D_MIX, D_MODEL), const2),
            _whole(post_g),
            tile(D_MODEL),
            _whole(pre_g),
            tile(D_B, c0), tile(D_B, c0 + 1), tile(D_B, c0 + 2),
            _whole(ln_g), _whole(ln_b),
            of_layer(B_GROUPS, CHUNK, CHUNK),
            of_layer(CHUNK, D_B),
        ],
        out_specs=out_specs,
        out_shape=out_shape,
        scratch_shapes=[pltpu.VMEM((tm, D_MIX), bf16)],
        compiler_params=_params("arbitrary"),
        name="out_proj_sgu",
    )(ya, yc, w, post_g, x2, pre_g, z2, z2, z2, ln_g, ln_b, w_s, b_map)


def kernel(x, pre_norm_g, w_in, attn_rpb, sgu_ln_g, sgu_ln_b, sgu_w, sgu_b, conv_w, conv_b,
           conv_ln_g, conv_ln_b, conv_pw_w, conv_pw_b, w_out, post_norm_g):
    bsz, t, d = x.shape
    depth = w_in.shape[0]
    assert d == D_MODEL and t % GRID_W == 0 and t % CHUNK == 0
    n_tok = bsz * t
    x2 = x.reshape(n_tok, d)
    vecs = _attention_bias_vectors(attn_rpb)
    b_map = jnp.repeat(jnp.swapaxes(sgu_b.astype(f32), 1, 2), D_B // B_GROUPS, axis=2)
    as_seq = lambda z: z.reshape(bsz, t, z.shape[-1])
    h, zq = _norm_proj(x2, pre_norm_g.astype(f32), w_in, 0, tn=D_A)
    for l in range(depth):
        col0 = D_A if l == 0 else 0
        zr = _in_proj(h, w_in, l, col0=col0)
        if l > 0:
            zq = zr
        y_a, w_out_bf, c_dw = _attention(as_seq(zq), as_seq(zr), col0, vecs, w_out, conv_w.astype(f32), l)
        y_c = _conformer_conv(c_dw.reshape(n_tok, D_C), zr, col0, l, conv_b.astype(f32), conv_ln_g.astype(f32),
                              conv_ln_b.astype(f32), conv_pw_w.astype(f32), conv_pw_b.astype(f32))
        last = l == depth - 1
        outs = _out_proj_sgu(
            y_a.reshape(n_tok, D_A), y_c, zr, col0, l, w_out_bf,
            post_norm_g.astype(f32), x2, pre_norm_g.astype(f32), sgu_ln_g.astype(f32), sgu_ln_b.astype(f32),
            sgu_w.astype(f32), b_map, emit_h=not last)
        x2 = outs[0]
        if not last:
            h = outs[1]
    return x2.reshape(bsz, t, d)
```

```python
import functools
import math

import jax
import jax.numpy as jnp
from jax import lax
from jax.experimental import pallas as pl
from jax.experimental.pallas import tpu as pltpu

D_MODEL = 2048
GRID_W = 64
WIN_H = 8
WIN_W = 16
HEAD_DIM = 128
D_A = 1024
N_HEADS = 8
CHUNK = 128
D_B = 512
B_GROUPS = 4
D_C = 512
CONV_W = 31
D_MIX = D_A + D_B + D_C
D_IN = 4 * D_A + 3 * D_B + 3 * D_C
EPS = 1e-6
MASKED = -1e30

LANES = 128
SUBLANES = 8
VMEM_LIMIT = 56 * 1024 * 1024

Q_ROWS = 4
K_ROWS = Q_ROWS + WIN_H
PAIR = 2 * GRID_W
assert PAIR == LANES

CONV_PAD = 16

f32 = jnp.float32
bf16 = jnp.bfloat16


def _sigmoid(x):
    return 1.0 / (1.0 + jnp.exp2(x * -math.log2(math.e)))


def _silu(x):
    return x * _sigmoid(x)


def _gelu(x):
    return 0.5 * x * (1.0 + lax.erf(x * math.sqrt(0.5)))


def _layer_norm(x, g, b):
    mu = jnp.mean(x, axis=-1, keepdims=True)
    xc = x - mu
    var = jnp.mean(xc * xc, axis=-1, keepdims=True)
    return xc * lax.rsqrt(var + EPS) * g + b


def _rms_norm(x, g):
    ms = jnp.mean(x * x, axis=-1, keepdims=True)
    return x * lax.rsqrt(ms + EPS) * g


def _params(*semantics):
    return pltpu.CompilerParams(dimension_semantics=semantics, vmem_limit_bytes=VMEM_LIMIT)


def _whole(arr):
    return pl.BlockSpec(arr.shape, lambda *_: (0,) * arr.ndim)


def _layer_row(ref, layer):
    return ref.at[layer:layer + 1, :]


def _norm_proj_kernel(x_ref, g_ref, w_ref, h_ref, z_ref, wb_scr, *, layer):
    @pl.when(pl.program_id(0) == 0)
    def _():
        wb_scr[...] = w_ref[...].astype(bf16)

    h = _rms_norm(x_ref[...], _layer_row(g_ref, layer)[...]).astype(bf16)
    h_ref[...] = h
    z_ref[...] = jnp.dot(h, wb_scr[...], preferred_element_type=f32).astype(z_ref.dtype)


def _norm_proj(x2, g, w_in, layer, tm=1024, tn=1024):
    n_tok = x2.shape[0]
    return pl.pallas_call(
        functools.partial(_norm_proj_kernel, layer=layer),
        grid=(n_tok // tm,),
        in_specs=[pl.BlockSpec((tm, D_MODEL), lambda i: (i, 0)),
                  _whole(g),
                  pl.BlockSpec((None, D_MODEL, tn), lambda i: (layer, 0, 0))],
        out_specs=[pl.BlockSpec((tm, D_MODEL), lambda i: (i, 0)),
                   pl.BlockSpec((tm, tn), lambda i: (i, 0))],
        out_shape=[jax.ShapeDtypeStruct((n_tok, D_MODEL), bf16),
                   jax.ShapeDtypeStruct((n_tok, tn), bf16)],
        scratch_shapes=[pltpu.VMEM((D_MODEL, tn), bf16)],
        compiler_params=_params("arbitrary"),
        name="norm_proj",
    )(x2, g, w_in)


def _in_proj_kernel(h_ref, w_ref, z_ref, wb_scr):
    @pl.when(pl.program_id(1) == 0)
    def _():
        wb_scr[...] = w_ref[...].astype(bf16)

    z_ref[...] = jnp.dot(h_ref[...], wb_scr[...], preferred_element_type=f32).astype(z_ref.dtype)


def _in_proj(h, w_in, layer, col0=0, tm=2048, tn=1024):
    n_tok = h.shape[0]
    assert col0 % tn == 0 and (D_IN - col0) % tn == 0
    return pl.pallas_call(
        _in_proj_kernel,
        grid=((D_IN - col0) // tn, n_tok // tm),
        in_specs=[
            pl.BlockSpec((tm, D_MODEL), lambda j, i: (i, 0)),
            pl.BlockSpec((None, D_MODEL, tn), lambda j, i: (layer, 0, col0 // tn + j)),
        ],
        out_specs=pl.BlockSpec((tm, tn), lambda j, i: (i, j)),
        out_shape=jax.ShapeDtypeStruct((n_tok, D_IN - col0), bf16),
        scratch_shapes=[pltpu.VMEM((D_MODEL, tn), bf16)],
        compiler_params=_params("arbitrary", "arbitrary"),
        name="in_proj",
    )(h, w_in)


def _attn_kernel(q_ref, k_ref, v_ref, g_ref, vec_ref, wo_ref, o_ref, wob_ref, bias_ref, *, rows, heads):
    wob_ref[...] = wo_ref[...].astype(bf16)
    log2e = math.log2(math.e)
    scale = HEAD_DIM ** -0.5 * log2e
    nq = Q_ROWS * GRID_W
    nk = K_ROWS * GRID_W
    w_idx = lax.broadcasted_iota(jnp.int32, (GRID_W, PAIR), 0)
    lane = lax.broadcasted_iota(jnp.int32, (GRID_W, PAIR), 1)
    left_half = lane < GRID_W
    kc = jnp.where(left_half, lane, lane - GRID_W)
    col_start = jnp.clip(w_idx - WIN_W // 2, 0, GRID_W - WIN_W)
    in_win = (kc >= col_start) & (kc < col_start + WIN_W)
    for hh in range(heads):
        for e in range(2 * WIN_H):
            vec = jnp.broadcast_to(vec_ref[hh, e:e + 1, :], (GRID_W, PAIR))
            bias = pltpu.roll(vec, 0, 1, stride=1, stride_axis=0) * log2e
            bias_ref[hh, e] = jnp.where(in_win, bias, MASKED)

    n_blk = rows // Q_ROWS
    first_key_row = lambda blk: min(max(blk * Q_ROWS - WIN_H // 2, 0), rows - K_ROWS)

    head_cols = lambda hh: slice(hh * HEAD_DIM, (hh + 1) * HEAD_DIM)
    items = [(hh, blk) for hh in range(heads) for blk in range(n_blk)]

    def scores(item):
        hh, blk = item
        q0, k0 = blk * nq, first_key_row(blk) * GRID_W
        q = (q_ref[0, q0:q0 + nq, head_cols(hh)].astype(f32) * scale).astype(bf16)
        kw = k_ref[0, k0:k0 + nk, head_cols(hh)]
        return lax.dot_general(q, kw, (((1,), (1,)), ((), ())), preferred_element_type=f32)

    s_next = scores(items[0])
    for pos, (hh, blk) in enumerate(items):
        ks = first_key_row(blk)
        q0, k0 = blk * nq, ks * GRID_W
        vw = v_ref[0, k0:k0 + nk, head_cols(hh)]
        s = s_next
        if pos + 1 < len(items):
            s_next = scores(items[pos + 1])

        p_rows = []
        for rq in range(Q_ROWS):
            r = blk * Q_ROWS + rq
            rs = min(max(r - WIN_H // 2, 0), rows - WIN_H)
            pieces = {}
            for ip in range(K_ROWS // 2):
                kr = ks + 2 * ip
                ok0, ok1 = rs <= kr < rs + WIN_H, rs <= kr + 1 < rs + WIN_H
                if not (ok0 or ok1):
                    continue
                dr0 = kr - r + (WIN_H - 1)
                sub = s[rq * GRID_W:(rq + 1) * GRID_W, ip * PAIR:(ip + 1) * PAIR]
                if ok0 and ok1:
                    bias = bias_ref[hh, dr0 + 1]
                elif ok0:
                    bias = jnp.where(left_half, bias_ref[hh, dr0 + 1], MASKED)
                else:
                    bias = jnp.where(left_half, MASKED, bias_ref[hh, dr0 + 1])
                pieces[ip] = sub + bias
            m = jnp.max(functools.reduce(jnp.maximum, pieces.values()), axis=-1, keepdims=True)
            pieces = {ip: jnp.exp2(sv - m) for ip, sv in pieces.items()}
            zero = jnp.zeros((GRID_W, PAIR), bf16)
            p_rows.append(jnp.concatenate(
                [pieces[ip].astype(bf16) if ip in pieces else zero for ip in range(K_ROWS // 2)], axis=1))
        p = jnp.concatenate(p_rows, axis=0)
        v_ext = jnp.concatenate([vw, jnp.ones((nk, HEAD_DIM), bf16)], axis=1)
        o = jnp.dot(p, v_ext, preferred_element_type=f32)
        for rq in range(Q_ROWS):
            rows_q = slice(q0 + rq * GRID_W, q0 + (rq + 1) * GRID_W)
            g = g_ref[0, rows_q, head_cols(hh)].astype(f32)
            o_rq = o[rq * GRID_W:(rq + 1) * GRID_W]
            o_rq = o_rq[:, :HEAD_DIM] * (1.0 / o_rq[:, HEAD_DIM:])
            o_ref[0, rows_q, head_cols(hh)] = (o_rq * _silu(g)).astype(o_ref.dtype)


def _attention(zq3, zr3, col0, vecs, w_out, layer, heads=2):
    bsz, t, _ = zq3.shape
    rows = t // GRID_W
    assert rows % Q_ROWS == 0 and rows >= K_ROWS and N_HEADS % heads == 0
    groups = N_HEADS // heads
    slab = D_MIX // (bsz * groups)
    assert slab * bsz * groups == D_MIX and slab % (2 * SUBLANES) == 0
    blk = (1, t, heads * HEAD_DIM)
    assert (D_A - col0) % blk[2] == 0
    k0 = (D_A - col0) // blk[2]
    return pl.pallas_call(
        functools.partial(_attn_kernel, rows=rows, heads=heads),
        grid=(bsz, groups),
        in_specs=[
            pl.BlockSpec(blk, lambda b, h: (b, 0, h)),
            pl.BlockSpec(blk, lambda b, h: (b, 0, k0 + h)),
            pl.BlockSpec(blk, lambda b, h: (b, 0, k0 + groups + h)),
            pl.BlockSpec(blk, lambda b, h: (b, 0, k0 + 2 * groups + h)),
            pl.BlockSpec((None, heads, 2 * WIN_H, PAIR), lambda b, h: (layer, h, 0, 0)),
            pl.BlockSpec((None, slab, D_MODEL), lambda b, h: (layer, b * groups + h, 0)),
        ],
        out_specs=[pl.BlockSpec(blk, lambda b, h: (b, 0, h)),
                   pl.BlockSpec((slab, D_MODEL), lambda b, h: (b * groups + h, 0))],
        out_shape=[jax.ShapeDtypeStruct((bsz, t, D_A), bf16),
                   jax.ShapeDtypeStruct((D_MIX, D_MODEL), bf16)],
        scratch_shapes=[pltpu.VMEM((heads, 2 * WIN_H, GRID_W, PAIR), f32)],
        compiler_params=_params("arbitrary", "arbitrary"),
        name="nbr_attention",
    )(zq3, zr3, zr3, zr3, vecs, w_out)


def _attention_bias_vectors(rpb):
    half = WIN_W - 1
    lead = rpb.shape[:-2]
    pad = jnp.full(lead + (1, 2 * WIN_W - 1), MASKED, f32)
    ext = jnp.concatenate([pad, rpb.astype(f32), pad], axis=-2)
    lower, upper = ext[..., :-1, :], ext[..., 1:, :]
    gap = jnp.full(lead + (2 * WIN_H, GRID_W - 2 * half - 1), MASKED, f32)
    return jnp.concatenate([lower[..., half:], gap, upper, gap, lower[..., :half]], axis=-1)


def _conv_kernel(a_ref, b_ref, g_ref, cw_ref, cb_ref, lng_ref, lnb_ref, pw_ref, pb_ref, o_ref,
                 h_scr, c_scr, *, t, tt, tm, layer):
    cb_ref, lng_ref, lnb_ref, pb_ref = (_layer_row(r, layer) for r in (cb_ref, lng_ref, lnb_ref, pb_ref))
    zeros = jnp.zeros((CONV_PAD, D_C), f32)
    h_scr[pl.ds(0, CONV_PAD), :] = zeros
    h_scr[pl.ds(CONV_PAD + t, CONV_PAD), :] = zeros
    h_scr[pl.ds(CONV_PAD, t), :] = a_ref[0].astype(f32) * _sigmoid(b_ref[0].astype(f32))

    shift = CONV_PAD - CONV_W // 2
    n_al = (CONV_W + shift + SUBLANES - 1) // SUBLANES
    win_rows = tt + n_al * SUBLANES

    def conv_tile(ti, carry):
        t0 = pl.multiple_of(ti * tt, tt)
        for cb in range(D_C // LANES):
            cols = slice(cb * LANES, (cb + 1) * LANES)
            win = h_scr[pl.ds(t0, win_rows), cols]
            acc = None
            for b in range(SUBLANES):
                part = None
                for a in range(n_al):
                    k = SUBLANES * a + b - shift
                    if 0 <= k < CONV_W:
                        term = cw_ref[k, :, cols] * win[SUBLANES * a:SUBLANES * a + tt + SUBLANES]
                        part = term if part is None else part + term
                part = part[b:b + tt]
                acc = part if acc is None else acc + part
            c_scr[pl.ds(t0, tt), cols] = acc
        return carry

    lax.fori_loop(0, t // tt, conv_tile, 0)

    def post_tile(ti, carry):
        t0 = pl.multiple_of(ti * tm, tm)
        c = c_scr[pl.ds(t0, tm), :] + cb_ref[...]
        c = _silu(_layer_norm(c, lng_ref[...], lnb_ref[...])).astype(bf16)
        y = jnp.dot(c, pw_ref[...].astype(bf16), preferred_element_type=f32) + pb_ref[...]
        g = g_ref[0, pl.ds(t0, tm), :].astype(f32)
        o_ref[0, pl.ds(t0, tm), :] = (y * _silu(g)).astype(o_ref.dtype)
        return carry

    lax.fori_loop(0, t // tm, post_tile, 0)


def _conformer_conv(z3, col0, layer, conv_w, conv_b, ln_g, ln_b, pw_w, pw_b, tt=128, tm=1024):
    bsz, t, _ = z3.shape
    assert t % tt == 0 and t % tm == 0 and (4 * D_A + 3 * D_B - col0) % D_C == 0
    c0 = (4 * D_A + 3 * D_B - col0) // D_C
    blk = (1, t, D_C)
    of_layer = lambda *dims: pl.BlockSpec((None,) + dims, lambda b: (layer,) + (0,) * len(dims))
    return pl.pallas_call(
        functools.partial(_conv_kernel, t=t, tt=tt, tm=tm, layer=layer),
        grid=(bsz,),
        in_specs=[
            pl.BlockSpec(blk, lambda b: (b, 0, c0)),
            pl.BlockSpec(blk, lambda b: (b, 0, c0 + 1)),
            pl.BlockSpec(blk, lambda b: (b, 0, c0 + 2)),
            of_layer(CONV_W, 1, D_C), _whole(conv_b), _whole(ln_g), _whole(ln_b), of_layer(D_C, D_C), _whole(pw_b),
        ],
        out_specs=pl.BlockSpec(blk, lambda b: (b, 0, 0)),
        out_shape=jax.ShapeDtypeStruct((bsz, t, D_C), bf16),
        scratch_shapes=[pltpu.VMEM((t + 2 * CONV_PAD, D_C), f32), pltpu.VMEM((t, D_C), f32)],
        compiler_params=_params("arbitrary"),
        name="conformer_conv",
    )(z3, z3, z3, conv_w, conv_b, ln_g, ln_b, pw_w, pw_b)


def _spatial_gating(u_ref, v_ref, g_ref, lng_ref, lnb_ref, ws_ref, bs_ref, o_ref, col0, r0, nr, bd_scr):
    gd = D_B // B_GROUPS
    w_cat = jnp.concatenate([ws_ref[grp].astype(bf16) for grp in range(B_GROUPS)], axis=1)
    for n in range(r0 // CHUNK, (r0 + nr) // CHUNK):
        rows = slice(n * CHUNK, (n + 1) * CHUNK)
        v = _layer_norm(_gelu(v_ref[rows, :].astype(f32)), lng_ref[...], lnb_ref[...]).astype(bf16)
        for grp in range(B_GROUPS):
            bd_scr[n, grp * CHUNK:(grp + 1) * CHUNK, grp * gd:(grp + 1) * gd] = v[:, grp * gd:(grp + 1) * gd]
        s = jnp.dot(w_cat, bd_scr[n], preferred_element_type=f32) + bs_ref[...]
        u = _gelu(u_ref[rows, :].astype(f32))
        g = g_ref[rows, :].astype(f32)
        o_ref[rows, col0:col0 + D_B] = (u * s * _silu(g)).astype(o_ref.dtype)


def _out_sgu_kernel(ya_ref, yc_ref, w_ref, pg_ref, x_ref, gn_ref,
                    u_ref, v_ref, g_ref, lng_ref, lnb_ref, ws_ref, bs_ref, o_ref, *rest,
                    tm, n_sub, emit_h, layer):
    pg_ref, lng_ref, lnb_ref = (_layer_row(r, layer) for r in (pg_ref, lng_ref, lnb_ref))
    gn_ref = _layer_row(gn_ref, layer + 1 if emit_h else layer)
    cat_scr, bd_scr = rest[-2:]
    sub = tm // n_sub

    @pl.when(pl.program_id(0) == 0)
    def _():
        bd_scr[...] = jnp.zeros(bd_scr.shape, bf16)

    def project(i):
        rows = slice(i * sub, (i + 1) * sub)
        cat_scr[rows, 0:D_A] = ya_ref[rows, :]
        _spatial_gating(u_ref, v_ref, g_ref, lng_ref, lnb_ref, ws_ref, bs_ref, cat_scr, D_A, i * sub, sub, bd_scr)
        cat_scr[rows, D_A + D_B:D_MIX] = yc_ref[rows, :]
        return jnp.dot(cat_scr[rows, :], w_ref[...], preferred_element_type=f32)

    def finish(i, y):
        rows = slice(i * sub, (i + 1) * sub)
        x_new = x_ref[rows, :] + _rms_norm(y, pg_ref[...])
        o_ref[rows, :] = x_new
        if emit_h:
            h_ref = rest[0]
            h_ref[rows, :] = _rms_norm(x_new, gn_ref[...]).astype(h_ref.dtype)

    y_prev = project(0)
    for i in range(1, n_sub):
        y_cur = project(i)
        finish(i - 1, y_prev)
        y_prev = y_cur
    finish(n_sub - 1, y_prev)


def _out_proj_sgu(ya, yc, z2, col0, layer, w, post_g, x2, pre_g, ln_g, ln_b, w_s, b_map, *, emit_h,
                  tm=512, n_sub=2):
    n_tok = x2.shape[0]
    assert tm % (n_sub * CHUNK) == 0 and (4 * D_A - col0) % D_B == 0
    c0 = (4 * D_A - col0) // D_B
    const2 = lambda i: (0, 0)
    of_layer = lambda *dims: pl.BlockSpec((None,) + dims, lambda i: (layer,) + (0,) * len(dims))
    tile = lambda width, col=0: pl.BlockSpec((tm, width), lambda i: (i, col))
    out_shape = [jax.ShapeDtypeStruct((n_tok, D_MODEL), f32)]
    out_specs = [tile(D_MODEL)]
    if emit_h:
        out_shape.append(jax.ShapeDtypeStruct((n_tok, D_MODEL), bf16))
        out_specs.append(tile(D_MODEL))
    return pl.pallas_call(
        functools.partial(_out_sgu_kernel, tm=tm, n_sub=n_sub, emit_h=emit_h, layer=layer),
        grid=(n_tok // tm,),
        in_specs=[
            tile(D_A), tile(D_C),
            pl.BlockSpec((D_MIX, D_MODEL), const2),
            _whole(post_g),
            tile(D_MODEL),
            _whole(pre_g),
            tile(D_B, c0), tile(D_B, c0 + 1), tile(D_B, c0 + 2),
            _whole(ln_g), _whole(ln_b),
            of_layer(B_GROUPS, CHUNK, CHUNK),
            of_layer(CHUNK, D_B),
        ],
        out_specs=out_specs,
        out_shape=out_shape,
        scratch_shapes=[pltpu.VMEM((tm, D_MIX), bf16), pltpu.VMEM((tm // CHUNK, B_GROUPS * CHUNK, D_B), bf16)],
        compiler_params=_params("arbitrary"),
        name="out_proj_sgu",
    )(ya, yc, w, post_g, x2, pre_g, z2, z2, z2, ln_g, ln_b, w_s, b_map)


def kernel(x, pre_norm_g, w_in, attn_rpb, sgu_ln_g, sgu_ln_b, sgu_w, sgu_b, conv_w, conv_b,
           conv_ln_g, conv_ln_b, conv_pw_w, conv_pw_b, w_out, post_norm_g):
    bsz, t, d = x.shape
    depth = w_in.shape[0]
    assert d == D_MODEL and t % GRID_W == 0 and t % CHUNK == 0
    n_tok = bsz * t
    x2 = x.reshape(n_tok, d)
    vecs = _attention_bias_vectors(attn_rpb)
    b_map = jnp.repeat(jnp.swapaxes(sgu_b.astype(f32), 1, 2), D_B // B_GROUPS, axis=2)
    as_seq = lambda z: z.reshape(bsz, t, z.shape[-1])
    h, zq = _norm_proj(x2, pre_norm_g.astype(f32), w_in, 0, tn=D_A)
    for l in range(depth):
        col0 = D_A if l == 0 else 0
        zr = _in_proj(h, w_in, l, col0=col0)
        if l > 0:
            zq = zr
        y_a, w_out_bf = _attention(as_seq(zq), as_seq(zr), col0, vecs, w_out, l)
        y_c = _conformer_conv(as_seq(zr), col0, l, conv_w.astype(f32), conv_b.astype(f32), conv_ln_g.astype(f32),
                              conv_ln_b.astype(f32), conv_pw_w.astype(f32), conv_pw_b.astype(f32))
        last = l == depth - 1
        outs = _out_proj_sgu(
            y_a.reshape(n_tok, D_A), y_c.reshape(n_tok, D_C), zr, col0, l, w_out_bf,
            post_norm_g.astype(f32), x2, pre_norm_g.astype(f32), sgu_ln_g.astype(f32), sgu_ln_b.astype(f32),
            sgu_w.astype(f32), b_map, emit_h=not last)
        x2 = outs[0]
        if not last:
            h = outs[1]
    return x2.reshape(bsz, t, d)
```

```python
import functools
import math

import jax
import jax.numpy as jnp
from jax import lax
from jax.experimental import pallas as pl
from jax.experimental.pallas import tpu as pltpu

D_MODEL = 2048
GRID_W = 64
WIN_H = 8
WIN_W = 16
HEAD_DIM = 128
D_A = 1024
N_HEADS = 8
CHUNK = 128
D_B = 512
B_GROUPS = 4
D_C = 512
CONV_W = 31
D_MIX = D_A + D_B + D_C
D_IN = 4 * D_A + 3 * D_B + 3 * D_C
EPS = 1e-6
MASKED = -1e30

LANES = 128
SUBLANES = 8
VMEM_LIMIT = 56 * 1024 * 1024

Q_ROWS = 4
K_ROWS = Q_ROWS + WIN_H
PAIR = 2 * GRID_W
assert PAIR == LANES

CONV_PAD = 16

f32 = jnp.float32
bf16 = jnp.bfloat16


def _sigmoid(x):
    return 1.0 / (1.0 + jnp.exp2(x * -math.log2(math.e)))


def _silu(x):
    return x * _sigmoid(x)


def _gelu(x):
    return 0.5 * x * (1.0 + lax.erf(x * math.sqrt(0.5)))


def _layer_norm(x, g, b):
    mu = jnp.mean(x, axis=-1, keepdims=True)
    xc = x - mu
    var = jnp.mean(xc * xc, axis=-1, keepdims=True)
    return xc * lax.rsqrt(var + EPS) * g + b


def _rms_norm(x, g):
    ms = jnp.mean(x * x, axis=-1, keepdims=True)
    return x * lax.rsqrt(ms + EPS) * g


def _params(*semantics):
    return pltpu.CompilerParams(dimension_semantics=semantics, vmem_limit_bytes=VMEM_LIMIT)


def _whole(arr):
    return pl.BlockSpec(arr.shape, lambda *_: (0,) * arr.ndim)


def _layer_row(ref, layer):
    return ref.at[layer:layer + 1, :]


def _norm_proj_kernel(x_ref, g_ref, w_ref, h_ref, z_ref, wb_scr, *, layer):
    @pl.when(pl.program_id(0) == 0)
    def _():
        wb_scr[...] = w_ref[...].astype(bf16)

    h = _rms_norm(x_ref[...], _layer_row(g_ref, layer)[...]).astype(bf16)
    h_ref[...] = h
    z_ref[...] = jnp.dot(h, wb_scr[...], preferred_element_type=f32).astype(z_ref.dtype)


def _norm_proj(x2, g, w_in, layer, tm=1024, tn=1024):
    n_tok = x2.shape[0]
    return pl.pallas_call(
        functools.partial(_norm_proj_kernel, layer=layer),
        grid=(n_tok // tm,),
        in_specs=[pl.BlockSpec((tm, D_MODEL), lambda i: (i, 0)),
                  _whole(g),
                  pl.BlockSpec((None, D_MODEL, tn), lambda i: (layer, 0, 0))],
        out_specs=[pl.BlockSpec((tm, D_MODEL), lambda i: (i, 0)),
                   pl.BlockSpec((tm, tn), lambda i: (i, 0))],
        out_shape=[jax.ShapeDtypeStruct((n_tok, D_MODEL), bf16),
                   jax.ShapeDtypeStruct((n_tok, tn), bf16)],
        scratch_shapes=[pltpu.VMEM((D_MODEL, tn), bf16)],
        compiler_params=_params("arbitrary"),
        name="norm_proj",
    )(x2, g, w_in)


def _in_proj_kernel(h_ref, w_ref, z_ref, wb_scr):
    @pl.when(pl.program_id(1) == 0)
    def _():
        wb_scr[...] = w_ref[...].astype(bf16)

    z_ref[...] = jnp.dot(h_ref[...], wb_scr[...], preferred_element_type=f32).astype(z_ref.dtype)


def _in_proj(h, w_in, layer, col0=0, tm=2048, tn=1024):
    n_tok = h.shape[0]
    assert col0 % tn == 0 and (D_IN - col0) % tn == 0
    return pl.pallas_call(
        _in_proj_kernel,
        grid=((D_IN - col0) // tn, n_tok // tm),
        in_specs=[
            pl.BlockSpec((tm, D_MODEL), lambda j, i: (i, 0)),
            pl.BlockSpec((None, D_MODEL, tn), lambda j, i: (layer, 0, col0 // tn + j)),
        ],
        out_specs=pl.BlockSpec((tm, tn), lambda j, i: (i, j)),
        out_shape=jax.ShapeDtypeStruct((n_tok, D_IN - col0), bf16),
        scratch_shapes=[pltpu.VMEM((D_MODEL, tn), bf16)],
        compiler_params=_params("arbitrary", "arbitrary"),
        name="in_proj",
    )(h, w_in)


def _attn_kernel(q_ref, k_ref, v_ref, g_ref, vec_ref, wo_ref, o_ref, wob_ref, bias_ref, *, rows, heads):
    wob_ref[...] = wo_ref[...].astype(bf16)
    log2e = math.log2(math.e)
    scale = HEAD_DIM ** -0.5 * log2e
    nq = Q_ROWS * GRID_W
    nk = K_ROWS * GRID_W
    w_idx = lax.broadcasted_iota(jnp.int32, (GRID_W, PAIR), 0)
    lane = lax.broadcasted_iota(jnp.int32, (GRID_W, PAIR), 1)
    left_half = lane < GRID_W
    kc = jnp.where(left_half, lane, lane - GRID_W)
    col_start = jnp.clip(w_idx - WIN_W // 2, 0, GRID_W - WIN_W)
    in_win = (kc >= col_start) & (kc < col_start + WIN_W)
    for hh in range(heads):
        for e in range(2 * WIN_H):
            vec = jnp.broadcast_to(vec_ref[hh, e:e + 1, :], (GRID_W, PAIR))
            bias = pltpu.roll(vec, 0, 1, stride=1, stride_axis=0) * log2e
            bias_ref[hh, e] = jnp.where(in_win, bias, MASKED)

    n_blk = rows // Q_ROWS
    first_key_row = lambda blk: min(max(blk * Q_ROWS - WIN_H // 2, 0), rows - K_ROWS)

    head_cols = lambda hh: slice(hh * HEAD_DIM, (hh + 1) * HEAD_DIM)
    items = [(hh, blk) for hh in range(heads) for blk in range(n_blk)]

    def scores(item):
        hh, blk = item
        q0, k0 = blk * nq, first_key_row(blk) * GRID_W
        q = (q_ref[0, q0:q0 + nq, head_cols(hh)].astype(f32) * scale).astype(bf16)
        kw = k_ref[0, k0:k0 + nk, head_cols(hh)]
        return lax.dot_general(q, kw, (((1,), (1,)), ((), ())), preferred_element_type=f32)

    s_next = scores(items[0])
    for pos, (hh, blk) in enumerate(items):
        ks = first_key_row(blk)
        q0, k0 = blk * nq, ks * GRID_W
        vw = v_ref[0, k0:k0 + nk, head_cols(hh)]
        s = s_next
        if pos + 1 < len(items):
            s_next = scores(items[pos + 1])

        p_rows = []
        for rq in range(Q_ROWS):
            r = blk * Q_ROWS + rq
            rs = min(max(r - WIN_H // 2, 0), rows - WIN_H)
            pieces = {}
            for ip in range(K_ROWS // 2):
                kr = ks + 2 * ip
                ok0, ok1 = rs <= kr < rs + WIN_H, rs <= kr + 1 < rs + WIN_H
                if not (ok0 or ok1):
                    continue
                dr0 = kr - r + (WIN_H - 1)
                sub = s[rq * GRID_W:(rq + 1) * GRID_W, ip * PAIR:(ip + 1) * PAIR]
                if ok0 and ok1:
                    bias = bias_ref[hh, dr0 + 1]
                elif ok0:
                    bias = jnp.where(left_half, bias_ref[hh, dr0 + 1], MASKED)
                else:
                    bias = jnp.where(left_half, MASKED, bias_ref[hh, dr0 + 1])
                pieces[ip] = sub + bias
            m = jnp.max(functools.reduce(jnp.maximum, pieces.values()), axis=-1, keepdims=True)
            pieces = {ip: jnp.exp2(sv - m) for ip, sv in pieces.items()}
            zero = jnp.zeros((GRID_W, PAIR), bf16)
            p_rows.append(jnp.concatenate(
                [pieces[ip].astype(bf16) if ip in pieces else zero for ip in range(K_ROWS // 2)], axis=1))
        p = jnp.concatenate(p_rows, axis=0)
        v_ext = jnp.concatenate([vw, jnp.ones((nk, HEAD_DIM), bf16)], axis=1)
        o = jnp.dot(p, v_ext, preferred_element_type=f32)
        for rq in range(Q_ROWS):
            rows_q = slice(q0 + rq * GRID_W, q0 + (rq + 1) * GRID_W)
            g = g_ref[0, rows_q, head_cols(hh)].astype(f32)
            o_rq = o[rq * GRID_W:(rq + 1) * GRID_W]
            o_rq = o_rq[:, :HEAD_DIM] * (1.0 / o_rq[:, HEAD_DIM:])
            o_ref[0, rows_q, head_cols(hh)] = (o_rq * _silu(g)).astype(o_ref.dtype)


def _attention(zq3, zr3, col0, vecs, w_out, layer, heads=2):
    bsz, t, _ = zq3.shape
    rows = t // GRID_W
    assert rows % Q_ROWS == 0 and rows >= K_ROWS and N_HEADS % heads == 0
    groups = N_HEADS // heads
    slab = D_MIX // (bsz * groups)
    assert slab * bsz * groups == D_MIX and slab % (2 * SUBLANES) == 0
    blk = (1, t, heads * HEAD_DIM)
    assert (D_A - col0) % blk[2] == 0
    k0 = (D_A - col0) // blk[2]
    return pl.pallas_call(
        functools.partial(_attn_kernel, rows=rows, heads=heads),
        grid=(bsz, groups),
        in_specs=[
            pl.BlockSpec(blk, lambda b, h: (b, 0, h)),
            pl.BlockSpec(blk, lambda b, h: (b, 0, k0 + h)),
            pl.BlockSpec(blk, lambda b, h: (b, 0, k0 + groups + h)),
            pl.BlockSpec(blk, lambda b, h: (b, 0, k0 + 2 * groups + h)),
            pl.BlockSpec((None, heads, 2 * WIN_H, PAIR), lambda b, h: (layer, h, 0, 0)),
            pl.BlockSpec((None, slab, D_MODEL), lambda b, h: (layer, b * groups + h, 0)),
        ],
        out_specs=[pl.BlockSpec(blk, lambda b, h: (b, 0, h)),
                   pl.BlockSpec((slab, D_MODEL), lambda b, h: (b * groups + h, 0))],
        out_shape=[jax.ShapeDtypeStruct((bsz, t, D_A), bf16),
                   jax.ShapeDtypeStruct((D_MIX, D_MODEL), bf16)],
        scratch_shapes=[pltpu.VMEM((heads, 2 * WIN_H, GRID_W, PAIR), f32)],
        compiler_params=_params("arbitrary", "arbitrary"),
        name="nbr_attention",
    )(zq3, zr3, zr3, zr3, vecs, w_out)


def _attention_bias_vectors(rpb):
    half = WIN_W - 1
    lead = rpb.shape[:-2]
    pad = jnp.full(lead + (1, 2 * WIN_W - 1), MASKED, f32)
    ext = jnp.concatenate([pad, rpb.astype(f32), pad], axis=-2)
    lower, upper = ext[..., :-1, :], ext[..., 1:, :]
    gap = jnp.full(lead + (2 * WIN_H, GRID_W - 2 * half - 1), MASKED, f32)
    return jnp.concatenate([lower[..., half:], gap, upper, gap, lower[..., :half]], axis=-1)


def _conv_kernel(a_ref, b_ref, g_ref, cw_ref, cb_ref, lng_ref, lnb_ref, pw_ref, pb_ref, o_ref,
                 h_scr, c_scr, *, t, tt, tm, layer):
    cb_ref, lng_ref, lnb_ref, pb_ref = (_layer_row(r, layer) for r in (cb_ref, lng_ref, lnb_ref, pb_ref))
    zeros = jnp.zeros((CONV_PAD, D_C), f32)
    h_scr[pl.ds(0, CONV_PAD), :] = zeros
    h_scr[pl.ds(CONV_PAD + t, CONV_PAD), :] = zeros
    h_scr[pl.ds(CONV_PAD, t), :] = a_ref[0].astype(f32) * _sigmoid(b_ref[0].astype(f32))

    shift = CONV_PAD - CONV_W // 2
    n_al = (CONV_W + shift + SUBLANES - 1) // SUBLANES
    win_rows = tt + n_al * SUBLANES

    def conv_tile(ti, carry):
        t0 = pl.multiple_of(ti * tt, tt)
        for cb in range(D_C // LANES):
            cols = slice(cb * LANES, (cb + 1) * LANES)
            win = h_scr[pl.ds(t0, win_rows), cols]
            acc = None
            for b in range(SUBLANES):
                part = None
                for a in range(n_al):
                    k = SUBLANES * a + b - shift
                    if 0 <= k < CONV_W:
                        term = cw_ref[k, :, cols] * win[SUBLANES * a:SUBLANES * a + tt + SUBLANES]
                        part = term if part is None else part + term
                part = part[b:b + tt]
                acc = part if acc is None else acc + part
            c_scr[pl.ds(t0, tt), cols] = acc
        return carry

    lax.fori_loop(0, t // tt, conv_tile, 0)

    def post_tile(ti, carry):
        t0 = pl.multiple_of(ti * tm, tm)
        c = c_scr[pl.ds(t0, tm), :] + cb_ref[...]
        c = _silu(_layer_norm(c, lng_ref[...], lnb_ref[...])).astype(bf16)
        y = jnp.dot(c, pw_ref[...].astype(bf16), preferred_element_type=f32) + pb_ref[...]
        g = g_ref[0, pl.ds(t0, tm), :].astype(f32)
        o_ref[0, pl.ds(t0, tm), :] = (y * _silu(g)).astype(o_ref.dtype)
        return carry

    lax.fori_loop(0, t // tm, post_tile, 0)


def _conformer_conv(z3, col0, layer, conv_w, conv_b, ln_g, ln_b, pw_w, pw_b, tt=128, tm=1024):
    bsz, t, _ = z3.shape
    assert t % tt == 0 and t % tm == 0 and (4 * D_A + 3 * D_B - col0) % D_C == 0
    c0 = (4 * D_A + 3 * D_B - col0) // D_C
    blk = (1, t, D_C)
    of_layer = lambda *dims: pl.BlockSpec((None,) + dims, lambda b: (layer,) + (0,) * len(dims))
    return pl.pallas_call(
        functools.partial(_conv_kernel, t=t, tt=tt, tm=tm, layer=layer),
        grid=(bsz,),
        in_specs=[
            pl.BlockSpec(blk, lambda b: (b, 0, c0)),
            pl.BlockSpec(blk, lambda b: (b, 0, c0 + 1)),
            pl.BlockSpec(blk, lambda b: (b, 0, c0 + 2)),
            of_layer(CONV_W, 1, D_C), _whole(conv_b), _whole(ln_g), _whole(ln_b), of_layer(D_C, D_C), _whole(pw_b),
        ],
        out_specs=pl.BlockSpec(blk, lambda b: (b, 0, 0)),
        out_shape=jax.ShapeDtypeStruct((bsz, t, D_C), bf16),
        scratch_shapes=[pltpu.VMEM((t + 2 * CONV_PAD, D_C), f32), pltpu.VMEM((t, D_C), f32)],
        compiler_params=_params("arbitrary"),
        name="conformer_conv",
    )(z3, z3, z3, conv_w, conv_b, ln_g, ln_b, pw_w, pw_b)


def _spatial_gating(u_ref, v_ref, g_ref, lng_ref, lnb_ref, ws_ref, bs_ref, o_ref, col0, r0, nr, bd_scr):
    gd = D_B // B_GROUPS
    w_cat = jnp.concatenate([ws_ref[grp].astype(bf16) for grp in range(B_GROUPS)], axis=1)
    for n in range(r0 // CHUNK, (r0 + nr) // CHUNK):
        rows = slice(n * CHUNK, (n + 1) * CHUNK)
        v = _layer_norm(_gelu(v_ref[rows, :].astype(f32)), lng_ref[...], lnb_ref[...]).astype(bf16)
        for grp in range(B_GROUPS):
            bd_scr[n, grp * CHUNK:(grp + 1) * CHUNK, grp * gd:(grp + 1) * gd] = v[:, grp * gd:(grp + 1) * gd]
        half = D_B // 2
        s = jnp.concatenate(
            [jnp.dot(w_cat[:, hb * half:(hb + 1) * half], bd_scr[n, hb * half:(hb + 1) * half, hb * half:(hb + 1) * half],
                     preferred_element_type=f32) for hb in range(2)], axis=1) + bs_ref[...]
        u = _gelu(u_ref[rows, :].astype(f32))
        g = g_ref[rows, :].astype(f32)
        o_ref[rows, col0:col0 + D_B] = (u * s * _silu(g)).astype(o_ref.dtype)


def _out_sgu_kernel(ya_ref, yc_ref, w_ref, pg_ref, x_ref, gn_ref,
                    u_ref, v_ref, g_ref, lng_ref, lnb_ref, ws_ref, bs_ref, o_ref, *rest,
                    tm, n_sub, emit_h, layer):
    pg_ref, lng_ref, lnb_ref = (_layer_row(r, layer) for r in (pg_ref, lng_ref, lnb_ref))
    gn_ref = _layer_row(gn_ref, layer + 1 if emit_h else layer)
    cat_scr, bd_scr = rest[-2:]
    sub = tm // n_sub

    @pl.when(pl.program_id(0) == 0)
    def _():
        bd_scr[...] = jnp.zeros(bd_scr.shape, bf16)

    def project(i):
        rows = slice(i * sub, (i + 1) * sub)
        cat_scr[rows, 0:D_A] = ya_ref[rows, :]
        _spatial_gating(u_ref, v_ref, g_ref, lng_ref, lnb_ref, ws_ref, bs_ref, cat_scr, D_A, i * sub, sub, bd_scr)
        cat_scr[rows, D_A + D_B:D_MIX] = yc_ref[rows, :]
        return jnp.dot(cat_scr[rows, :], w_ref[...], preferred_element_type=f32)

    def finish(i, y):
        rows = slice(i * sub, (i + 1) * sub)
        x_new = x_ref[rows, :] + _rms_norm(y, pg_ref[...])
        o_ref[rows, :] = x_new
        if emit_h:
            h_ref = rest[0]
            h_ref[rows, :] = _rms_norm(x_new, gn_ref[...]).astype(h_ref.dtype)

    y_prev = project(0)
    for i in range(1, n_sub):
        y_cur = project(i)
        finish(i - 1, y_prev)
        y_prev = y_cur
    finish(n_sub - 1, y_prev)


def _out_proj_sgu(ya, yc, z2, col0, layer, w, post_g, x2, pre_g, ln_g, ln_b, w_s, b_map, *, emit_h,
                  tm=512, n_sub=2):
    n_tok = x2.shape[0]
    assert tm % (n_sub * CHUNK) == 0 and (4 * D_A - col0) % D_B == 0
    c0 = (4 * D_A - col0) // D_B
    const2 = lambda i: (0, 0)
    of_layer = lambda *dims: pl.BlockSpec((None,) + dims, lambda i: (layer,) + (0,) * len(dims))
    tile = lambda width, col=0: pl.BlockSpec((tm, width), lambda i: (i, col))
    out_shape = [jax.ShapeDtypeStruct((n_tok, D_MODEL), f32)]
    out_specs = [tile(D_MODEL)]
    if emit_h:
        out_shape.append(jax.ShapeDtypeStruct((n_tok, D_MODEL), bf16))
        out_specs.append(tile(D_MODEL))
    return pl.pallas_call(
        functools.partial(_out_sgu_kernel, tm=tm, n_sub=n_sub, emit_h=emit_h, layer=layer),
        grid=(n_tok // tm,),
        in_specs=[
            tile(D_A), tile(D_C),
            pl.BlockSpec((D_MIX, D_MODEL), const2),
            _whole(post_g),
            tile(D_MODEL),
            _whole(pre_g),
            tile(D_B, c0), tile(D_B, c0 + 1), tile(D_B, c0 + 2),
            _whole(ln_g), _whole(ln_b),
            of_layer(B_GROUPS, CHUNK, CHUNK),
            of_layer(CHUNK, D_B),
        ],
        out_specs=out_specs,
        out_shape=out_shape,
        scratch_shapes=[pltpu.VMEM((tm, D_MIX), bf16), pltpu.VMEM((tm // CHUNK, B_GROUPS * CHUNK, D_B), bf16)],
        compiler_params=_params("arbitrary"),
        name="out_proj_sgu",
    )(ya, yc, w, post_g, x2, pre_g, z2, z2, z2, ln_g, ln_b, w_s, b_map)


def kernel(x, pre_norm_g, w_in, attn_rpb, sgu_ln_g, sgu_ln_b, sgu_w, sgu_b, conv_w, conv_b,
           conv_ln_g, conv_ln_b, conv_pw_w, conv_pw_b, w_out, post_norm_g):
    bsz, t, d = x.shape
    depth = w_in.shape[0]
    assert d == D_MODEL and t % GRID_W == 0 and t % CHUNK == 0
    n_tok = bsz * t
    x2 = x.reshape(n_tok, d)
    vecs = _attention_bias_vectors(attn_rpb)
    b_map = jnp.repeat(jnp.swapaxes(sgu_b.astype(f32), 1, 2), D_B // B_GROUPS, axis=2)
    as_seq = lambda z: z.reshape(bsz, t, z.shape[-1])
    h, zq = _norm_proj(x2, pre_norm_g.astype(f32), w_in, 0, tn=D_A)
    for l in range(depth):
        col0 = D_A if l == 0 else 0
        zr = _in_proj(h, w_in, l, col0=col0)
        if l > 0:
            zq = zr
        y_a, w_out_bf = _attention(as_seq(zq), as_seq(zr), col0, vecs, w_out, l)
        y_c = _conformer_conv(as_seq(zr), col0, l, conv_w.astype(f32), conv_b.astype(f32), conv_ln_g.astype(f32),
                              conv_ln_b.astype(f32), conv_pw_w.astype(f32), conv_pw_b.astype(f32))
        last = l == depth - 1
        outs = _out_proj_sgu(
            y_a.reshape(n_tok, D_A), y_c.reshape(n_tok, D_C), zr, col0, l, w_out_bf,
            post_norm_g.astype(f32), x2, pre_norm_g.astype(f32), sgu_ln_g.astype(f32), sgu_ln_b.astype(f32),
            sgu_w.astype(f32), b_map, emit_h=not last)
        x2 = outs[0]
        if not last:
            h = outs[1]
    return x2.reshape(bsz, t, d)
```

```python
import functools
import math

import jax
import jax.numpy as jnp
from jax import lax
from jax.experimental import pallas as pl
from jax.experimental.pallas import tpu as pltpu

D_MODEL = 2048
GRID_W = 64
WIN_H = 8
WIN_W = 16
HEAD_DIM = 128
D_A = 1024
N_HEADS = 8
CHUNK = 128
D_B = 512
B_GROUPS = 4
D_C = 512
CONV_W = 31
D_MIX = D_A + D_B + D_C
D_IN = 4 * D_A + 3 * D_B + 3 * D_C
EPS = 1e-6
MASKED = -1e30

LANES = 128
SUBLANES = 8
VMEM_LIMIT = 56 * 1024 * 1024

Q_ROWS = 4
K_ROWS = Q_ROWS + WIN_H
PAIR = 2 * GRID_W
assert PAIR == LANES

CONV_PAD = 16

f32 = jnp.float32
bf16 = jnp.bfloat16


def _sigmoid(x):
    return 1.0 / (1.0 + jnp.exp2(x * -math.log2(math.e)))


def _silu(x):
    return x * _sigmoid(x)


def _gelu(x):
    return 0.5 * x * (1.0 + lax.erf(x * math.sqrt(0.5)))


def _layer_norm(x, g, b):
    mu = jnp.mean(x, axis=-1, keepdims=True)
    xc = x - mu
    var = jnp.mean(xc * xc, axis=-1, keepdims=True)
    return xc * lax.rsqrt(var + EPS) * g + b


def _rms_norm(x, g):
    ms = jnp.mean(x * x, axis=-1, keepdims=True)
    return x * lax.rsqrt(ms + EPS) * g


def _params(*semantics):
    return pltpu.CompilerParams(dimension_semantics=semantics, vmem_limit_bytes=VMEM_LIMIT)


def _whole(arr):
    return pl.BlockSpec(arr.shape, lambda *_: (0,) * arr.ndim)


def _layer_row(ref, layer):
    return ref.at[layer:layer + 1, :]


def _norm_proj_kernel(x_ref, g_ref, w_ref, h_ref, z_ref, wb_scr, *, layer):
    @pl.when(pl.program_id(0) == 0)
    def _():
        wb_scr[...] = w_ref[...].astype(bf16)

    h = _rms_norm(x_ref[...], _layer_row(g_ref, layer)[...]).astype(bf16)
    h_ref[...] = h
    z_ref[...] = jnp.dot(h, wb_scr[...], preferred_element_type=f32).astype(z_ref.dtype)


def _norm_proj(x2, g, w_in, layer, tm=1024, tn=1024):
    n_tok = x2.shape[0]
    return pl.pallas_call(
        functools.partial(_norm_proj_kernel, layer=layer),
        grid=(n_tok // tm,),
        in_specs=[pl.BlockSpec((tm, D_MODEL), lambda i: (i, 0)),
                  _whole(g),
                  pl.BlockSpec((None, D_MODEL, tn), lambda i: (layer, 0, 0))],
        out_specs=[pl.BlockSpec((tm, D_MODEL), lambda i: (i, 0)),
                   pl.BlockSpec((tm, tn), lambda i: (i, 0))],
        out_shape=[jax.ShapeDtypeStruct((n_tok, D_MODEL), bf16),
                   jax.ShapeDtypeStruct((n_tok, tn), bf16)],
        scratch_shapes=[pltpu.VMEM((D_MODEL, tn), bf16)],
        compiler_params=_params("arbitrary"),
        name="norm_proj",
    )(x2, g, w_in)


def _in_proj_kernel(h_ref, w_ref, z_ref, wb_scr):
    @pl.when(pl.program_id(1) == 0)
    def _():
        wb_scr[...] = w_ref[...].astype(bf16)

    z_ref[...] = jnp.dot(h_ref[...], wb_scr[...], preferred_element_type=f32).astype(z_ref.dtype)


def _in_proj(h, w_in, layer, col0=0, tm=2048, tn=1024):
    n_tok = h.shape[0]
    assert col0 % tn == 0 and (D_IN - col0) % tn == 0
    return pl.pallas_call(
        _in_proj_kernel,
        grid=((D_IN - col0) // tn, n_tok // tm),
        in_specs=[
            pl.BlockSpec((tm, D_MODEL), lambda j, i: (i, 0)),
            pl.BlockSpec((None, D_MODEL, tn), lambda j, i: (layer, 0, col0 // tn + j)),
        ],
        out_specs=pl.BlockSpec((tm, tn), lambda j, i: (i, j)),
        out_shape=jax.ShapeDtypeStruct((n_tok, D_IN - col0), bf16),
        scratch_shapes=[pltpu.VMEM((D_MODEL, tn), bf16)],
        compiler_params=_params("arbitrary", "arbitrary"),
        name="in_proj",
    )(h, w_in)


def _attn_kernel(q_ref, k_ref, v_ref, g_ref, vec_ref, wo_ref, o_ref, wob_ref, bias_ref, *, rows, heads):
    wob_ref[...] = wo_ref[...].astype(bf16)
    log2e = math.log2(math.e)
    scale = HEAD_DIM ** -0.5 * log2e
    nq = Q_ROWS * GRID_W
    nk = K_ROWS * GRID_W
    w_idx = lax.broadcasted_iota(jnp.int32, (GRID_W, PAIR), 0)
    lane = lax.broadcasted_iota(jnp.int32, (GRID_W, PAIR), 1)
    left_half = lane < GRID_W
    kc = jnp.where(left_half, lane, lane - GRID_W)
    col_start = jnp.clip(w_idx - WIN_W // 2, 0, GRID_W - WIN_W)
    in_win = (kc >= col_start) & (kc < col_start + WIN_W)
    for hh in range(heads):
        for e in range(2 * WIN_H):
            vec = jnp.broadcast_to(vec_ref[hh, e:e + 1, :], (GRID_W, PAIR))
            bias = pltpu.roll(vec, 0, 1, stride=1, stride_axis=0) * log2e
            bias_ref[hh, e] = jnp.where(in_win, bias, MASKED)

    n_blk = rows // Q_ROWS
    first_key_row = lambda blk: min(max(blk * Q_ROWS - WIN_H // 2, 0), rows - K_ROWS)

    head_cols = lambda hh: slice(hh * HEAD_DIM, (hh + 1) * HEAD_DIM)
    items = [(hh, blk) for hh in range(heads) for blk in range(n_blk)]

    def scores(item):
        hh, blk = item
        q0, k0 = blk * nq, first_key_row(blk) * GRID_W
        q = (q_ref[0, q0:q0 + nq, head_cols(hh)].astype(f32) * scale).astype(bf16)
        kw = k_ref[0, k0:k0 + nk, head_cols(hh)]
        return lax.dot_general(q, kw, (((1,), (1,)), ((), ())), preferred_element_type=f32)

    s_next = scores(items[0])
    for pos, (hh, blk) in enumerate(items):
        ks = first_key_row(blk)
        q0, k0 = blk * nq, ks * GRID_W
        vw = v_ref[0, k0:k0 + nk, head_cols(hh)]
        s = s_next
        if pos + 1 < len(items):
            s_next = scores(items[pos + 1])

        p_rows = []
        for rq in range(Q_ROWS):
            r = blk * Q_ROWS + rq
            rs = min(max(r - WIN_H // 2, 0), rows - WIN_H)
            pieces = {}
            for ip in range(K_ROWS // 2):
                kr = ks + 2 * ip
                ok0, ok1 = rs <= kr < rs + WIN_H, rs <= kr + 1 < rs + WIN_H
                if not (ok0 or ok1):
                    continue
                dr0 = kr - r + (WIN_H - 1)
                sub = s[rq * GRID_W:(rq + 1) * GRID_W, ip * PAIR:(ip + 1) * PAIR]
                if ok0 and ok1:
                    bias = bias_ref[hh, dr0 + 1]
                elif ok0:
                    bias = jnp.where(left_half, bias_ref[hh, dr0 + 1], MASKED)
                else:
                    bias = jnp.where(left_half, MASKED, bias_ref[hh, dr0 + 1])
                pieces[ip] = sub + bias
            m = jnp.max(functools.reduce(jnp.maximum, pieces.values()), axis=-1, keepdims=True)
            pieces = {ip: jnp.exp2(sv - m) for ip, sv in pieces.items()}
            zero = jnp.zeros((GRID_W, PAIR), bf16)
            p_rows.append(jnp.concatenate(
                [pieces[ip].astype(bf16) if ip in pieces else zero for ip in range(K_ROWS // 2)], axis=1))
        p = jnp.concatenate(p_rows, axis=0)
        v_ext = jnp.concatenate([vw, jnp.ones((nk, HEAD_DIM), bf16)], axis=1)
        o = jnp.dot(p, v_ext, preferred_element_type=f32)
        for rq in range(Q_ROWS):
            rows_q = slice(q0 + rq * GRID_W, q0 + (rq + 1) * GRID_W)
            g = g_ref[0, rows_q, head_cols(hh)].astype(f32)
            o_rq = o[rq * GRID_W:(rq + 1) * GRID_W]
            o_rq = o_rq[:, :HEAD_DIM] * (1.0 / o_rq[:, HEAD_DIM:])
            o_ref[0, rows_q, head_cols(hh)] = (o_rq * _silu(g)).astype(o_ref.dtype)


def _attention(zq3, zr3, col0, vecs, w_out, layer, heads=2):
    bsz, t, _ = zq3.shape
    rows = t // GRID_W
    assert rows % Q_ROWS == 0 and rows >= K_ROWS and N_HEADS % heads == 0
    groups = N_HEADS // heads
    slab = D_MIX // (bsz * groups)
    assert slab * bsz * groups == D_MIX and slab % (2 * SUBLANES) == 0
    blk = (1, t, heads * HEAD_DIM)
    assert (D_A - col0) % blk[2] == 0
    k0 = (D_A - col0) // blk[2]
    return pl.pallas_call(
        functools.partial(_attn_kernel, rows=rows, heads=heads),
        grid=(bsz, groups),
        in_specs=[
            pl.BlockSpec(blk, lambda b, h: (b, 0, h)),
            pl.BlockSpec(blk, lambda b, h: (b, 0, k0 + h)),
            pl.BlockSpec(blk, lambda b, h: (b, 0, k0 + groups + h)),
            pl.BlockSpec(blk, lambda b, h: (b, 0, k0 + 2 * groups + h)),
            pl.BlockSpec((None, heads, 2 * WIN_H, PAIR), lambda b, h: (layer, h, 0, 0)),
            pl.BlockSpec((None, slab, D_MODEL), lambda b, h: (layer, b * groups + h, 0)),
        ],
        out_specs=[pl.BlockSpec(blk, lambda b, h: (b, 0, h)),
                   pl.BlockSpec((slab, D_MODEL), lambda b, h: (b * groups + h, 0))],
        out_shape=[jax.ShapeDtypeStruct((bsz, t, D_A), bf16),
                   jax.ShapeDtypeStruct((D_MIX, D_MODEL), bf16)],
        scratch_shapes=[pltpu.VMEM((heads, 2 * WIN_H, GRID_W, PAIR), f32)],
        compiler_params=_params("arbitrary", "arbitrary"),
        name="nbr_attention",
    )(zq3, zr3, zr3, zr3, vecs, w_out)


def _attention_bias_vectors(rpb):
    half = WIN_W - 1
    lead = rpb.shape[:-2]
    pad = jnp.full(lead + (1, 2 * WIN_W - 1), MASKED, f32)
    ext = jnp.concatenate([pad, rpb.astype(f32), pad], axis=-2)
    lower, upper = ext[..., :-1, :], ext[..., 1:, :]
    gap = jnp.full(lead + (2 * WIN_H, GRID_W - 2 * half - 1), MASKED, f32)
    return jnp.concatenate([lower[..., half:], gap, upper, gap, lower[..., :half]], axis=-1)


def _conv_kernel(a_ref, b_ref, g_ref, cw_ref, cb_ref, lng_ref, lnb_ref, pw_ref, pb_ref, o_ref,
                 h_scr, c_scr, *, t, tt, tm, layer):
    cb_ref, lng_ref, lnb_ref, pb_ref = (_layer_row(r, layer) for r in (cb_ref, lng_ref, lnb_ref, pb_ref))
    zeros = jnp.zeros((CONV_PAD, D_C), f32)
    h_scr[pl.ds(0, CONV_PAD), :] = zeros
    h_scr[pl.ds(CONV_PAD + t, CONV_PAD), :] = zeros
    h_scr[pl.ds(CONV_PAD, t), :] = a_ref[0].astype(f32) * _sigmoid(b_ref[0].astype(f32))

    shift = CONV_PAD - CONV_W // 2
    n_al = (CONV_W + shift + SUBLANES - 1) // SUBLANES
    win_rows = tt + n_al * SUBLANES

    def conv_tile(ti, carry):
        t0 = pl.multiple_of(ti * tt, tt)
        for cb in range(D_C // LANES):
            cols = slice(cb * LANES, (cb + 1) * LANES)
            win = h_scr[pl.ds(t0, win_rows), cols]
            acc = None
            for b in range(SUBLANES):
                part = None
                for a in range(n_al):
                    k = SUBLANES * a + b - shift
                    if 0 <= k < CONV_W:
                        term = cw_ref[k, :, cols] * win[SUBLANES * a:SUBLANES * a + tt + SUBLANES]
                        part = term if part is None else part + term
                part = part[b:b + tt]
                acc = part if acc is None else acc + part
            c_scr[pl.ds(t0, tt), cols] = acc
        return carry

    lax.fori_loop(0, t // tt, conv_tile, 0)

    def post_tile(ti, carry):
        t0 = pl.multiple_of(ti * tm, tm)
        c = c_scr[pl.ds(t0, tm), :] + cb_ref[...]
        c = _silu(_layer_norm(c, lng_ref[...], lnb_ref[...])).astype(bf16)
        y = jnp.dot(c, pw_ref[...].astype(bf16), preferred_element_type=f32) + pb_ref[...]
        g = g_ref[0, pl.ds(t0, tm), :].astype(f32)
        o_ref[0, pl.ds(t0, tm), :] = (y * _silu(g)).astype(o_ref.dtype)
        return carry

    lax.fori_loop(0, t // tm, post_tile, 0)


def _conformer_conv(z3, col0, layer, conv_w, conv_b, ln_g, ln_b, pw_w, pw_b, tt=128, tm=1024):
    bsz, t, _ = z3.shape
    assert t % tt == 0 and t % tm == 0 and (4 * D_A + 3 * D_B - col0) % D_C == 0
    c0 = (4 * D_A + 3 * D_B - col0) // D_C
    blk = (1, t, D_C)
    of_layer = lambda *dims: pl.BlockSpec((None,) + dims, lambda b: (layer,) + (0,) * len(dims))
    return pl.pallas_call(
        functools.partial(_conv_kernel, t=t, tt=tt, tm=tm, layer=layer),
        grid=(bsz,),
        in_specs=[
            pl.BlockSpec(blk, lambda b: (b, 0, c0)),
            pl.BlockSpec(blk, lambda b: (b, 0, c0 + 1)),
            pl.BlockSpec(blk, lambda b: (b, 0, c0 + 2)),
            of_layer(CONV_W, 1, D_C), _whole(conv_b), _whole(ln_g), _whole(ln_b), of_layer(D_C, D_C), _whole(pw_b),
        ],
        out_specs=pl.BlockSpec(blk, lambda b: (b, 0, 0)),
        out_shape=jax.ShapeDtypeStruct((bsz, t, D_C), bf16),
        scratch_shapes=[pltpu.VMEM((t + 2 * CONV_PAD, D_C), f32), pltpu.VMEM((t, D_C), f32)],
        compiler_params=_params("arbitrary"),
        name="conformer_conv",
    )(z3, z3, z3, conv_w, conv_b, ln_g, ln_b, pw_w, pw_b)


def _spatial_gating(u_ref, v_ref, g_ref, lng_ref, lnb_ref, ws_ref, bs_ref, o_ref, col0, r0, nr, bd_scr):
    gd = D_B // B_GROUPS
    w_cat = jnp.concatenate([ws_ref[grp].astype(bf16) for grp in range(B_GROUPS)], axis=1)
    for n in range(r0 // CHUNK, (r0 + nr) // CHUNK):
        rows = slice(n * CHUNK, (n + 1) * CHUNK)
        v = _layer_norm(_gelu(v_ref[rows, :].astype(f32)), lng_ref[...], lnb_ref[...]).astype(bf16)
        for grp in range(B_GROUPS):
            bd_scr[n, grp * CHUNK:(grp + 1) * CHUNK, grp * gd:(grp + 1) * gd] = v[:, grp * gd:(grp + 1) * gd]
        half = D_B // 2
        s = jnp.concatenate(
            [jnp.dot(w_cat[:, hb * half:(hb + 1) * half], bd_scr[n, hb * half:(hb + 1) * half, hb * half:(hb + 1) * half],
                     preferred_element_type=f32) for hb in range(2)], axis=1) + bs_ref[...]
        u = _gelu(u_ref[rows, :].astype(f32))
        g = g_ref[rows, :].astype(f32)
        o_ref[rows, col0:col0 + D_B] = (u * s * _silu(g)).astype(o_ref.dtype)


def _out_sgu_kernel(ya_ref, yc_ref, w_ref, pg_ref, x_ref, gn_ref,
                    u_ref, v_ref, g_ref, lng_ref, lnb_ref, ws_ref, bs_ref, o_ref, *rest,
                    tm, n_sub, emit_h, layer):
    pg_ref, lng_ref, lnb_ref = (_layer_row(r, layer) for r in (pg_ref, lng_ref, lnb_ref))
    gn_ref = _layer_row(gn_ref, layer + 1 if emit_h else layer)
    cat_scr, bd_scr, bmap_scr = rest[-3:]
    sub = tm // n_sub
    b_ref, bs_ref = bs_ref, bmap_scr

    @pl.when(pl.program_id(0) == 0)
    def _():
        bd_scr[...] = jnp.zeros(bd_scr.shape, bf16)
        for grp in range(B_GROUPS):
            row = jnp.broadcast_to(b_ref[layer, grp:grp + 1, :], (CHUNK, CHUNK))
            bmap_scr[:, grp * CHUNK:(grp + 1) * CHUNK] = row.T

    def project(i):
        rows = slice(i * sub, (i + 1) * sub)
        cat_scr[rows, 0:D_A] = ya_ref[rows, :]
        _spatial_gating(u_ref, v_ref, g_ref, lng_ref, lnb_ref, ws_ref, bs_ref, cat_scr, D_A, i * sub, sub, bd_scr)
        cat_scr[rows, D_A + D_B:D_MIX] = yc_ref[rows, :]
        return jnp.dot(cat_scr[rows, :], w_ref[...], preferred_element_type=f32)

    def finish(i, y):
        rows = slice(i * sub, (i + 1) * sub)
        x_new = x_ref[rows, :] + _rms_norm(y, pg_ref[...])
        o_ref[rows, :] = x_new
        if emit_h:
            h_ref = rest[0]
            h_ref[rows, :] = _rms_norm(x_new, gn_ref[...]).astype(h_ref.dtype)

    y_prev = project(0)
    for i in range(1, n_sub):
        y_cur = project(i)
        finish(i - 1, y_prev)
        y_prev = y_cur
    finish(n_sub - 1, y_prev)


def _out_proj_sgu(ya, yc, z2, col0, layer, w, post_g, x2, pre_g, ln_g, ln_b, w_s, b_map, *, emit_h,
                  tm=512, n_sub=2):
    n_tok = x2.shape[0]
    assert tm % (n_sub * CHUNK) == 0 and (4 * D_A - col0) % D_B == 0
    c0 = (4 * D_A - col0) // D_B
    const2 = lambda i: (0, 0)
    of_layer = lambda *dims: pl.BlockSpec((None,) + dims, lambda i: (layer,) + (0,) * len(dims))
    tile = lambda width, col=0: pl.BlockSpec((tm, width), lambda i: (i, col))
    out_shape = [jax.ShapeDtypeStruct((n_tok, D_MODEL), f32)]
    out_specs = [tile(D_MODEL)]
    if emit_h:
        out_shape.append(jax.ShapeDtypeStruct((n_tok, D_MODEL), bf16))
        out_specs.append(tile(D_MODEL))
    return pl.pallas_call(
        functools.partial(_out_sgu_kernel, tm=tm, n_sub=n_sub, emit_h=emit_h, layer=layer),
        grid=(n_tok // tm,),
        in_specs=[
            tile(D_A), tile(D_C),
            pl.BlockSpec((D_MIX, D_MODEL), const2),
            _whole(post_g),
            tile(D_MODEL),
            _whole(pre_g),
            tile(D_B, c0), tile(D_B, c0 + 1), tile(D_B, c0 + 2),
            _whole(ln_g), _whole(ln_b),
            of_layer(B_GROUPS, CHUNK, CHUNK),
            _whole(b_map),
        ],
        out_specs=out_specs,
        out_shape=out_shape,
        scratch_shapes=[pltpu.VMEM((tm, D_MIX), bf16), pltpu.VMEM((tm // CHUNK, B_GROUPS * CHUNK, D_B), bf16),
                        pltpu.VMEM((CHUNK, D_B), f32)],
        compiler_params=_params("arbitrary"),
        name="out_proj_sgu",
    )(ya, yc, w, post_g, x2, pre_g, z2, z2, z2, ln_g, ln_b, w_s, b_map)


def kernel(x, pre_norm_g, w_in, attn_rpb, sgu_ln_g, sgu_ln_b, sgu_w, sgu_b, conv_w, conv_b,
           conv_ln_g, conv_ln_b, conv_pw_w, conv_pw_b, w_out, post_norm_g):
    bsz, t, d = x.shape
    depth = w_in.shape[0]
    assert d == D_MODEL and t % GRID_W == 0 and t % CHUNK == 0
    n_tok = bsz * t
    x2 = x.reshape(n_tok, d)
    vecs = _attention_bias_vectors(attn_rpb)
    b_map = sgu_b.astype(f32)
    as_seq = lambda z: z.reshape(bsz, t, z.shape[-1])
    h, zq = _norm_proj(x2, pre_norm_g.astype(f32), w_in, 0, tn=D_A)
    for l in range(depth):
        col0 = D_A if l == 0 else 0
        zr = _in_proj(h, w_in, l, col0=col0)
        if l > 0:
            zq = zr
        y_a, w_out_bf = _attention(as_seq(zq), as_seq(zr), col0, vecs, w_out, l)
        y_c = _conformer_conv(as_seq(zr), col0, l, conv_w.astype(f32), conv_b.astype(f32), conv_ln_g.astype(f32),
                              conv_ln_b.astype(f32), conv_pw_w.astype(f32), conv_pw_b.astype(f32))
        last = l == depth - 1
        outs = _out_proj_sgu(
            y_a.reshape(n_tok, D_A), y_c.reshape(n_tok, D_C), zr, col0, l, w_out_bf,
            post_norm_g.astype(f32), x2, pre_norm_g.astype(f32), sgu_ln_g.astype(f32), sgu_ln_b.astype(f32),
            sgu_w.astype(f32), b_map, emit_h=not last)
        x2 = outs[0]
        if not last:
            h = outs[1]
    return x2.reshape(bsz, t, d)
```
